```python
import math
import jax, jax.numpy as jnp
from jax import lax
import numpy as np

D_MODEL = 1024
BATCH = 2
SEQ = 8192
DEPTH = 2

N_HEADS = 16
D_NOPE = 64
D_V = 64
D_CQ = 256
D_CKV = 128
IDX_HEADS = 8
IDX_DIM = 64
TOPK_MAX = 256
Q_BLOCK = 128
W_IN_COLS = D_CQ + D_CKV + IDX_DIM + IDX_HEADS
CONV_WIDTH = 31
D_FF = 4 * D_MODEL
NUM_BUCKETS = 32
MAX_EXACT = 16
MAX_DISTANCE = 128
EPS = 1e-6
N_ATTN_LAYERS = (DEPTH + 1) // 2
N_CONV_LAYERS = DEPTH // 2

kernel_name = "dsa_conformer_interleaved_hybrid"


def rmsnorm(x, g):
    xf = x.astype(jnp.float32)
    y = xf * lax.rsqrt(jnp.mean(xf * xf, axis=-1, keepdims=True) + EPS)
    return (y * g.astype(jnp.float32)).astype(x.dtype)


def layernorm(x, g, b):
    xf = x.astype(jnp.float32)
    mu = jnp.mean(xf, axis=-1, keepdims=True)
    var = jnp.mean(jnp.square(xf - mu), axis=-1, keepdims=True)
    y = (xf - mu) * lax.rsqrt(var + EPS)
    return (y * g.astype(jnp.float32) + b.astype(jnp.float32)).astype(x.dtype)


def t5_bucket(dist):
    n = jnp.maximum(dist, 0)
    nf = jnp.maximum(n, 1).astype(jnp.float32)
    large = MAX_EXACT + (jnp.log(nf / MAX_EXACT) / math.log(MAX_DISTANCE / MAX_EXACT)
                         * (NUM_BUCKETS - MAX_EXACT)).astype(jnp.int32)
    large = jnp.minimum(large, NUM_BUCKETS - 1)
    return jnp.where(n < MAX_EXACT, n, large)


def dsa_mla(h, rel_bias, w_in, q_norm, kv_norm, kidx_norm, w_qidx, w_uq, w_uk, w_uv, w_o):
    B, S, _ = h.shape
    k_sel = min(TOPK_MAX, S // 4)
    proj = h @ w_in
    c_q, c_kv, k_idx, w_idx = jnp.split(proj, [D_CQ, D_CQ + D_CKV, D_CQ + D_CKV + IDX_DIM], axis=-1)
    c_q = rmsnorm(c_q, q_norm)
    c_kv = rmsnorm(c_kv, kv_norm)
    k_idx = rmsnorm(k_idx, kidx_norm)
    q_idx = (c_q @ w_qidx).reshape(B, S, IDX_HEADS, IDX_DIM) * (IDX_DIM ** -0.5)
    w_idx = w_idx * (IDX_HEADS ** -0.5)
    q_h = (c_q @ w_uq).reshape(B, S, N_HEADS, D_NOPE)
    q_lat = jnp.einsum('bshd,chd->bshc', q_h, w_uk) * (D_NOPE ** -0.5)
    s_pos = jnp.arange(S, dtype=jnp.int32)

    def block(i):
        q0 = i * Q_BLOCK
        qi = lax.dynamic_slice_in_dim(q_idx, q0, Q_BLOCK, axis=1)
        wi = lax.dynamic_slice_in_dim(w_idx, q0, Q_BLOCK, axis=1)
        ql = lax.dynamic_slice_in_dim(q_lat, q0, Q_BLOCK, axis=1)
        t = q0 + jnp.arange(Q_BLOCK, dtype=jnp.int32)
        score = jnp.einsum('bqh,bqhs->bqs', wi,
                           jax.nn.relu(jnp.einsum('bqhd,bsd->bqhs', qi, k_idx)))
        causal = s_pos[None, None, :] <= t[None, :, None]
        score = jnp.where(causal, score.astype(jnp.float32), -jnp.inf)
        _, idx = lax.top_k(score, k_sel)
        valid = idx <= t[None, :, None]
        kv = jax.vmap(lambda c, j: c[j])(c_kv, idx)
        logits = jnp.einsum('bqhc,bqkc->bqhk', ql, kv).astype(jnp.float32)
        bias = rel_bias[t5_bucket(t[None, :, None] - idx)]
        logits = logits + jnp.moveaxis(bias, -1, 2).astype(jnp.float32)
        logits = jnp.where(valid[:, :, None, :], logits, -jnp.inf)
        p = jax.nn.softmax(logits, axis=-1).astype(kv.dtype)
        o_lat = jnp.einsum('bqhk,bqkc->bqhc', p, kv)
        o = jnp.einsum('bqhc,chd->bqhd', o_lat, w_uv)
        return o.reshape(B, Q_BLOCK, N_HEADS * D_V)

    out = lax.map(block, jnp.arange(S // Q_BLOCK, dtype=jnp.int32))
    out = jnp.transpose(out, (1, 0, 2, 3)).reshape(B, S, N_HEADS * D_V)
    return out @ w_o


def conformer_conv(h, w_pw1, b_pw1, w_dw, b_dw, ln_g, ln_b, w_pw2, b_pw2):
    u = h @ w_pw1 + b_pw1
    a, g = jnp.split(u, 2, axis=-1)
    u = a * jax.nn.sigmoid(g)
    u = lax.conv_general_dilated(u, w_dw[:, None, :], window_strides=(1,),
                                 padding=[(CONV_WIDTH - 1, 0)],
                                 dimension_numbers=('NWC', 'WIO', 'NWC'),
                                 feature_group_count=D_MODEL) + b_dw
    u = layernorm(u, ln_g, ln_b)
    u = jax.nn.silu(u)
    return u @ w_pw2 + b_pw2


def sqrelu_mlp(h, w_up, w_down):
    return jnp.square(jax.nn.relu(h @ w_up)) @ w_down


def setup_inputs(seed: int = 0) -> dict:
    key = jax.random.key(seed)
    ks = iter(jax.random.split(key, 32))
    nrm = lambda shape, scale: jax.random.normal(next(ks), shape, jnp.float32) * scale
    gain = lambda shape: 1.0 + 0.05 * jax.random.normal(next(ks), shape, jnp.float32)
    nA, nB, D = N_ATTN_LAYERS, N_CONV_LAYERS, D_MODEL
    return {
        "x": nrm((BATCH, SEQ, D), 1.0),
        "norm_mix": gain((DEPTH, D)),
        "norm_mlp": gain((DEPTH, D)),
        "norm_final": gain((D,)),
        "rel_bias": nrm((NUM_BUCKETS, N_HEADS), 0.5),
        "attn_w_in": nrm((nA, D, W_IN_COLS), D ** -0.5),
        "attn_q_norm": gain((nA, D_CQ)),
        "attn_kv_norm": gain((nA, D_CKV)),
        "attn_kidx_norm": gain((nA, IDX_DIM)),
        "attn_w_qidx": nrm((nA, D_CQ, IDX_HEADS * IDX_DIM), D_CQ ** -0.5),
        "attn_w_uq": nrm((nA, D_CQ, N_HEADS * D_NOPE), D_CQ ** -0.5),
        "attn_w_uk": nrm((nA, D_CKV, N_HEADS, D_NOPE), D_CKV ** -0.5),
        "attn_w_uv": nrm((nA, D_CKV, N_HEADS, D_V), D_CKV ** -0.5),
        "attn_w_o": nrm((nA, N_HEADS * D_V, D), (N_HEADS * D_V) ** -0.5),
        "conv_w_pw1": nrm((nB, D, 2 * D), D ** -0.5),
        "conv_b_pw1": nrm((nB, 2 * D), 0.02),
        "conv_w_dw": nrm((nB, CONV_WIDTH, D), CONV_WIDTH ** -0.5),
        "conv_b_dw": nrm((nB, D), 0.02),
        "conv_ln_g": gain((nB, D)),
        "conv_ln_b": nrm((nB, D), 0.02),
        "conv_w_pw2": nrm((nB, D, D), D ** -0.5),
        "conv_b_pw2": nrm((nB, D), 0.02),
        "mlp_w_up": nrm((DEPTH, D, D_FF), D ** -0.5),
        "mlp_w_down": nrm((DEPTH, D_FF, D), D_FF ** -0.5),
    }


def reference(x, norm_mix, norm_mlp, norm_final, rel_bias,
              attn_w_in, attn_q_norm, attn_kv_norm, attn_kidx_norm, attn_w_qidx,
              attn_w_uq, attn_w_uk, attn_w_uv, attn_w_o,
              conv_w_pw1, conv_b_pw1, conv_w_dw, conv_b_dw, conv_ln_g, conv_ln_b,
              conv_w_pw2, conv_b_pw2, mlp_w_up, mlp_w_down):
    for i in range(DEPTH):
        h = rmsnorm(x, norm_mix[i])
        j = i // 2
        if i % 2 == 0:
            x = x + dsa_mla(h, rel_bias, attn_w_in[j], attn_q_norm[j], attn_kv_norm[j],
                            attn_kidx_norm[j], attn_w_qidx[j], attn_w_uq[j], attn_w_uk[j],
                            attn_w_uv[j], attn_w_o[j])
        else:
            x = x + conformer_conv(h, conv_w_pw1[j], conv_b_pw1[j], conv_w_dw[j], conv_b_dw[j],
                                   conv_ln_g[j], conv_ln_b[j], conv_w_pw2[j], conv_b_pw2[j])
        h = rmsnorm(x, norm_mlp[i])
        x = x + sqrelu_mlp(h, mlp_w_up[i], mlp_w_down[i])
    return rmsnorm(x, norm_final)
```

```python
import functools
import math

import jax
import jax.numpy as jnp
from jax import lax
from jax.experimental import pallas as pl
from jax.experimental.pallas import tpu as pltpu

F32 = jnp.float32
BF16 = jnp.bfloat16
I32 = jnp.int32

D_MODEL = 1024
N_HEADS = 16
D_NOPE = 64
D_V = 64
D_CQ = 256
D_CKV = 128
IDX_HEADS = 8
IDX_DIM = 64
TOPK_MAX = 256
CONV_WIDTH = 31
NUM_BUCKETS = 32
MAX_EXACT = 16
MAX_DISTANCE = 128
EPS = 1e-6

LANES = 128
TQ = 128
UNIT = 128
SCORE_TILE = 512
UNITS_PER_SCORE_TILE = SCORE_TILE // UNIT
ATT_UNITS = 2
ATT_TK = ATT_UNITS * UNIT
PAD_UNITS = ATT_UNITS - 1
NEG_MASK = -1e30
VMEM_LIMIT = 56 * 1024 * 1024


def _cparams(n_axes):
    return pltpu.CompilerParams(dimension_semantics=("arbitrary",) * n_axes,
                                vmem_limit_bytes=VMEM_LIMIT)


def _rms(x, g):
    return x * lax.rsqrt(jnp.mean(x * x, axis=-1, keepdims=True) + EPS) * g


def _proj_kernel(x_ref, g_ref, win_ref, qn_ref, kvn_ref, kin_ref, wqidx_ref, wuq_ref, wukt_ref,
                 qidx_ref, widx_ref, qlat_ref, kidx_ref, ckv_ref):
    h = _rms(x_ref[...], g_ref[...])
    proj = jnp.dot(h.astype(BF16), win_ref[...], preferred_element_type=F32)
    o1, o2, o3 = D_CQ, D_CQ + D_CKV, D_CQ + D_CKV + IDX_DIM
    cq = _rms(proj[:, :o1], qn_ref[...])
    ckv = _rms(proj[:, o1:o2], kvn_ref[...])
    kid = _rms(proj[:, o2:o3], kin_ref[...])
    widx_ref[...] = proj[:, o3:o3 + IDX_HEADS] * (IDX_HEADS ** -0.5)
    ckv_ref[...] = ckv.astype(BF16)
    kidx_ref[...] = kid.astype(BF16)
    cqb = cq.astype(BF16)
    qidx = jnp.dot(cqb, wqidx_ref[...], preferred_element_type=F32) * (IDX_DIM ** -0.5)
    qidx_ref[...] = qidx.astype(BF16)
    qh = jnp.dot(cqb, wuq_ref[...], preferred_element_type=F32).astype(BF16)
    for hh in range(N_HEADS):
        ql = jnp.dot(qh[:, hh * D_NOPE:(hh + 1) * D_NOPE], wukt_ref[hh],
                     preferred_element_type=F32) * (D_NOPE ** -0.5)
        qlat_ref[:, hh * D_CKV:(hh + 1) * D_CKV] = ql.astype(BF16)


def _proj(x2, g, w_in, qn, kvn, kin, w_qidx, w_uq, w_uk, tm=512):
    n = x2.shape[0]
    ncol = w_in.shape[1]
    npad = -ncol % LANES
    win = jnp.pad(w_in, ((0, 0), (0, npad))).astype(BF16)
    wukt = jnp.transpose(w_uk, (1, 2, 0)).astype(BF16)
    full = lambda a: pl.BlockSpec(a.shape, lambda i: (0,) * a.ndim)
    row = lambda c: pl.BlockSpec((tm, c), lambda i: (i, 0))
    args = (x2, g.reshape(1, -1), win, qn.reshape(1, -1), kvn.reshape(1, -1), kin.reshape(1, -1),
            w_qidx.astype(BF16), w_uq.astype(BF16), wukt)
    return pl.pallas_call(
        _proj_kernel,
        grid=(n // tm,),
        in_specs=[row(D_MODEL)] + [full(a) for a in args[1:]],
        out_specs=[row(IDX_HEADS * IDX_DIM), row(IDX_HEADS), row(N_HEADS * D_CKV), row(IDX_DIM),
                   row(D_CKV)],
        out_shape=[jax.ShapeDtypeStruct((n, IDX_HEADS * IDX_DIM), BF16),
                   jax.ShapeDtypeStruct((n, IDX_HEADS), F32),
                   jax.ShapeDtypeStruct((n, N_HEADS * D_CKV), BF16),
                   jax.ShapeDtypeStruct((n, IDX_DIM), BF16),
                   jax.ShapeDtypeStruct((n, D_CKV), BF16)],
        compiler_params=_cparams(1),
        name="dsa_proj",
    )(*args)


def _attn_kernel(qidx_ref, widx_ref, qlat_ref, kidxt_ref, ckvt_ref, vaug_ref, bias_ref, olat_ref,
                 qs_ref, wb_ref, keys_ref, maskb_ref, a_ref, acc_ref, m_ref, *, k_sel, idx_bits):
    i = pl.program_id(1)
    q0 = i * TQ
    n_rows = N_HEADS * TQ
    lane = lax.broadcasted_iota(I32, (TQ, LANES), 1)
    row = q0 + lax.broadcasted_iota(I32, (TQ, LANES), 0)
    rowc = q0 + lax.broadcasted_iota(I32, (TQ, 1), 0)

    for h in range(IDX_HEADS):
        qs_ref[h * TQ:(h + 1) * TQ, :] = qidx_ref[:, h * IDX_DIM:(h + 1) * IDX_DIM]
        wb_ref[h] = jnp.broadcast_to(widx_ref[:, h:h + 1], (TQ, LANES))
    eye = (lax.broadcasted_iota(I32, (TQ, TQ), 0) == lax.broadcasted_iota(I32, (TQ, TQ), 1))
    eye = jnp.where(eye, 1.0, 0.0).astype(BF16)
    for h in range(N_HEADS):
        a_ref[h * TQ:(h + 1) * TQ, :D_CKV] = qlat_ref[:, h * D_CKV:(h + 1) * D_CKV]
        a_ref[h * TQ:(h + 1) * TQ, D_CKV:] = eye

    n_tiles = i // UNITS_PER_SCORE_TILE + 1

    def score_body(tt, carry):
        qk = jnp.dot(qs_ref[...], kidxt_ref[tt], preferred_element_type=F32)
        for c in range(UNITS_PER_SCORE_TILE):
            sc = jnp.zeros((TQ, LANES), F32)
            for h in range(IDX_HEADS):
                sc = sc + wb_ref[h] * jnp.maximum(
                    qk[h * TQ:(h + 1) * TQ, c * LANES:(c + 1) * LANES], 0.0)
            col = tt * SCORE_TILE + c * LANES + lane
            sc = jnp.where(col <= row, sc, -jnp.inf)
            bits = pltpu.bitcast(sc, I32)
            keys_ref[tt, :, c * LANES:(c + 1) * LANES] = bits ^ ((bits >> 31) & 0x7FFFFFFF)
        return carry

    lax.fori_loop(0, n_tiles, score_body, 0)

    def count(pred):
        def body(tt, acc):
            for c in range(UNITS_PER_SCORE_TILE):
                k = keys_ref[tt, :, c * LANES:(c + 1) * LANES]
                acc = acc + jnp.where(pred(k, tt * SCORE_TILE + c * LANES + lane), 1, 0)
            return acc
        acc = lax.fori_loop(0, n_tiles, body, jnp.zeros((TQ, LANES), I32))
        return jnp.sum(acc, axis=1, keepdims=True)

    trivial = rowc < k_sel

    def bis_cond(st):
        return jnp.logical_and(st[0] < 32, st[3] > 0)

    def bis_body(st):
        step, cand, cntc, _ = st
        trial = cand ^ (jnp.int32(1) << (31 - step))
        tb = jnp.broadcast_to(trial, (TQ, LANES))
        cnt = count(lambda k, col: k >= tb)
        ok = cnt >= k_sel
        cand = jnp.where(ok, trial, cand)
        cntc = jnp.where(ok, cnt, cntc)
        open_rows = jnp.logical_and(cntc != k_sel, jnp.logical_not(trivial))
        return step + 1, cand, cntc, jnp.sum(jnp.where(open_rows, 1, 0))

    int_min = jnp.int32(-2 ** 31)
    _, cand, cntc, _ = lax.while_loop(
        bis_cond, bis_body,
        (jnp.int32(0), jnp.full((TQ, 1), int_min, I32), jnp.full((TQ, 1), 2 ** 30, I32),
         jnp.int32(1)))
    cb = jnp.broadcast_to(cand, (TQ, LANES))

    excess = jnp.logical_and(cntc > k_sel, jnp.logical_not(trivial))
    big_j = jnp.int32(2 ** 30)

    def tie_fn():
        need = k_sel - count(lambda k, col: k > cb)

        def jb(b, jv):
            trial = jv | (jnp.int32(1) << (idx_bits - 1 - b))
            tb = jnp.broadcast_to(trial, (TQ, LANES))
            c = count(lambda k, col: jnp.logical_and(k == cb, col < tb))
            return jnp.where(c < need, trial, jv)

        jv = lax.fori_loop(0, idx_bits, jb, jnp.zeros((TQ, 1), I32))
        return jnp.where(excess, jv, big_j)

    jv = lax.cond(jnp.sum(jnp.where(excess, 1, 0)) > 0, tie_fn,
                  lambda: jnp.full((TQ, 1), big_j, I32))
    jb_ = jnp.broadcast_to(jv, (TQ, LANES))

    def mask_body(tt, carry):
        for c in range(UNITS_PER_SCORE_TILE):
            k = keys_ref[tt, :, c * LANES:(c + 1) * LANES]
            col = tt * SCORE_TILE + c * LANES + lane
            sel = jnp.logical_or(k > cb, jnp.logical_and(k == cb, col <= jb_))
            sel = jnp.logical_and(sel, col <= row)
            maskb_ref[PAD_UNITS + tt * UNITS_PER_SCORE_TILE + c] = (
                jnp.where(sel, 0.0, NEG_MASK).astype(BF16))
        return carry

    lax.fori_loop(0, n_tiles, mask_body, 0)
    for p in range(PAD_UNITS):
        maskb_ref[p] = jnp.full((TQ, UNIT), NEG_MASK, BF16)

    m_ref[...] = jnp.full((n_rows, LANES), -jnp.inf, F32)
    acc_ref[...] = jnp.zeros((n_rows, 2 * D_CKV), F32)

    def att_step(j, with_bias):
        base = i + PAD_UNITS - j * ATT_UNITS - (ATT_UNITS - 1)
        rk = jnp.concatenate([ckvt_ref[base + u] for u in range(ATT_UNITS)], axis=1)
        rm = jnp.concatenate([maskb_ref[base + u] for u in range(ATT_UNITS)], axis=1)
        rhs = jnp.concatenate([rk, rm], axis=0)
        va = jnp.concatenate([vaug_ref[base + u] for u in range(ATT_UNITS)], axis=0)
        s = jnp.dot(a_ref[...], rhs, preferred_element_type=F32)
        ps = []
        for h in range(N_HEADS):
            sh = s[h * TQ:(h + 1) * TQ]
            if with_bias:
                sh = sh + bias_ref[h]
            m_old = m_ref[h * TQ:(h + 1) * TQ, :]
            m_new = jnp.maximum(m_old, jnp.max(sh, axis=1, keepdims=True))
            alpha = jnp.exp(m_old - m_new)
            m_ref[h * TQ:(h + 1) * TQ, :] = m_new
            p = jnp.exp(sh - jnp.concatenate([m_new] * ATT_UNITS, axis=1))
            ps.append(p.astype(BF16))
            acc_ref[h * TQ:(h + 1) * TQ, :] = (acc_ref[h * TQ:(h + 1) * TQ, :]
                                                * jnp.concatenate([alpha, alpha], axis=1))
        pall = jnp.concatenate(ps, axis=0)
        acc_ref[...] += jnp.dot(pall, va, preferred_element_type=F32)

    att_step(0, True)
    n_steps = (i + ATT_UNITS) // ATT_UNITS

    def att_body(j, carry):
        att_step(j, False)
        return carry

    lax.fori_loop(1, n_steps, att_body, 0)

    for h in range(N_HEADS):
        acc = acc_ref[h * TQ:(h + 1) * TQ, :]
        olat_ref[:, h * D_CKV:(h + 1) * D_CKV] = (acc[:, :D_CKV] / acc[:, D_CKV:]).astype(BF16)


def _t5_bucket(dist):
    n = jnp.maximum(dist, 0)
    nf = jnp.maximum(n, 1).astype(F32)
    large = MAX_EXACT + (jnp.log(nf / MAX_EXACT) / math.log(MAX_DISTANCE / MAX_EXACT)
                         * (NUM_BUCKETS - MAX_EXACT)).astype(I32)
    large = jnp.minimum(large, NUM_BUCKETS - 1)
    return jnp.where(n < MAX_EXACT, n, large)


def _attention(qidx, widx, qlat, kidx, ckv, rel_bias, batch, seq):
    assert seq % SCORE_TILE == 0
    k_sel = min(TOPK_MAX, seq // 4)
    n_units = seq // UNIT
    n_score_tiles = seq // SCORE_TILE
    r3 = lambda a: a.reshape(batch, seq, a.shape[-1])
    kidxt = jnp.transpose(r3(kidx).reshape(batch, n_score_tiles, SCORE_TILE, IDX_DIM), (0, 1, 3, 2))
    ckv3 = jnp.pad(r3(ckv), ((0, 0), (PAD_UNITS * UNIT, 0), (0, 0)))
    ckv4 = ckv3.reshape(batch, n_units + PAD_UNITS, UNIT, D_CKV)
    ckvt = jnp.transpose(ckv4, (0, 1, 3, 2))
    vaug = jnp.concatenate([ckv4, jnp.ones_like(ckv4)], axis=-1)
    assert ATT_UNITS == 2
    dist = (jnp.arange(TQ)[:, None] - jnp.arange(ATT_TK)[None, :] + (ATT_UNITS - 1) * UNIT)
    tab = rel_bias[_t5_bucket(dist)] - rel_bias[NUM_BUCKETS - 1]
    bias = jnp.transpose(tab, (2, 0, 1)).astype(F32)

    qblk = lambda c: pl.BlockSpec((None, TQ, c), lambda b, i: (b, i, 0))
    per_b = lambda a: pl.BlockSpec((None,) + a.shape[1:], lambda b, i: (b,) + (0,) * (a.ndim - 1))
    n_rows = N_HEADS * TQ
    out = pl.pallas_call(
        functools.partial(_attn_kernel, k_sel=k_sel, idx_bits=int(seq).bit_length()),
        grid=(batch, seq // TQ),
        in_specs=[qblk(IDX_HEADS * IDX_DIM), qblk(IDX_HEADS), qblk(N_HEADS * D_CKV),
                  per_b(kidxt), per_b(ckvt), per_b(vaug),
                  pl.BlockSpec(bias.shape, lambda b, i: (0, 0, 0))],
        out_specs=qblk(N_HEADS * D_CKV),
        out_shape=jax.ShapeDtypeStruct((batch, seq, N_HEADS * D_CKV), BF16),
        scratch_shapes=[
            pltpu.VMEM((IDX_HEADS * TQ, IDX_DIM), BF16),
            pltpu.VMEM((IDX_HEADS, TQ, LANES), F32),
            pltpu.VMEM((n_score_tiles, TQ, SCORE_TILE), I32),
            pltpu.VMEM((n_units + PAD_UNITS, TQ, UNIT), BF16),
            pltpu.VMEM((n_rows, 2 * D_CKV), BF16),
            pltpu.VMEM((n_rows, 2 * D_CKV), F32),
            pltpu.VMEM((n_rows, LANES), F32),
        ],
        compiler_params=_cparams(2),
        name="dsa_attn",
    )(r3(qidx), r3(widx), r3(qlat), kidxt, ckvt, vaug, bias)
    return out.reshape(batch * seq, N_HEADS * D_CKV)


def _attn_out_kernel(x_ref, olat_ref, wuv_ref, wo_ref, y_ref):
    o = jnp.dot(olat_ref[...], wuv_ref[...], preferred_element_type=F32)
    y_ref[...] = x_ref[...] + jnp.dot(o.astype(BF16), wo_ref[...], preferred_element_type=F32)


def _attn_out(x2, olat, w_uv, w_o, tm=512):
    n = x2.shape[0]
    wuv = jnp.transpose(w_uv, (1, 0, 2))
    eye = jnp.eye(N_HEADS, dtype=w_uv.dtype)
    wbd = (wuv[:, :, None, :] * eye[:, None, :, None]).reshape(N_HEADS * D_CKV, N_HEADS * D_V)
    full = lambda a: pl.BlockSpec(a.shape, lambda i: (0,) * a.ndim)
    args = (x2, olat, wbd.astype(BF16), w_o.astype(BF16))
    return pl.pallas_call(
        _attn_out_kernel,
        grid=(n // tm,),
        in_specs=[pl.BlockSpec((tm, D_MODEL), lambda i: (i, 0)),
                  pl.BlockSpec((tm, N_HEADS * D_CKV), lambda i: (i, 0)), full(args[2]), full(args[3])],
        out_specs=pl.BlockSpec((tm, D_MODEL), lambda i: (i, 0)),
        out_shape=jax.ShapeDtypeStruct((n, D_MODEL), F32),
        compiler_params=_cparams(1),
        name="attn_out",
    )(*args)


def _mlp_kernel(x_ref, g_ref, wup_ref, wdn_ref, gf_ref, y_ref, acc_ref, *, f_chunk, final_norm):
    x = x_ref[...]
    h = _rms(x, g_ref[...]).astype(BF16)
    d_ff = wup_ref.shape[1]
    for c in range(d_ff // f_chunk):
        u = jnp.dot(h, wup_ref[:, c * f_chunk:(c + 1) * f_chunk], preferred_element_type=F32)
        a = jnp.square(jnp.maximum(u, 0.0)).astype(BF16)
        d = jnp.dot(a, wdn_ref[c * f_chunk:(c + 1) * f_chunk, :], preferred_element_type=F32)
        if c == 0:
            acc_ref[...] = d
        else:
            acc_ref[...] += d
    y = x + acc_ref[...]
    if final_norm:
        y = _rms(y, gf_ref[...])
    y_ref[...] = y


def _mlp(x2, g, w_up, w_down, g_final, final_norm, tm=512, f_chunk=512):
    n = x2.shape[0]
    full = lambda a: pl.BlockSpec(a.shape, lambda i: (0,) * a.ndim)
    args = (x2, g.reshape(1, -1), w_up.astype(BF16), w_down.astype(BF16), g_final.reshape(1, -1))
    return pl.pallas_call(
        functools.partial(_mlp_kernel, f_chunk=f_chunk, final_norm=final_norm),
        grid=(n // tm,),
        in_specs=[pl.BlockSpec((tm, D_MODEL), lambda i: (i, 0))] + [full(a) for a in args[1:]],
        out_specs=pl.BlockSpec((tm, D_MODEL), lambda i: (i, 0)),
        out_shape=jax.ShapeDtypeStruct((n, D_MODEL), F32),
        scratch_shapes=[pltpu.VMEM((tm, D_MODEL), F32)],
        compiler_params=_cparams(1),
        name="mlp_final" if final_norm else "mlp",
    )(*args)


def _glu_kernel(x_ref, g_ref, w_ref, b_ref, u_ref):
    h = _rms(x_ref[...], g_ref[...]).astype(BF16)
    u = jnp.dot(h, w_ref[...], preferred_element_type=F32) + b_ref[...]
    d = u.shape[1] // 2
    u_ref[...] = u[:, :d] * jax.nn.sigmoid(u[:, d:])


def _glu(x2, g, w_pw1, b_pw1, tm=512):
    n = x2.shape[0]
    full = lambda a: pl.BlockSpec(a.shape, lambda i: (0,) * a.ndim)
    args = (x2, g.reshape(1, -1), w_pw1.astype(BF16), b_pw1.reshape(1, -1))
    return pl.pallas_call(
        _glu_kernel,
        grid=(n // tm,),
        in_specs=[pl.BlockSpec((tm, D_MODEL), lambda i: (i, 0))] + [full(a) for a in args[1:]],
        out_specs=pl.BlockSpec((tm, D_MODEL), lambda i: (i, 0)),
        out_shape=jax.ShapeDtypeStruct((n, D_MODEL), F32),
        compiler_params=_cparams(1),
        name="conv_glu",
    )(*args)


HALO = 32
CONV_ROWS = 32


def _dwconv_kernel(x_ref, u_ref, halo_ref, wdw_ref, bdw_ref, lng_ref, lnb_ref, w2_ref, b2_ref, y_ref,
                   ext_ref, cv_ref):
    i = pl.program_id(1)
    tm = u_ref.shape[0]
    ext_ref[:HALO, :] = jnp.where(i == 0, 0.0, halo_ref[...])
    ext_ref[HALO:, :] = u_ref[...]
    off = HALO - (CONV_WIDTH - 1)

    for r in range(tm // CONV_ROWS):
        r0 = r * CONV_ROWS
        acc = jnp.broadcast_to(bdw_ref[...], (CONV_ROWS, D_MODEL))
        for j in range(CONV_WIDTH):
            acc = acc + wdw_ref[j:j + 1, :] * ext_ref[r0 + off + j:r0 + off + j + CONV_ROWS, :]
        cv_ref[r0:r0 + CONV_ROWS, :] = acc
    v = cv_ref[...]
    mu = jnp.mean(v, axis=-1, keepdims=True)
    var = jnp.mean(jnp.square(v - mu), axis=-1, keepdims=True)
    v = (v - mu) * lax.rsqrt(var + EPS) * lng_ref[...] + lnb_ref[...]
    v = v * jax.nn.sigmoid(v)
    y_ref[...] = (x_ref[...] + jnp.dot(v.astype(BF16), w2_ref[...], preferred_element_type=F32)
                  + b2_ref[...])


def _dwconv(x2, u2, w_dw, b_dw, ln_g, ln_b, w_pw2, b_pw2, batch, seq, tm=512):
    x3 = x2.reshape(batch, seq, D_MODEL)
    u3 = u2.reshape(batch, seq, D_MODEL)
    hb = tm // HALO
    blk = pl.BlockSpec((None, tm, D_MODEL), lambda b, i: (b, i, 0))
    halo = pl.BlockSpec((None, HALO, D_MODEL), lambda b, i: (b, jnp.maximum(i * hb - 1, 0), 0))
    full = lambda a: pl.BlockSpec(a.shape, lambda b, i: (0,) * a.ndim)
    args = (x3, u3, u3, w_dw, b_dw.reshape(1, -1), ln_g.reshape(1, -1), ln_b.reshape(1, -1),
            w_pw2.astype(BF16), b_pw2.reshape(1, -1))
    out = pl.pallas_call(
        _dwconv_kernel,
        grid=(batch, seq // tm),
        in_specs=[blk, blk, halo] + [full(a) for a in args[3:]],
        out_specs=blk,
        out_shape=jax.ShapeDtypeStruct((batch, seq, D_MODEL), F32),
        scratch_shapes=[pltpu.VMEM((HALO + tm, D_MODEL), F32), pltpu.VMEM((tm, D_MODEL), F32)],
        compiler_params=_cparams(2),
        name="conv_dw",
    )(*args)
    return out.reshape(batch * seq, D_MODEL)


def kernel(x, norm_mix, norm_mlp, norm_final, rel_bias, attn_w_in, attn_q_norm, attn_kv_norm, attn_kidx_norm, attn_w_qidx, attn_w_uq, attn_w_uk, attn_w_uv, attn_w_o, conv_w_pw1, conv_b_pw1, conv_w_dw, conv_b_dw, conv_ln_g, conv_ln_b, conv_w_pw2, conv_b_pw2, mlp_w_up, mlp_w_down):
    batch, seq, d = x.shape
    depth = norm_mix.shape[0]
    x2 = x.reshape(batch * seq, d)
    for i in range(depth):
        j = i // 2
        if i % 2 == 0:
            qidx, widx, qlat, kidx, ckv = _proj(
                x2, norm_mix[i], attn_w_in[j], attn_q_norm[j], attn_kv_norm[j], attn_kidx_norm[j],
                attn_w_qidx[j], attn_w_uq[j], attn_w_uk[j])
            olat = _attention(qidx, widx, qlat, kidx, ckv, rel_bias, batch, seq)
            x2 = _attn_out(x2, olat, attn_w_uv[j], attn_w_o[j])
        else:
            u = _glu(x2, norm_mix[i], conv_w_pw1[j], conv_b_pw1[j])
            x2 = _dwconv(x2, u, conv_w_dw[j], conv_b_dw[j], conv_ln_g[j], conv_ln_b[j],
                         conv_w_pw2[j], conv_b_pw2[j], batch, seq)
        last = i == depth - 1
        x2 = _mlp(x2, norm_mlp[i], mlp_w_up[i], mlp_w_down[i], norm_final, final_norm=last)
    if depth == 0:
        raise ValueError("depth must be positive")
    return x2.reshape(batch, seq, d)
```

```python
import functools
import math

import jax
import jax.numpy as jnp
from jax import lax
from jax.experimental import pallas as pl
from jax.experimental.pallas import tpu as pltpu

F32 = jnp.float32
BF16 = jnp.bfloat16
I32 = jnp.int32

D_MODEL = 1024
N_HEADS = 16
D_NOPE = 64
D_V = 64
D_CQ = 256
D_CKV = 128
IDX_HEADS = 8
IDX_DIM = 64
TOPK_MAX = 256
CONV_WIDTH = 31
NUM_BUCKETS = 32
MAX_EXACT = 16
MAX_DISTANCE = 128
EPS = 1e-6

LANES = 128
SUBLANES = 8
TQ = 128
SCORE_TILE = 512
ATT_TK = 512
BIAS_TK = 2 * TQ
PAD_KEYS = ATT_TK - TQ
NEG_MASK = -1e30
LOG2E = math.log2(math.e)
FIXED_BITS = 12
VMEM_LIMIT = 56 * 1024 * 1024


def _cparams(n_axes):
    return pltpu.CompilerParams(dimension_semantics=("arbitrary",) * n_axes,
                                vmem_limit_bytes=VMEM_LIMIT)


def _rms(x, g):
    return x * lax.rsqrt(jnp.mean(x * x, axis=-1, keepdims=True) + EPS) * g


def _proj_kernel(x_ref, g_ref, win_ref, qn_ref, kvn_ref, kin_ref, wqidx_ref, wuq_ref, wukt_ref,
                 qidx_ref, widx_ref, qlat_ref, kidx_ref, ckv_ref):
    h = _rms(x_ref[...], g_ref[...])
    proj = jnp.dot(h.astype(BF16), win_ref[...], preferred_element_type=F32)
    o1, o2, o3 = D_CQ, D_CQ + D_CKV, D_CQ + D_CKV + IDX_DIM
    cq = _rms(proj[:, :o1], qn_ref[...])
    ckv = _rms(proj[:, o1:o2], kvn_ref[...])
    kid = _rms(proj[:, o2:o3], kin_ref[...])
    widx_ref[...] = proj[:, o3:o3 + IDX_HEADS] * (IDX_HEADS ** -0.5)
    ckv_ref[...] = ckv.astype(BF16)
    kidx_ref[...] = kid.astype(BF16)
    cqb = cq.astype(BF16)
    qidx = jnp.dot(cqb, wqidx_ref[...], preferred_element_type=F32) * (IDX_DIM ** -0.5)
    qidx_ref[...] = qidx.astype(BF16)
    qh = jnp.dot(cqb, wuq_ref[...], preferred_element_type=F32).astype(BF16)
    for hh in range(N_HEADS):
        ql = jnp.dot(qh[:, hh * D_NOPE:(hh + 1) * D_NOPE], wukt_ref[hh],
                     preferred_element_type=F32) * (D_NOPE ** -0.5 * LOG2E)
        qlat_ref[:, hh * D_CKV:(hh + 1) * D_CKV] = ql.astype(BF16)


def _proj(x2, g, w_in, qn, kvn, kin, w_qidx, w_uq, w_uk, tm=512):
    n = x2.shape[0]
    ncol = w_in.shape[1]
    npad = -ncol % LANES
    win = jnp.pad(w_in, ((0, 0), (0, npad))).astype(BF16)
    wukt = jnp.transpose(w_uk, (1, 2, 0)).astype(BF16)
    full = lambda a: pl.BlockSpec(a.shape, lambda i: (0,) * a.ndim)
    row = lambda c: pl.BlockSpec((tm, c), lambda i: (i, 0))
    args = (x2, g.reshape(1, -1), win, qn.reshape(1, -1), kvn.reshape(1, -1), kin.reshape(1, -1),
            w_qidx.astype(BF16), w_uq.astype(BF16), wukt)
    return pl.pallas_call(
        _proj_kernel,
        grid=(n // tm,),
        in_specs=[row(D_MODEL)] + [full(a) for a in args[1:]],
        out_specs=[row(IDX_HEADS * IDX_DIM), row(IDX_HEADS), row(N_HEADS * D_CKV), row(IDX_DIM),
                   row(D_CKV)],
        out_shape=[jax.ShapeDtypeStruct((n, IDX_HEADS * IDX_DIM), BF16),
                   jax.ShapeDtypeStruct((n, IDX_HEADS), F32),
                   jax.ShapeDtypeStruct((n, N_HEADS * D_CKV), BF16),
                   jax.ShapeDtypeStruct((n, IDX_DIM), BF16),
                   jax.ShapeDtypeStruct((n, D_CKV), BF16)],
        compiler_params=_cparams(1),
        name="dsa_proj",
    )(*args)


def _attn_kernel(qidxr_ref, widxt_ref, qlat_ref, kidx_ref, vaug_ref, bias_ref, olat_ref,
                 keys_ref, maskt_ref, a_ref, acc_ref, m_ref, *, k_sel, idx_bits):
    i = pl.program_id(1)
    q0 = i * TQ
    n_rows = N_HEADS * TQ
    qpos = q0 + lax.broadcasted_iota(I32, (1, TQ), 1)
    krow = lax.broadcasted_iota(I32, (SCORE_TILE, TQ), 0)
    n_tiles = i // (SCORE_TILE // TQ) + 1

    def tile_rows(tt):
        return pl.ds(pl.multiple_of(tt * SCORE_TILE, SCORE_TILE), SCORE_TILE)

    def score_body(tt, carry):
        qk = jnp.dot(kidx_ref[tile_rows(tt), :], qidxr_ref[...],
                     preferred_element_type=F32)
        sc = jnp.zeros((SCORE_TILE, TQ), F32)
        for h in range(IDX_HEADS):
            sc = sc + widxt_ref[h:h + 1, :] * jnp.maximum(qk[:, h * TQ:(h + 1) * TQ], 0.0)
        sc = jnp.where(tt * SCORE_TILE + krow <= qpos, sc, -jnp.inf)
        bits = pltpu.bitcast(sc, I32)
        keys_ref[tile_rows(tt), :] = bits ^ ((bits >> 31) & 0x7FFFFFFF)
        return carry

    lax.fori_loop(0, n_tiles, score_body, 0)

    def count(pred):
        def body(tt, acc):
            ind = jnp.where(pred(keys_ref[tile_rows(tt), :], tt * SCORE_TILE + krow), 1, 0)
            return acc + jnp.sum(ind.reshape(SCORE_TILE // SUBLANES, SUBLANES, TQ), axis=0)
        acc = lax.fori_loop(0, n_tiles, body, jnp.zeros((SUBLANES, TQ), I32))
        return jnp.sum(acc, axis=0, keepdims=True)

    trivial = qpos < k_sel

    def bit_step(step, cand, cntc):
        trial = cand ^ (jnp.int32(1) << (31 - step))
        cnt = count(lambda k, idx: k >= trial)
        ok = cnt >= k_sel
        return jnp.where(ok, trial, cand), jnp.where(ok, cnt, cntc)

    def n_open(cntc):
        return jnp.sum(jnp.where(jnp.logical_and(cntc != k_sel, jnp.logical_not(trivial)), 1, 0))

    cand = jnp.full((1, TQ), -2 ** 31, I32)
    cntc = jnp.full((1, TQ), 2 ** 30, I32)
    cand, cntc = lax.fori_loop(0, FIXED_BITS, lambda s, st: bit_step(s, *st), (cand, cntc))

    def bis_body(st):
        step, cand, cntc, _ = st
        cand, cntc = bit_step(step, cand, cntc)
        return step + 1, cand, cntc, n_open(cntc)

    _, cand, cntc, _ = lax.while_loop(
        lambda st: jnp.logical_and(st[0] < 32, st[3] > 0), bis_body,
        (jnp.int32(FIXED_BITS), cand, cntc, n_open(cntc)))

    excess = jnp.logical_and(cntc > k_sel, jnp.logical_not(trivial))
    big_j = jnp.int32(2 ** 30)

    def tie_fn():
        need = k_sel - count(lambda k, idx: k > cand)

        def jb(b, jv):
            trial = jv | (jnp.int32(1) << (idx_bits - 1 - b))
            c = count(lambda k, idx: jnp.logical_and(k == cand, idx < trial))
            return jnp.where(c < need, trial, jv)

        jv = lax.fori_loop(0, idx_bits, jb, jnp.zeros((1, TQ), I32))
        return jnp.where(excess, jv, big_j)

    jv = lax.cond(jnp.sum(jnp.where(excess, 1, 0)) > 0, tie_fn,
                  lambda: jnp.full((1, TQ), big_j, I32))

    def mask_body(tt, carry):
        k = keys_ref[tile_rows(tt), :]
        idx = tt * SCORE_TILE + krow
        sel = jnp.logical_or(k > cand, jnp.logical_and(k == cand, idx <= jv))
        sel = jnp.logical_and(sel, idx <= qpos)
        rows = pl.ds(pl.multiple_of(PAD_KEYS + tt * SCORE_TILE, TQ), SCORE_TILE)
        maskt_ref[rows, :] = jnp.where(sel, 0.0, NEG_MASK).astype(BF16)
        return carry

    lax.fori_loop(0, n_tiles, mask_body, 0)
    maskt_ref[:PAD_KEYS, :] = jnp.full((PAD_KEYS, TQ), NEG_MASK, BF16)

    eye = (lax.broadcasted_iota(I32, (TQ, TQ), 0) == lax.broadcasted_iota(I32, (TQ, TQ), 1))
    eye = jnp.where(eye, 1.0, 0.0).astype(BF16)
    for h in range(N_HEADS):
        a_ref[h * TQ:(h + 1) * TQ, :D_CKV] = qlat_ref[:, h * D_CKV:(h + 1) * D_CKV]
        a_ref[h * TQ:(h + 1) * TQ, D_CKV:] = eye
    m_ref[...] = jnp.full((n_rows, LANES), -jnp.inf, F32)
    acc_ref[...] = jnp.zeros((n_rows, 2 * D_CKV), F32)
    n_chunks = ATT_TK // LANES

    def att_step(j, with_bias):
        r0 = pl.multiple_of(q0 + TQ + PAD_KEYS - (j + 1) * ATT_TK, TQ)
        vk = vaug_ref[pl.ds(r0, ATT_TK), :]
        rhs = jnp.concatenate([vk[:, :D_CKV], maskt_ref[pl.ds(r0, ATT_TK), :]], axis=1)
        s = lax.dot_general(a_ref[...], rhs, (((1,), (1,)), ((), ())),
                            preferred_element_type=F32)
        ps = []
        for h in range(N_HEADS):
            rows = slice(h * TQ, (h + 1) * TQ)
            sh = s[rows]
            if with_bias:
                sh = jnp.concatenate([sh[:, :ATT_TK - BIAS_TK], sh[:, ATT_TK - BIAS_TK:] + bias_ref[h]],
                                     axis=1)
            m_old = m_ref[rows, :]
            m_new = jnp.maximum(m_old, jnp.max(sh, axis=1, keepdims=True))
            alpha = jnp.exp2(m_old - m_new)
            m_ref[rows, :] = m_new
            p = jnp.exp2(sh - jnp.concatenate([m_new] * n_chunks, axis=1))
            ps.append(p.astype(BF16))
            acc_ref[rows, :] = acc_ref[rows, :] * jnp.concatenate([alpha, alpha], axis=1)
        acc_ref[...] += jnp.dot(jnp.concatenate(ps, axis=0), vk, preferred_element_type=F32)

    att_step(0, True)
    n_steps = (q0 + TQ + ATT_TK - 1) // ATT_TK

    def att_body(j, carry):
        att_step(j, False)
        return carry

    lax.fori_loop(1, n_steps, att_body, 0)

    for h in range(N_HEADS):
        acc = acc_ref[h * TQ:(h + 1) * TQ, :]
        olat_ref[:, h * D_CKV:(h + 1) * D_CKV] = (acc[:, :D_CKV] / acc[:, D_CKV:]).astype(BF16)


def _t5_bucket(dist):
    n = jnp.maximum(dist, 0)
    nf = jnp.maximum(n, 1).astype(F32)
    large = MAX_EXACT + (jnp.log(nf / MAX_EXACT) / math.log(MAX_DISTANCE / MAX_EXACT)
                         * (NUM_BUCKETS - MAX_EXACT)).astype(I32)
    large = jnp.minimum(large, NUM_BUCKETS - 1)
    return jnp.where(n < MAX_EXACT, n, large)


def _near_bias(rel_bias):
    assert BIAS_TK >= TQ + MAX_DISTANCE - 1
    period = TQ + BIAS_TK
    e = jnp.arange(period)
    e = jnp.where(e < BIAS_TK, e, e - period)
    tab = rel_bias[_t5_bucket(BIAS_TK - TQ - e)] - rel_bias[NUM_BUCKETS - 1]
    flat = jnp.tile(tab, (TQ, 1))[:TQ * (period - 1)]
    near = flat.reshape(TQ, period - 1, N_HEADS)[:, :BIAS_TK]
    return (jnp.transpose(near, (2, 0, 1)) * LOG2E).astype(F32)


def _attention(qidx, widx, qlat, kidx, ckv, rel_bias, batch, seq):
    assert seq % SCORE_TILE == 0 and SCORE_TILE % TQ == 0 and ATT_TK % TQ == 0
    k_sel = min(TOPK_MAX, seq // 4)
    nq = seq // TQ
    r3 = lambda a: a.reshape(batch, seq, a.shape[-1])
    qidxr = jnp.transpose(qidx.reshape(batch, nq, TQ, IDX_HEADS, IDX_DIM), (0, 1, 4, 3, 2))
    qidxr = qidxr.reshape(batch, nq, IDX_DIM, IDX_HEADS * TQ)
    widxt = jnp.transpose(widx.reshape(batch, nq, TQ, IDX_HEADS), (0, 1, 3, 2))
    ckv3 = jnp.pad(r3(ckv), ((0, 0), (PAD_KEYS, 0), (0, 0)))
    vaug = jnp.concatenate([ckv3, jnp.ones_like(ckv3)], axis=-1)
    bias = _near_bias(rel_bias)

    qblk = lambda c: pl.BlockSpec((None, TQ, c), lambda b, i: (b, i, 0))
    qtile = lambda a: pl.BlockSpec((None, None) + a.shape[2:], lambda b, i: (b, i, 0, 0))
    per_b = lambda a: pl.BlockSpec((None,) + a.shape[1:], lambda b, i: (b,) + (0,) * (a.ndim - 1))
    n_rows = N_HEADS * TQ
    kidx3 = r3(kidx)
    out = pl.pallas_call(
        functools.partial(_attn_kernel, k_sel=k_sel, idx_bits=int(seq).bit_length()),
        grid=(batch, nq),
        in_specs=[qtile(qidxr), qtile(widxt), qblk(N_HEADS * D_CKV), per_b(kidx3), per_b(vaug),
                  pl.BlockSpec(bias.shape, lambda b, i: (0, 0, 0))],
        out_specs=qblk(N_HEADS * D_CKV),
        out_shape=jax.ShapeDtypeStruct((batch, seq, N_HEADS * D_CKV), BF16),
        scratch_shapes=[
            pltpu.VMEM((seq, TQ), I32),
            pltpu.VMEM((seq + PAD_KEYS, TQ), BF16),
            pltpu.VMEM((n_rows, 2 * D_CKV), BF16),
            pltpu.VMEM((n_rows, 2 * D_CKV), F32),
            pltpu.VMEM((n_rows, LANES), F32),
        ],
        compiler_params=_cparams(2),
        name="dsa_attn",
    )(qidxr, widxt, r3(qlat), kidx3, vaug, bias)
    return out.reshape(batch * seq, N_HEADS * D_CKV)


def _attn_out_kernel(x_ref, olat_ref, wuv_ref, wo_ref, y_ref):
    o = jnp.dot(olat_ref[...], wuv_ref[...], preferred_element_type=F32)
    y_ref[...] = x_ref[...] + jnp.dot(o.astype(BF16), wo_ref[...], preferred_element_type=F32)


def _attn_out(x2, olat, w_uv, w_o, tm=512):
    n = x2.shape[0]
    wuv = jnp.transpose(w_uv, (1, 0, 2))
    eye = jnp.eye(N_HEADS, dtype=w_uv.dtype)
    wbd = (wuv[:, :, None, :] * eye[:, None, :, None]).reshape(N_HEADS * D_CKV, N_HEADS * D_V)
    full = lambda a: pl.BlockSpec(a.shape, lambda i: (0,) * a.ndim)
    args = (x2, olat, wbd.astype(BF16), w_o.astype(BF16))
    return pl.pallas_call(
        _attn_out_kernel,
        grid=(n // tm,),
        in_specs=[pl.BlockSpec((tm, D_MODEL), lambda i: (i, 0)),
                  pl.BlockSpec((tm, N_HEADS * D_CKV), lambda i: (i, 0)), full(args[2]), full(args[3])],
        out_specs=pl.BlockSpec((tm, D_MODEL), lambda i: (i, 0)),
        out_shape=jax.ShapeDtypeStruct((n, D_MODEL), F32),
        compiler_params=_cparams(1),
        name="attn_out",
    )(*args)


def _mlp_kernel(x_ref, g_ref, wup_ref, wdn_ref, gf_ref, y_ref, acc_ref, *, f_chunk, final_norm):
    x = x_ref[...]
    h = _rms(x, g_ref[...]).astype(BF16)
    d_ff = wup_ref.shape[1]
    for c in range(d_ff // f_chunk):
        u = jnp.dot(h, wup_ref[:, c * f_chunk:(c + 1) * f_chunk], preferred_element_type=F32)
        a = jnp.square(jnp.maximum(u, 0.0)).astype(BF16)
        d = jnp.dot(a, wdn_ref[c * f_chunk:(c + 1) * f_chunk, :], preferred_element_type=F32)
        if c == 0:
            acc_ref[...] = d
        else:
            acc_ref[...] += d
    y = x + acc_ref[...]
    if final_norm:
        y = _rms(y, gf_ref[...])
    y_ref[...] = y


def _mlp(x2, g, w_up, w_down, g_final, final_norm, tm=512, f_chunk=512):
    n = x2.shape[0]
    full = lambda a: pl.BlockSpec(a.shape, lambda i: (0,) * a.ndim)
    args = (x2, g.reshape(1, -1), w_up.astype(BF16), w_down.astype(BF16), g_final.reshape(1, -1))
    return pl.pallas_call(
        functools.partial(_mlp_kernel, f_chunk=f_chunk, final_norm=final_norm),
        grid=(n // tm,),
        in_specs=[pl.BlockSpec((tm, D_MODEL), lambda i: (i, 0))] + [full(a) for a in args[1:]],
        out_specs=pl.BlockSpec((tm, D_MODEL), lambda i: (i, 0)),
        out_shape=jax.ShapeDtypeStruct((n, D_MODEL), F32),
        scratch_shapes=[pltpu.VMEM((tm, D_MODEL), F32)],
        compiler_params=_cparams(1),
        name="mlp_final" if final_norm else "mlp",
    )(*args)


def _glu_kernel(x_ref, g_ref, w_ref, b_ref, u_ref):
    h = _rms(x_ref[...], g_ref[...]).astype(BF16)
    u = jnp.dot(h, w_ref[...], preferred_element_type=F32) + b_ref[...]
    d = u.shape[1] // 2
    u_ref[...] = u[:, :d] * jax.nn.sigmoid(u[:, d:])


def _glu(x2, g, w_pw1, b_pw1, tm=512):
    n = x2.shape[0]
    full = lambda a: pl.BlockSpec(a.shape, lambda i: (0,) * a.ndim)
    args = (x2, g.reshape(1, -1), w_pw1.astype(BF16), b_pw1.reshape(1, -1))
    return pl.pallas_call(
        _glu_kernel,
        grid=(n // tm,),
        in_specs=[pl.BlockSpec((tm, D_MODEL), lambda i: (i, 0))] + [full(a) for a in args[1:]],
        out_specs=pl.BlockSpec((tm, D_MODEL), lambda i: (i, 0)),
        out_shape=jax.ShapeDtypeStruct((n, D_MODEL), F32),
        compiler_params=_cparams(1),
        name="conv_glu",
    )(*args)


HALO = 32
CONV_ROWS = 32


def _dwconv_kernel(x_ref, u_ref, halo_ref, wdw_ref, bdw_ref, lng_ref, lnb_ref, w2_ref, b2_ref, y_ref,
                   ext_ref, cv_ref):
    i = pl.program_id(1)
    tm = u_ref.shape[0]
    ext_ref[:HALO, :] = jnp.where(i == 0, 0.0, halo_ref[...])
    ext_ref[HALO:, :] = u_ref[...]
    off = HALO - (CONV_WIDTH - 1)
    for r in range(tm // CONV_ROWS):
        r0 = r * CONV_ROWS
        acc = jnp.broadcast_to(bdw_ref[...], (CONV_ROWS, D_MODEL))
        for j in range(CONV_WIDTH):
            acc = acc + wdw_ref[j:j + 1, :] * ext_ref[r0 + off + j:r0 + off + j + CONV_ROWS, :]
        cv_ref[r0:r0 + CONV_ROWS, :] = acc
    v = cv_ref[...]
    mu = jnp.mean(v, axis=-1, keepdims=True)
    var = jnp.mean(jnp.square(v - mu), axis=-1, keepdims=True)
    v = (v - mu) * lax.rsqrt(var + EPS) * lng_ref[...] + lnb_ref[...]
    v = v * jax.nn.sigmoid(v)
    y_ref[...] = (x_ref[...] + jnp.dot(v.astype(BF16), w2_ref[...], preferred_element_type=F32)
                  + b2_ref[...])


def _dwconv(x2, u2, w_dw, b_dw, ln_g, ln_b, w_pw2, b_pw2, batch, seq, tm=512):
    x3 = x2.reshape(batch, seq, D_MODEL)
    u3 = u2.reshape(batch, seq, D_MODEL)
    hb = tm // HALO
    blk = pl.BlockSpec((None, tm, D_MODEL), lambda b, i: (b, i, 0))
    halo = pl.BlockSpec((None, HALO, D_MODEL), lambda b, i: (b, jnp.maximum(i * hb - 1, 0), 0))
    full = lambda a: pl.BlockSpec(a.shape, lambda b, i: (0,) * a.ndim)
    args = (x3, u3, u3, w_dw, b_dw.reshape(1, -1), ln_g.reshape(1, -1), ln_b.reshape(1, -1),
            w_pw2.astype(BF16), b_pw2.reshape(1, -1))
    out = pl.pallas_call(
        _dwconv_kernel,
        grid=(batch, seq // tm),
        in_specs=[blk, blk, halo] + [full(a) for a in args[3:]],
        out_specs=blk,
        out_shape=jax.ShapeDtypeStruct((batch, seq, D_MODEL), F32),
        scratch_shapes=[pltpu.VMEM((HALO + tm, D_MODEL), F32), pltpu.VMEM((tm, D_MODEL), F32)],
        compiler_params=_cparams(2),
        name="conv_dw",
    )(*args)
    return out.reshape(batch * seq, D_MODEL)


def kernel(x, norm_mix, norm_mlp, norm_final, rel_bias, attn_w_in, attn_q_norm, attn_kv_norm, attn_kidx_norm, attn_w_qidx, attn_w_uq, attn_w_uk, attn_w_uv, attn_w_o, conv_w_pw1, conv_b_pw1, conv_w_dw, conv_b_dw, conv_ln_g, conv_ln_b, conv_w_pw2, conv_b_pw2, mlp_w_up, mlp_w_down):
    batch, seq, d = x.shape
    depth = norm_mix.shape[0]
    x2 = x.reshape(batch * seq, d)
    for i in range(depth):
        j = i // 2
        if i % 2 == 0:
            qidx, widx, qlat, kidx, ckv = _proj(
                x2, norm_mix[i], attn_w_in[j], attn_q_norm[j], attn_kv_norm[j], attn_kidx_norm[j],
                attn_w_qidx[j], attn_w_uq[j], attn_w_uk[j])
            olat = _attention(qidx, widx, qlat, kidx, ckv, rel_bias, batch, seq)
            x2 = _attn_out(x2, olat, attn_w_uv[j], attn_w_o[j])
        else:
            u = _glu(x2, norm_mix[i], conv_w_pw1[j], conv_b_pw1[j])
            x2 = _dwconv(x2, u, conv_w_dw[j], conv_b_dw[j], conv_ln_g[j], conv_ln_b[j],
                         conv_w_pw2[j], conv_b_pw2[j], batch, seq)
        last = i == depth - 1
        x2 = _mlp(x2, norm_mlp[i], mlp_w_up[i], mlp_w_down[i], norm_final, final_norm=last)
    if depth == 0:
        raise ValueError("depth must be positive")
    return x2.reshape(batch, seq, d)
```

```python
import functools
import math

import jax
import jax.numpy as jnp
from jax import lax
from jax.experimental import pallas as pl
from jax.experimental.pallas import tpu as pltpu

F32 = jnp.float32
BF16 = jnp.bfloat16
I32 = jnp.int32

D_MODEL = 1024
N_HEADS = 16
D_NOPE = 64
D_V = 64
D_CQ = 256
D_CKV = 128
IDX_HEADS = 8
IDX_DIM = 64
TOPK_MAX = 256
CONV_WIDTH = 31
NUM_BUCKETS = 32
MAX_EXACT = 16
MAX_DISTANCE = 128
EPS = 1e-6

LANES = 128
SUBLANES = 8
TQ = 128
SCORE_TILE = 512
ATT_TK = 512
BIAS_TK = 2 * TQ
PAD_KEYS = ATT_TK - TQ
NEG_MASK = -1e30
LOG2E = math.log2(math.e)
FIXED_BITS = 12
VMEM_LIMIT = 56 * 1024 * 1024


def _cparams(n_axes):
    return pltpu.CompilerParams(dimension_semantics=("arbitrary",) * n_axes,
                                vmem_limit_bytes=VMEM_LIMIT)


def _rms(x, g):
    return x * lax.rsqrt(jnp.mean(x * x, axis=-1, keepdims=True) + EPS) * g


def _proj_kernel(x_ref, g_ref, win_ref, qn_ref, kvn_ref, kin_ref, wqidx_ref, wuq_ref, wukt_ref,
                 qidx_ref, widx_ref, qlat_ref, kidx_ref, ckv_ref):
    h = _rms(x_ref[...], g_ref[...])
    proj = jnp.dot(h.astype(BF16), win_ref[...], preferred_element_type=F32)
    o1, o2, o3 = D_CQ, D_CQ + D_CKV, D_CQ + D_CKV + IDX_DIM
    cq = _rms(proj[:, :o1], qn_ref[...])
    ckv = _rms(proj[:, o1:o2], kvn_ref[...])
    kid = _rms(proj[:, o2:o3], kin_ref[...])
    widx_ref[...] = proj[:, o3:o3 + IDX_HEADS] * (IDX_HEADS ** -0.5)
    ckv_ref[...] = ckv.astype(BF16)
    kidx_ref[...] = kid.astype(BF16)
    cqb = cq.astype(BF16)
    qidx = jnp.dot(cqb, wqidx_ref[...], preferred_element_type=F32) * (IDX_DIM ** -0.5)
    qidx_ref[...] = qidx.astype(BF16)
    qh = jnp.dot(cqb, wuq_ref[...], preferred_element_type=F32).astype(BF16)
    for hh in range(N_HEADS):
        ql = jnp.dot(qh[:, hh * D_NOPE:(hh + 1) * D_NOPE], wukt_ref[hh],
                     preferred_element_type=F32) * (D_NOPE ** -0.5 * LOG2E)
        qlat_ref[:, hh * D_CKV:(hh + 1) * D_CKV] = ql.astype(BF16)


def _proj(x2, g, w_in, qn, kvn, kin, w_qidx, w_uq, w_uk, tm=512):
    n = x2.shape[0]
    ncol = w_in.shape[1]
    npad = -ncol % LANES
    win = jnp.pad(w_in, ((0, 0), (0, npad))).astype(BF16)
    wukt = jnp.transpose(w_uk, (1, 2, 0)).astype(BF16)
    full = lambda a: pl.BlockSpec(a.shape, lambda i: (0,) * a.ndim)
    row = lambda c: pl.BlockSpec((tm, c), lambda i: (i, 0))
    args = (x2, g.reshape(1, -1), win, qn.reshape(1, -1), kvn.reshape(1, -1), kin.reshape(1, -1),
            w_qidx.astype(BF16), w_uq.astype(BF16), wukt)
    return pl.pallas_call(
        _proj_kernel,
        grid=(n // tm,),
        in_specs=[row(D_MODEL)] + [full(a) for a in args[1:]],
        out_specs=[row(IDX_HEADS * IDX_DIM), row(IDX_HEADS), row(N_HEADS * D_CKV), row(IDX_DIM),
                   row(D_CKV)],
        out_shape=[jax.ShapeDtypeStruct((n, IDX_HEADS * IDX_DIM), BF16),
                   jax.ShapeDtypeStruct((n, IDX_HEADS), F32),
                   jax.ShapeDtypeStruct((n, N_HEADS * D_CKV), BF16),
                   jax.ShapeDtypeStruct((n, IDX_DIM), BF16),
                   jax.ShapeDtypeStruct((n, D_CKV), BF16)],
        compiler_params=_cparams(1),
        name="dsa_proj",
    )(*args)


def _attn_kernel(qidxr_ref, widxt_ref, qlat_ref, kidx_ref, vaug_ref, bias_ref, olat_ref,
                 keys_ref, maskt_ref, a_ref, acc_ref, m_ref, s_ref, *, k_sel, idx_bits):
    i = pl.program_id(1)
    q0 = i * TQ
    n_rows = N_HEADS * TQ
    qpos = q0 + lax.broadcasted_iota(I32, (1, TQ), 1)
    krow = lax.broadcasted_iota(I32, (SCORE_TILE, TQ), 0)
    n_tiles = i // (SCORE_TILE // TQ) + 1

    def tile_rows(tt):
        return pl.ds(pl.multiple_of(tt * SCORE_TILE, SCORE_TILE), SCORE_TILE)

    def score_body(tt, carry):
        qk = jnp.dot(kidx_ref[tile_rows(tt), :], qidxr_ref[...],
                     preferred_element_type=F32)
        sc = jnp.zeros((SCORE_TILE, TQ), F32)
        for h in range(IDX_HEADS):
            sc = sc + widxt_ref[h:h + 1, :] * jnp.maximum(qk[:, h * TQ:(h + 1) * TQ], 0.0)
        sc = jnp.where(tt * SCORE_TILE + krow <= qpos, sc, -jnp.inf)
        bits = pltpu.bitcast(sc, I32)
        keys_ref[tile_rows(tt), :] = bits ^ ((bits >> 31) & 0x7FFFFFFF)
        return carry

    lax.fori_loop(0, n_tiles, score_body, 0)

    def count(pred):
        def body(tt, acc):
            ind = jnp.where(pred(keys_ref[tile_rows(tt), :], tt * SCORE_TILE + krow), 1, 0)
            return acc + jnp.sum(ind.reshape(SCORE_TILE // SUBLANES, SUBLANES, TQ), axis=0)
        acc = lax.fori_loop(0, n_tiles, body, jnp.zeros((SUBLANES, TQ), I32))
        return jnp.sum(acc, axis=0, keepdims=True)

    trivial = qpos < k_sel

    def bit_step(step, cand, cntc):
        trial = cand ^ (jnp.int32(1) << (31 - step))
        cnt = count(lambda k, idx: k >= trial)
        ok = cnt >= k_sel
        return jnp.where(ok, trial, cand), jnp.where(ok, cnt, cntc)

    def n_open(cntc):
        return jnp.sum(jnp.where(jnp.logical_and(cntc != k_sel, jnp.logical_not(trivial)), 1, 0))

    cand = jnp.full((1, TQ), -2 ** 31, I32)
    cntc = jnp.full((1, TQ), 2 ** 30, I32)
    cand, cntc = lax.fori_loop(0, FIXED_BITS, lambda s, st: bit_step(s, *st), (cand, cntc))

    def bis_body(st):
        step, cand, cntc, _ = st
        cand, cntc = bit_step(step, cand, cntc)
        return step + 1, cand, cntc, n_open(cntc)

    _, cand, cntc, _ = lax.while_loop(
        lambda st: jnp.logical_and(st[0] < 32, st[3] > 0), bis_body,
        (jnp.int32(FIXED_BITS), cand, cntc, n_open(cntc)))

    excess = jnp.logical_and(cntc > k_sel, jnp.logical_not(trivial))
    big_j = jnp.int32(2 ** 30)

    def tie_fn():
        need = k_sel - count(lambda k, idx: k > cand)

        def jb(b, jv):
            trial = jv | (jnp.int32(1) << (idx_bits - 1 - b))
            c = count(lambda k, idx: jnp.logical_and(k == cand, idx < trial))
            return jnp.where(c < need, trial, jv)

        jv = lax.fori_loop(0, idx_bits, jb, jnp.zeros((1, TQ), I32))
        return jnp.where(excess, jv, big_j)

    jv = lax.cond(jnp.sum(jnp.where(excess, 1, 0)) > 0, tie_fn,
                  lambda: jnp.full((1, TQ), big_j, I32))

    def mask_body(tt, carry):
        k = keys_ref[tile_rows(tt), :]
        idx = tt * SCORE_TILE + krow
        sel = jnp.logical_or(k > cand, jnp.logical_and(k == cand, idx <= jv))
        sel = jnp.logical_and(sel, idx <= qpos)
        rows = pl.ds(pl.multiple_of(PAD_KEYS + tt * SCORE_TILE, TQ), SCORE_TILE)
        maskt_ref[rows, :] = jnp.where(sel, 0.0, NEG_MASK).astype(BF16)
        return carry

    lax.fori_loop(0, n_tiles, mask_body, 0)
    maskt_ref[:PAD_KEYS, :] = jnp.full((PAD_KEYS, TQ), NEG_MASK, BF16)

    eye = (lax.broadcasted_iota(I32, (TQ, TQ), 0) == lax.broadcasted_iota(I32, (TQ, TQ), 1))
    eye = jnp.where(eye, 1.0, 0.0).astype(BF16)
    for h in range(N_HEADS):
        a_ref[h * TQ:(h + 1) * TQ, :D_CKV] = qlat_ref[:, h * D_CKV:(h + 1) * D_CKV]
        a_ref[h * TQ:(h + 1) * TQ, D_CKV:] = eye
    m_ref[...] = jnp.full((n_rows, LANES), -jnp.inf, F32)
    acc_ref[...] = jnp.zeros((n_rows, 2 * D_CKV), F32)
    n_chunks = ATT_TK // LANES

    def key_rows(j):
        r0 = jnp.maximum(q0 + TQ + PAD_KEYS - (j + 1) * ATT_TK, 0)
        return pl.ds(pl.multiple_of(r0, TQ), ATT_TK)

    def logits(j, slot):
        rows = key_rows(j)
        rhs = jnp.concatenate([vaug_ref[rows, :D_CKV], maskt_ref[rows, :]], axis=1)
        s_ref[slot] = lax.dot_general(a_ref[...], rhs, (((1,), (1,)), ((), ())),
                                      preferred_element_type=F32)

    def consume(j, slot):
        vk = vaug_ref[key_rows(j), :]
        near = jnp.minimum(j, 1)
        ps = []
        for h in range(N_HEADS):
            rows = slice(h * TQ, (h + 1) * TQ)
            sh = s_ref[slot, rows, :]
            sh = jnp.concatenate([sh[:, :ATT_TK - BIAS_TK],
                                  sh[:, ATT_TK - BIAS_TK:] + bias_ref[near, h]], axis=1)
            m_old = m_ref[rows, :]
            m_new = jnp.maximum(m_old, jnp.max(sh, axis=1, keepdims=True))
            alpha = jnp.exp2(m_old - m_new)
            m_ref[rows, :] = m_new
            p = jnp.exp2(sh - jnp.concatenate([m_new] * n_chunks, axis=1))
            ps.append(p.astype(BF16))
            acc_ref[rows, :] = acc_ref[rows, :] * jnp.concatenate([alpha, alpha], axis=1)
        acc_ref[...] += jnp.dot(jnp.concatenate(ps, axis=0), vk, preferred_element_type=F32)

    n_steps = (q0 + TQ + ATT_TK - 1) // ATT_TK
    logits(0, 0)

    def att_pair(jj, carry):
        j = 2 * jj
        logits(j + 1, 1)
        consume(j, 0)
        logits(j + 2, 0)
        consume(j + 1, 1)
        return carry

    lax.fori_loop(0, n_steps // 2, att_pair, 0)

    @pl.when(n_steps % 2 == 1)
    def _():
        consume(n_steps - 1, 0)

    for h in range(N_HEADS):
        acc = acc_ref[h * TQ:(h + 1) * TQ, :]
        olat_ref[:, h * D_CKV:(h + 1) * D_CKV] = (acc[:, :D_CKV] / acc[:, D_CKV:]).astype(BF16)


def _t5_bucket(dist):
    n = jnp.maximum(dist, 0)
    nf = jnp.maximum(n, 1).astype(F32)
    large = MAX_EXACT + (jnp.log(nf / MAX_EXACT) / math.log(MAX_DISTANCE / MAX_EXACT)
                         * (NUM_BUCKETS - MAX_EXACT)).astype(I32)
    large = jnp.minimum(large, NUM_BUCKETS - 1)
    return jnp.where(n < MAX_EXACT, n, large)


def _near_bias(rel_bias):
    assert BIAS_TK >= TQ + MAX_DISTANCE - 1
    period = TQ + BIAS_TK
    e = jnp.arange(period)
    e = jnp.where(e < BIAS_TK, e, e - period)
    tab = rel_bias[_t5_bucket(BIAS_TK - TQ - e)] - rel_bias[NUM_BUCKETS - 1]
    flat = jnp.tile(tab, (TQ, 1))[:TQ * (period - 1)]
    near = flat.reshape(TQ, period - 1, N_HEADS)[:, :BIAS_TK]
    return (jnp.transpose(near, (2, 0, 1)) * LOG2E).astype(F32)


def _attention(qidx, widx, qlat, kidx, ckv, rel_bias, batch, seq):
    assert seq % SCORE_TILE == 0 and SCORE_TILE % TQ == 0 and ATT_TK % TQ == 0
    k_sel = min(TOPK_MAX, seq // 4)
    nq = seq // TQ
    r3 = lambda a: a.reshape(batch, seq, a.shape[-1])
    qidxr = jnp.transpose(qidx.reshape(batch, nq, TQ, IDX_HEADS, IDX_DIM), (0, 1, 4, 3, 2))
    qidxr = qidxr.reshape(batch, nq, IDX_DIM, IDX_HEADS * TQ)
    widxt = jnp.transpose(widx.reshape(batch, nq, TQ, IDX_HEADS), (0, 1, 3, 2))
    ckv3 = jnp.pad(r3(ckv), ((0, 0), (PAD_KEYS, 0), (0, 0)))
    vaug = jnp.concatenate([ckv3, jnp.ones_like(ckv3)], axis=-1)
    bias = _near_bias(rel_bias)
    bias = jnp.stack([bias, jnp.zeros_like(bias)])

    qblk = lambda c: pl.BlockSpec((None, TQ, c), lambda b, i: (b, i, 0))
    qtile = lambda a: pl.BlockSpec((None, None) + a.shape[2:], lambda b, i: (b, i, 0, 0))
    per_b = lambda a: pl.BlockSpec((None,) + a.shape[1:], lambda b, i: (b,) + (0,) * (a.ndim - 1))
    n_rows = N_HEADS * TQ
    kidx3 = r3(kidx)
    out = pl.pallas_call(
        functools.partial(_attn_kernel, k_sel=k_sel, idx_bits=int(seq).bit_length()),
        grid=(batch, nq),
        in_specs=[qtile(qidxr), qtile(widxt), qblk(N_HEADS * D_CKV), per_b(kidx3), per_b(vaug),
                  pl.BlockSpec(bias.shape, lambda b, i: (0, 0, 0, 0),
                               pipeline_mode=pl.Buffered(1))],
        out_specs=qblk(N_HEADS * D_CKV),
        out_shape=jax.ShapeDtypeStruct((batch, seq, N_HEADS * D_CKV), BF16),
        scratch_shapes=[
            pltpu.VMEM((seq, TQ), I32),
            pltpu.VMEM((seq + PAD_KEYS, TQ), BF16),
            pltpu.VMEM((n_rows, 2 * D_CKV), BF16),
            pltpu.VMEM((n_rows, 2 * D_CKV), F32),
            pltpu.VMEM((n_rows, LANES), F32),
            pltpu.VMEM((2, n_rows, ATT_TK), F32),
        ],
        compiler_params=_cparams(2),
        name="dsa_attn",
    )(qidxr, widxt, r3(qlat), kidx3, vaug, bias)
    return out.reshape(batch * seq, N_HEADS * D_CKV)


def _attn_out_kernel(x_ref, olat_ref, wuv_ref, wo_ref, y_ref):
    o = jnp.dot(olat_ref[...], wuv_ref[...], preferred_element_type=F32)
    y_ref[...] = x_ref[...] + jnp.dot(o.astype(BF16), wo_ref[...], preferred_element_type=F32)


def _attn_out(x2, olat, w_uv, w_o, tm=512):
    n = x2.shape[0]
    wuv = jnp.transpose(w_uv, (1, 0, 2))
    eye = jnp.eye(N_HEADS, dtype=w_uv.dtype)
    wbd = (wuv[:, :, None, :] * eye[:, None, :, None]).reshape(N_HEADS * D_CKV, N_HEADS * D_V)
    full = lambda a: pl.BlockSpec(a.shape, lambda i: (0,) * a.ndim)
    args = (x2, olat, wbd.astype(BF16), w_o.astype(BF16))
    return pl.pallas_call(
        _attn_out_kernel,
        grid=(n // tm,),
        in_specs=[pl.BlockSpec((tm, D_MODEL), lambda i: (i, 0)),
                  pl.BlockSpec((tm, N_HEADS * D_CKV), lambda i: (i, 0)), full(args[2]), full(args[3])],
        out_specs=pl.BlockSpec((tm, D_MODEL), lambda i: (i, 0)),
        out_shape=jax.ShapeDtypeStruct((n, D_MODEL), F32),
        compiler_params=_cparams(1),
        name="attn_out",
    )(*args)


def _mlp_kernel(x_ref, g_ref, wup_ref, wdn_ref, gf_ref, y_ref, acc_ref, *, f_chunk, final_norm):
    x = x_ref[...]
    h = _rms(x, g_ref[...]).astype(BF16)
    d_ff = wup_ref.shape[1]
    for c in range(d_ff // f_chunk):
        u = jnp.dot(h, wup_ref[:, c * f_chunk:(c + 1) * f_chunk], preferred_element_type=F32)
        a = jnp.square(jnp.maximum(u, 0.0)).astype(BF16)
        d = jnp.dot(a, wdn_ref[c * f_chunk:(c + 1) * f_chunk, :], preferred_element_type=F32)
        if c == 0:
            acc_ref[...] = d
        else:
            acc_ref[...] += d
    y = x + acc_ref[...]
    if final_norm:
        y = _rms(y, gf_ref[...])
    y_ref[...] = y


def _mlp(x2, g, w_up, w_down, g_final, final_norm, tm=512, f_chunk=512):
    n = x2.shape[0]
    full = lambda a: pl.BlockSpec(a.shape, lambda i: (0,) * a.ndim)
    args = (x2, g.reshape(1, -1), w_up.astype(BF16), w_down.astype(BF16), g_final.reshape(1, -1))
    return pl.pallas_call(
        functools.partial(_mlp_kernel, f_chunk=f_chunk, final_norm=final_norm),
        grid=(n // tm,),
        in_specs=[pl.BlockSpec((tm, D_MODEL), lambda i: (i, 0))] + [full(a) for a in args[1:]],
        out_specs=pl.BlockSpec((tm, D_MODEL), lambda i: (i, 0)),
        out_shape=jax.ShapeDtypeStruct((n, D_MODEL), F32),
        scratch_shapes=[pltpu.VMEM((tm, D_MODEL), F32)],
        compiler_params=_cparams(1),
        name="mlp_final" if final_norm else "mlp",
    )(*args)


def _glu_kernel(x_ref, g_ref, w_ref, b_ref, u_ref):
    h = _rms(x_ref[...], g_ref[...]).astype(BF16)
    u = jnp.dot(h, w_ref[...], preferred_element_type=F32) + b_ref[...]
    d = u.shape[1] // 2
    u_ref[...] = u[:, :d] * jax.nn.sigmoid(u[:, d:])


def _glu(x2, g, w_pw1, b_pw1, tm=512):
    n = x2.shape[0]
    full = lambda a: pl.BlockSpec(a.shape, lambda i: (0,) * a.ndim)
    args = (x2, g.reshape(1, -1), w_pw1.astype(BF16), b_pw1.reshape(1, -1))
    return pl.pallas_call(
        _glu_kernel,
        grid=(n // tm,),
        in_specs=[pl.BlockSpec((tm, D_MODEL), lambda i: (i, 0))] + [full(a) for a in args[1:]],
        out_specs=pl.BlockSpec((tm, D_MODEL), lambda i: (i, 0)),
        out_shape=jax.ShapeDtypeStruct((n, D_MODEL), F32),
        compiler_params=_cparams(1),
        name="conv_glu",
    )(*args)


HALO = 32
CONV_ROWS = 32


def _dwconv_kernel(x_ref, u_ref, halo_ref, wdw_ref, bdw_ref, lng_ref, lnb_ref, w2_ref, b2_ref, y_ref,
                   ext_ref, cv_ref):
    i = pl.program_id(1)
    tm = u_ref.shape[0]
    ext_ref[:HALO, :] = jnp.where(i == 0, 0.0, halo_ref[...])
    ext_ref[HALO:, :] = u_ref[...]
    off = HALO - (CONV_WIDTH - 1)
    for r in range(tm // CONV_ROWS):
        r0 = r * CONV_ROWS
        acc = jnp.broadcast_to(bdw_ref[...], (CONV_ROWS, D_MODEL))
        for j in range(CONV_WIDTH):
            acc = acc + wdw_ref[j:j + 1, :] * ext_ref[r0 + off + j:r0 + off + j + CONV_ROWS, :]
        cv_ref[r0:r0 + CONV_ROWS, :] = acc
    v = cv_ref[...]
    mu = jnp.mean(v, axis=-1, keepdims=True)
    var = jnp.mean(jnp.square(v - mu), axis=-1, keepdims=True)
    v = (v - mu) * lax.rsqrt(var + EPS) * lng_ref[...] + lnb_ref[...]
    v = v * jax.nn.sigmoid(v)
    y_ref[...] = (x_ref[...] + jnp.dot(v.astype(BF16), w2_ref[...], preferred_element_type=F32)
                  + b2_ref[...])


def _dwconv(x2, u2, w_dw, b_dw, ln_g, ln_b, w_pw2, b_pw2, batch, seq, tm=512):
    x3 = x2.reshape(batch, seq, D_MODEL)
    u3 = u2.reshape(batch, seq, D_MODEL)
    hb = tm // HALO
    blk = pl.BlockSpec((None, tm, D_MODEL), lambda b, i: (b, i, 0))
    halo = pl.BlockSpec((None, HALO, D_MODEL), lambda b, i: (b, jnp.maximum(i * hb - 1, 0), 0))
    full = lambda a: pl.BlockSpec(a.shape, lambda b, i: (0,) * a.ndim)
    args = (x3, u3, u3, w_dw, b_dw.reshape(1, -1), ln_g.reshape(1, -1), ln_b.reshape(1, -1),
            w_pw2.astype(BF16), b_pw2.reshape(1, -1))
    out = pl.pallas_call(
        _dwconv_kernel,
        grid=(batch, seq // tm),
        in_specs=[blk, blk, halo] + [full(a) for a in args[3:]],
        out_specs=blk,
        out_shape=jax.ShapeDtypeStruct((batch, seq, D_MODEL), F32),
        scratch_shapes=[pltpu.VMEM((HALO + tm, D_MODEL), F32), pltpu.VMEM((tm, D_MODEL), F32)],
        compiler_params=_cparams(2),
        name="conv_dw",
    )(*args)
    return out.reshape(batch * seq, D_MODEL)


def kernel(x, norm_mix, norm_mlp, norm_final, rel_bias, attn_w_in, attn_q_norm, attn_kv_norm, attn_kidx_norm, attn_w_qidx, attn_w_uq, attn_w_uk, attn_w_uv, attn_w_o, conv_w_pw1, conv_b_pw1, conv_w_dw, conv_b_dw, conv_ln_g, conv_ln_b, conv_w_pw2, conv_b_pw2, mlp_w_up, mlp_w_down):
    batch, seq, d = x.shape
    depth = norm_mix.shape[0]
    x2 = x.reshape(batch * seq, d)
    for i in range(depth):
        j = i // 2
        if i % 2 == 0:
            qidx, widx, qlat, kidx, ckv = _proj(
                x2, norm_mix[i], attn_w_in[j], attn_q_norm[j], attn_kv_norm[j], attn_kidx_norm[j],
                attn_w_qidx[j], attn_w_uq[j], attn_w_uk[j])
            olat = _attention(qidx, widx, qlat, kidx, ckv, rel_bias, batch, seq)
            x2 = _attn_out(x2, olat, attn_w_uv[j], attn_w_o[j])
        else:
            u = _glu(x2, norm_mix[i], conv_w_pw1[j], conv_b_pw1[j])
            x2 = _dwconv(x2, u, conv_w_dw[j], conv_b_dw[j], conv_ln_g[j], conv_ln_b[j],
                         conv_w_pw2[j], conv_b_pw2[j], batch, seq)
        last = i == depth - 1
        x2 = _mlp(x2, norm_mlp[i], mlp_w_up[i], mlp_w_down[i], norm_final, final_norm=last)
    if depth == 0:
        raise ValueError("depth must be positive")
    return x2.reshape(batch, seq, d)
```

```python
import functools
import math

import jax
import jax.numpy as jnp
from jax import lax
from jax.experimental import pallas as pl
from jax.experimental.pallas import tpu as pltpu

F32 = jnp.float32
BF16 = jnp.bfloat16
I32 = jnp.int32
I16 = jnp.int16

D_MODEL = 1024
N_HEADS = 16
D_NOPE = 64
D_V = 64
D_CQ = 256
D_CKV = 128
IDX_HEADS = 8
IDX_DIM = 64
TOPK_MAX = 256
CONV_WIDTH = 31
NUM_BUCKETS = 32
MAX_EXACT = 16
MAX_DISTANCE = 128
EPS = 1e-6

LANES = 128
PACKED_ROWS = 16
TQ = 128
SCORE_TILE = 512
ATT_TK = 512
BIAS_TK = 2 * TQ
PAD_KEYS = ATT_TK - TQ
NEG_MASK = -1e30
LOG2E = math.log2(math.e)
VMEM_LIMIT = 56 * 1024 * 1024


def _cparams(n_axes):
    return pltpu.CompilerParams(dimension_semantics=("arbitrary",) * n_axes,
                                vmem_limit_bytes=VMEM_LIMIT)


def _rms(x, g):
    return x * lax.rsqrt(jnp.mean(x * x, axis=-1, keepdims=True) + EPS) * g


def _proj_kernel(x_ref, g_ref, win_ref, qn_ref, kvn_ref, kin_ref, wqidx_ref, wuq_ref, wukt_ref,
                 qidx_ref, widx_ref, qlat_ref, kidx_ref, ckv_ref):
    h = _rms(x_ref[...], g_ref[...])
    proj = jnp.dot(h.astype(BF16), win_ref[...], preferred_element_type=F32)
    o1, o2, o3 = D_CQ, D_CQ + D_CKV, D_CQ + D_CKV + IDX_DIM
    cq = _rms(proj[:, :o1], qn_ref[...])
    ckv = _rms(proj[:, o1:o2], kvn_ref[...])
    kid = _rms(proj[:, o2:o3], kin_ref[...])
    widx_ref[...] = proj[:, o3:o3 + IDX_HEADS] * (IDX_HEADS ** -0.5)
    ckv_ref[...] = ckv.astype(BF16)
    kidx_ref[...] = kid.astype(BF16)
    cqb = cq.astype(BF16)
    qidx = jnp.dot(cqb, wqidx_ref[...], preferred_element_type=F32) * (IDX_DIM ** -0.5)
    qidx_ref[...] = qidx.astype(BF16)
    qh = jnp.dot(cqb, wuq_ref[...], preferred_element_type=F32).astype(BF16)
    for hh in range(N_HEADS):
        ql = jnp.dot(qh[:, hh * D_NOPE:(hh + 1) * D_NOPE], wukt_ref[hh],
                     preferred_element_type=F32) * (D_NOPE ** -0.5 * LOG2E)
        qlat_ref[:, hh * D_CKV:(hh + 1) * D_CKV] = ql.astype(BF16)


def _proj(x2, g, w_in, qn, kvn, kin, w_qidx, w_uq, w_uk, tm=512):
    n = x2.shape[0]
    ncol = w_in.shape[1]
    npad = -ncol % LANES
    win = jnp.pad(w_in, ((0, 0), (0, npad))).astype(BF16)
    wukt = jnp.transpose(w_uk, (1, 2, 0)).astype(BF16)
    full = lambda a: pl.BlockSpec(a.shape, lambda i: (0,) * a.ndim)
    row = lambda c: pl.BlockSpec((tm, c), lambda i: (i, 0))
    args = (x2, g.reshape(1, -1), win, qn.reshape(1, -1), kvn.reshape(1, -1), kin.reshape(1, -1),
            w_qidx.astype(BF16), w_uq.astype(BF16), wukt)
    return pl.pallas_call(
        _proj_kernel,
        grid=(n // tm,),
        in_specs=[row(D_MODEL)] + [full(a) for a in args[1:]],
        out_specs=[row(IDX_HEADS * IDX_DIM), row(IDX_HEADS), row(N_HEADS * D_CKV), row(IDX_DIM),
                   row(D_CKV)],
        out_shape=[jax.ShapeDtypeStruct((n, IDX_HEADS * IDX_DIM), BF16),
                   jax.ShapeDtypeStruct((n, IDX_HEADS), F32),
                   jax.ShapeDtypeStruct((n, N_HEADS * D_CKV), BF16),
                   jax.ShapeDtypeStruct((n, IDX_DIM), BF16),
                   jax.ShapeDtypeStruct((n, D_CKV), BF16)],
        compiler_params=_cparams(1),
        name="dsa_proj",
    )(*args)


def _attn_kernel(qidxr_ref, widxt_ref, qlat_ref, kidx_ref, vaug_ref, bias_ref, olat_ref,
                 khi_ref, klo_ref, klm_ref, maskt_ref, a_ref, acc_ref, m_ref, s_ref, *, k_sel,
                 idx_bits):
    i = pl.program_id(1)
    q0 = i * TQ
    n_rows = N_HEADS * TQ
    qpos = q0 + lax.broadcasted_iota(I32, (1, TQ), 1)
    krow = lax.broadcasted_iota(I32, (SCORE_TILE, TQ), 0)
    n_tiles = i // (SCORE_TILE // TQ) + 1
    i16_min = -2 ** 15

    def tile_rows(tt):
        return pl.ds(pl.multiple_of(tt * SCORE_TILE, SCORE_TILE), SCORE_TILE)

    def key_index(tt):
        return (tt * SCORE_TILE + krow).astype(I16)

    def score_body(tt, carry):
        qk = jnp.dot(kidx_ref[tile_rows(tt), :], qidxr_ref[...],
                     preferred_element_type=F32)
        sc = jnp.zeros((SCORE_TILE, TQ), F32)
        for h in range(IDX_HEADS):
            sc = sc + widxt_ref[h:h + 1, :] * jnp.maximum(qk[:, h * TQ:(h + 1) * TQ], 0.0)
        sc = jnp.where(tt * SCORE_TILE + krow <= qpos, sc, -jnp.inf)
        bits = pltpu.bitcast(sc, I32)
        key = bits ^ ((bits >> 31) & 0x7FFFFFFF)
        khi_ref[tile_rows(tt), :] = (key >> 16).astype(I16)
        klo_ref[tile_rows(tt), :] = ((key & 0xFFFF) + i16_min).astype(I16)
        return carry

    lax.fori_loop(0, n_tiles, score_body, 0)

    def count(pred):
        def body(tt, acc):
            ind = jnp.where(pred(tt), jnp.int16(1), jnp.int16(0))
            for c in range(SCORE_TILE // PACKED_ROWS):
                acc = acc + ind[c * PACKED_ROWS:(c + 1) * PACKED_ROWS]
            return acc
        acc = lax.fori_loop(0, n_tiles, body, jnp.zeros((PACKED_ROWS, TQ), I16))
        return jnp.sum(acc.astype(I32), axis=0, keepdims=True)

    def half_step(step, cand, cntc, ref, target):
        trial = cand + (jnp.int32(1) << (15 - step))
        t16 = trial.astype(I16)
        cnt = count(lambda tt: ref[tile_rows(tt), :] >= t16)
        ok = cnt >= target
        return jnp.where(ok, trial, cand), jnp.where(ok, cnt, cntc)

    cand0 = jnp.full((1, TQ), i16_min, I32)
    hi_t, cnt_ge_hi = lax.fori_loop(
        0, 16, lambda s, st: half_step(s, st[0], st[1], khi_ref, k_sel),
        (cand0, jnp.full((1, TQ), n_tiles * SCORE_TILE, I32)))
    hi16 = hi_t.astype(I16)
    cnt_gt_hi = count(lambda tt: khi_ref[tile_rows(tt), :] > hi16)
    need_lo = k_sel - cnt_gt_hi

    def low_body(tt, carry):
        rows = tile_rows(tt)
        klm_ref[rows, :] = jnp.where(khi_ref[rows, :] == hi16, klo_ref[rows, :], jnp.int16(i16_min))
        return carry

    lax.fori_loop(0, n_tiles, low_body, 0)
    trivial = qpos < k_sel

    def n_open(cntc):
        return jnp.sum(jnp.where(jnp.logical_and(cntc != need_lo, jnp.logical_not(trivial)), 1, 0))

    def low_pair(st):
        step, cand, cntc, _ = st
        cand, cntc = half_step(step, cand, cntc, klm_ref, need_lo)
        cand, cntc = half_step(step + 1, cand, cntc, klm_ref, need_lo)
        return step + 2, cand, cntc, n_open(cntc)

    cnt_eq_hi = cnt_ge_hi - cnt_gt_hi
    _, lo_t, cnt_sel_lo, _ = lax.while_loop(
        lambda st: jnp.logical_and(st[0] < 16, st[3] > 0), low_pair,
        (jnp.int32(0), cand0, cnt_eq_hi, n_open(cnt_eq_hi)))
    hi_t = jnp.where(trivial, i16_min, hi_t)
    lo_t = jnp.where(trivial, i16_min, lo_t)
    hi16 = hi_t.astype(I16)
    lo16 = lo_t.astype(I16)

    def tie_keys(tt):
        rows = tile_rows(tt)
        return jnp.logical_and(khi_ref[rows, :] == hi16, klo_ref[rows, :] == lo16)

    excess = jnp.logical_and(cnt_sel_lo > need_lo, jnp.logical_not(trivial))
    big_j = jnp.int32(2 ** 15 - 1)

    def tie_fn():
        above = count(lambda tt: jnp.logical_and(khi_ref[tile_rows(tt), :] == hi16,
                                                 klo_ref[tile_rows(tt), :] > lo16))
        need = need_lo - above

        def jb(b, jv):
            trial = jv | (jnp.int32(1) << (idx_bits - 1 - b))
            t16 = trial.astype(I16)
            c = count(lambda tt: jnp.logical_and(tie_keys(tt), key_index(tt) < t16))
            return jnp.where(c < need, trial, jv)

        jv = lax.fori_loop(0, idx_bits, jb, jnp.zeros((1, TQ), I32))
        return jnp.where(excess, jv, big_j)

    jv = lax.cond(jnp.sum(jnp.where(excess, 1, 0)) > 0, tie_fn,
                  lambda: jnp.full((1, TQ), big_j, I32))

    jv16 = jv.astype(I16)
    qpos16 = qpos.astype(I16)

    def mask_body(tt, carry):
        rows = tile_rows(tt)
        khi = khi_ref[rows, :]
        klo = klo_ref[rows, :]
        idx = key_index(tt)
        at_lo = jnp.logical_or(klo > lo16, jnp.logical_and(klo == lo16, idx <= jv16))
        sel = jnp.logical_or(khi > hi16, jnp.logical_and(khi == hi16, at_lo))
        sel = jnp.logical_and(sel, idx <= qpos16)
        out_rows = pl.ds(pl.multiple_of(PAD_KEYS + tt * SCORE_TILE, TQ), SCORE_TILE)
        maskt_ref[out_rows, :] = jnp.where(sel, jnp.asarray(0.0, BF16), jnp.asarray(NEG_MASK, BF16))
        return carry

    lax.fori_loop(0, n_tiles, mask_body, 0)
    maskt_ref[:PAD_KEYS, :] = jnp.full((PAD_KEYS, TQ), NEG_MASK, BF16)

    eye = (lax.broadcasted_iota(I32, (TQ, TQ), 0) == lax.broadcasted_iota(I32, (TQ, TQ), 1))
    eye = jnp.where(eye, 1.0, 0.0).astype(BF16)
    for h in range(N_HEADS):
        a_ref[h * TQ:(h + 1) * TQ, :D_CKV] = qlat_ref[:, h * D_CKV:(h + 1) * D_CKV]
        a_ref[h * TQ:(h + 1) * TQ, D_CKV:] = eye
    m_ref[...] = jnp.full((n_rows, LANES), -jnp.inf, F32)
    acc_ref[...] = jnp.zeros((n_rows, 2 * D_CKV), F32)
    n_chunks = ATT_TK // LANES

    def key_rows(j):
        r0 = jnp.maximum(q0 + TQ + PAD_KEYS - (j + 1) * ATT_TK, 0)
        return pl.ds(pl.multiple_of(r0, TQ), ATT_TK)

    def logits(j, slot):
        rows = key_rows(j)
        rhs = jnp.concatenate([vaug_ref[rows, :D_CKV], maskt_ref[rows, :]], axis=1)
        s_ref[slot] = lax.dot_general(a_ref[...], rhs, (((1,), (1,)), ((), ())),
                                      preferred_element_type=F32)

    def consume(j, slot):
        vk = vaug_ref[key_rows(j), :]
        near = jnp.minimum(j, 1)
        ps = []
        for h in range(N_HEADS):
            rows = slice(h * TQ, (h + 1) * TQ)
            sh = s_ref[slot, rows, :]
            sh = jnp.concatenate([sh[:, :ATT_TK - BIAS_TK],
                                  sh[:, ATT_TK - BIAS_TK:] + bias_ref[near, h]], axis=1)
            m_old = m_ref[rows, :]
            m_new = jnp.maximum(m_old, jnp.max(sh, axis=1, keepdims=True))
            alpha = jnp.exp2(m_old - m_new)
            m_ref[rows, :] = m_new
            p = jnp.exp2(sh - jnp.concatenate([m_new] * n_chunks, axis=1))
            ps.append(p.astype(BF16))
            acc_ref[rows, :] = acc_ref[rows, :] * jnp.concatenate([alpha, alpha], axis=1)
        acc_ref[...] += jnp.dot(jnp.concatenate(ps, axis=0), vk, preferred_element_type=F32)

    n_steps = (q0 + TQ + ATT_TK - 1) // ATT_TK
    logits(0, 0)

    def att_pair(jj, carry):
        j = 2 * jj
        logits(j + 1, 1)
        consume(j, 0)
        logits(j + 2, 0)
        consume(j + 1, 1)
        return carry

    lax.fori_loop(0, n_steps // 2, att_pair, 0)

    @pl.when(n_steps % 2 == 1)
    def _():
        consume(n_steps - 1, 0)

    for h in range(N_HEADS):
        acc = acc_ref[h * TQ:(h + 1) * TQ, :]
        olat_ref[:, h * D_CKV:(h + 1) * D_CKV] = (acc[:, :D_CKV] / acc[:, D_CKV:]).astype(BF16)


def _t5_bucket(dist):
    n = jnp.maximum(dist, 0)
    nf = jnp.maximum(n, 1).astype(F32)
    large = MAX_EXACT + (jnp.log(nf / MAX_EXACT) / math.log(MAX_DISTANCE / MAX_EXACT)
                         * (NUM_BUCKETS - MAX_EXACT)).astype(I32)
    large = jnp.minimum(large, NUM_BUCKETS - 1)
    return jnp.where(n < MAX_EXACT, n, large)


def _near_bias(rel_bias):
    assert BIAS_TK >= TQ + MAX_DISTANCE - 1
    period = TQ + BIAS_TK
    e = jnp.arange(period)
    e = jnp.where(e < BIAS_TK, e, e - period)
    tab = rel_bias[_t5_bucket(BIAS_TK - TQ - e)] - rel_bias[NUM_BUCKETS - 1]
    flat = jnp.tile(tab, (TQ, 1))[:TQ * (period - 1)]
    near = flat.reshape(TQ, period - 1, N_HEADS)[:, :BIAS_TK]
    return (jnp.transpose(near, (2, 0, 1)) * LOG2E).astype(F32)


def _attention(qidx, widx, qlat, kidx, ckv, rel_bias, batch, seq):
    assert seq % SCORE_TILE == 0 and SCORE_TILE % TQ == 0 and ATT_TK % TQ == 0
    assert seq < 2 ** 15
    k_sel = min(TOPK_MAX, seq // 4)
    nq = seq // TQ
    r3 = lambda a: a.reshape(batch, seq, a.shape[-1])
    qidxr = jnp.transpose(qidx.reshape(batch, nq, TQ, IDX_HEADS, IDX_DIM), (0, 1, 4, 3, 2))
    qidxr = qidxr.reshape(batch, nq, IDX_DIM, IDX_HEADS * TQ)
    widxt = jnp.transpose(widx.reshape(batch, nq, TQ, IDX_HEADS), (0, 1, 3, 2))
    ckv3 = jnp.pad(r3(ckv), ((0, 0), (PAD_KEYS, 0), (0, 0)))
    vaug = jnp.concatenate([ckv3, jnp.ones_like(ckv3)], axis=-1)
    bias = _near_bias(rel_bias)
    bias = jnp.stack([bias, jnp.zeros_like(bias)])

    qblk = lambda c: pl.BlockSpec((None, TQ, c), lambda b, i: (b, i, 0))
    qtile = lambda a: pl.BlockSpec((None, None) + a.shape[2:], lambda b, i: (b, i, 0, 0))
    per_b = lambda a: pl.BlockSpec((None,) + a.shape[1:], lambda b, i: (b,) + (0,) * (a.ndim - 1))
    n_rows = N_HEADS * TQ
    kidx3 = r3(kidx)
    out = pl.pallas_call(
        functools.partial(_attn_kernel, k_sel=k_sel, idx_bits=int(seq).bit_length()),
        grid=(batch, nq),
        in_specs=[qtile(qidxr), qtile(widxt), qblk(N_HEADS * D_CKV), per_b(kidx3), per_b(vaug),
                  pl.BlockSpec(bias.shape, lambda b, i: (0, 0, 0, 0),
                               pipeline_mode=pl.Buffered(1))],
        out_specs=qblk(N_HEADS * D_CKV),
        out_shape=jax.ShapeDtypeStruct((batch, seq, N_HEADS * D_CKV), BF16),
        scratch_shapes=[
            pltpu.VMEM((seq, TQ), I16),
            pltpu.VMEM((seq, TQ), I16),
            pltpu.VMEM((seq, TQ), I16),
            pltpu.VMEM((seq + PAD_KEYS, TQ), BF16),
            pltpu.VMEM((n_rows, 2 * D_CKV), BF16),
            pltpu.VMEM((n_rows, 2 * D_CKV), F32),
            pltpu.VMEM((n_rows, LANES), F32),
            pltpu.VMEM((2, n_rows, ATT_TK), F32),
        ],
        compiler_params=_cparams(2),
        name="dsa_attn",
    )(qidxr, widxt, r3(qlat), kidx3, vaug, bias)
    return out.reshape(batch * seq, N_HEADS * D_CKV)


def _attn_out_kernel(x_ref, olat_ref, wuv_ref, wo_ref, y_ref):
    o = jnp.dot(olat_ref[...], wuv_ref[...], preferred_element_type=F32)
    y_ref[...] = x_ref[...] + jnp.dot(o.astype(BF16), wo_ref[...], preferred_element_type=F32)


def _attn_out(x2, olat, w_uv, w_o, tm=512):
    n = x2.shape[0]
    wuv = jnp.transpose(w_uv, (1, 0, 2))
    eye = jnp.eye(N_HEADS, dtype=w_uv.dtype)
    wbd = (wuv[:, :, None, :] * eye[:, None, :, None]).reshape(N_HEADS * D_CKV, N_HEADS * D_V)
    full = lambda a: pl.BlockSpec(a.shape, lambda i: (0,) * a.ndim)
    args = (x2, olat, wbd.astype(BF16), w_o.astype(BF16))
    return pl.pallas_call(
        _attn_out_kernel,
        grid=(n // tm,),
        in_specs=[pl.BlockSpec((tm, D_MODEL), lambda i: (i, 0)),
                  pl.BlockSpec((tm, N_HEADS * D_CKV), lambda i: (i, 0)), full(args[2]), full(args[3])],
        out_specs=pl.BlockSpec((tm, D_MODEL), lambda i: (i, 0)),
        out_shape=jax.ShapeDtypeStruct((n, D_MODEL), F32),
        compiler_params=_cparams(1),
        name="attn_out",
    )(*args)


def _mlp_kernel(x_ref, g_ref, wup_ref, wdn_ref, gf_ref, y_ref, acc_ref, *, f_chunk, final_norm):
    x = x_ref[...]
    h = _rms(x, g_ref[...]).astype(BF16)
    d_ff = wup_ref.shape[1]
    for c in range(d_ff // f_chunk):
        u = jnp.dot(h, wup_ref[:, c * f_chunk:(c + 1) * f_chunk], preferred_element_type=F32)
        a = jnp.square(jnp.maximum(u, 0.0)).astype(BF16)
        d = jnp.dot(a, wdn_ref[c * f_chunk:(c + 1) * f_chunk, :], preferred_element_type=F32)
        if c == 0:
            acc_ref[...] = d
        else:
            acc_ref[...] += d
    y = x + acc_ref[...]
    if final_norm:
        y = _rms(y, gf_ref[...])
    y_ref[...] = y


def _mlp(x2, g, w_up, w_down, g_final, final_norm, tm=512, f_chunk=512):
    n = x2.shape[0]
    full = lambda a: pl.BlockSpec(a.shape, lambda i: (0,) * a.ndim)
    args = (x2, g.reshape(1, -1), w_up.astype(BF16), w_down.astype(BF16), g_final.reshape(1, -1))
    return pl.pallas_call(
        functools.partial(_mlp_kernel, f_chunk=f_chunk, final_norm=final_norm),
        grid=(n // tm,),
        in_specs=[pl.BlockSpec((tm, D_MODEL), lambda i: (i, 0))] + [full(a) for a in args[1:]],
        out_specs=pl.BlockSpec((tm, D_MODEL), lambda i: (i, 0)),
        out_shape=jax.ShapeDtypeStruct((n, D_MODEL), F32),
        scratch_shapes=[pltpu.VMEM((tm, D_MODEL), F32)],
        compiler_params=_cparams(1),
        name="mlp_final" if final_norm else "mlp",
    )(*args)


def _glu_kernel(x_ref, g_ref, w_ref, b_ref, u_ref):
    h = _rms(x_ref[...], g_ref[...]).astype(BF16)
    u = jnp.dot(h, w_ref[...], preferred_element_type=F32) + b_ref[...]
    d = u.shape[1] // 2
    u_ref[...] = u[:, :d] * jax.nn.sigmoid(u[:, d:])


def _glu(x2, g, w_pw1, b_pw1, tm=512):
    n = x2.shape[0]
    full = lambda a: pl.BlockSpec(a.shape, lambda i: (0,) * a.ndim)
    args = (x2, g.reshape(1, -1), w_pw1.astype(BF16), b_pw1.reshape(1, -1))
    return pl.pallas_call(
        _glu_kernel,
        grid=(n // tm,),
        in_specs=[pl.BlockSpec((tm, D_MODEL), lambda i: (i, 0))] + [full(a) for a in args[1:]],
        out_specs=pl.BlockSpec((tm, D_MODEL), lambda i: (i, 0)),
        out_shape=jax.ShapeDtypeStruct((n, D_MODEL), F32),
        compiler_params=_cparams(1),
        name="conv_glu",
    )(*args)


HALO = 32
CONV_ROWS = 32


def _dwconv_kernel(x_ref, u_ref, halo_ref, wdw_ref, bdw_ref, lng_ref, lnb_ref, w2_ref, b2_ref, y_ref,
                   ext_ref, cv_ref):
    i = pl.program_id(1)
    tm = u_ref.shape[0]
    ext_ref[:HALO, :] = jnp.where(i == 0, 0.0, halo_ref[...])
    ext_ref[HALO:, :] = u_ref[...]
    off = HALO - (CONV_WIDTH - 1)
    for r in range(tm // CONV_ROWS):
        r0 = r * CONV_ROWS
        acc = jnp.broadcast_to(bdw_ref[...], (CONV_ROWS, D_MODEL))
        for j in range(CONV_WIDTH):
            acc = acc + wdw_ref[j:j + 1, :] * ext_ref[r0 + off + j:r0 + off + j + CONV_ROWS, :]
        cv_ref[r0:r0 + CONV_ROWS, :] = acc
    v = cv_ref[...]
    mu = jnp.mean(v, axis=-1, keepdims=True)
    var = jnp.mean(jnp.square(v - mu), axis=-1, keepdims=True)
    v = (v - mu) * lax.rsqrt(var + EPS) * lng_ref[...] + lnb_ref[...]
    v = v * jax.nn.sigmoid(v)
    y_ref[...] = (x_ref[...] + jnp.dot(v.astype(BF16), w2_ref[...], preferred_element_type=F32)
                  + b2_ref[...])


def _dwconv(x2, u2, w_dw, b_dw, ln_g, ln_b, w_pw2, b_pw2, batch, seq, tm=512):
    x3 = x2.reshape(batch, seq, D_MODEL)
    u3 = u2.reshape(batch, seq, D_MODEL)
    hb = tm // HALO
    blk = pl.BlockSpec((None, tm, D_MODEL), lambda b, i: (b, i, 0))
    halo = pl.BlockSpec((None, HALO, D_MODEL), lambda b, i: (b, jnp.maximum(i * hb - 1, 0), 0))
    full = lambda a: pl.BlockSpec(a.shape, lambda b, i: (0,) * a.ndim)
    args = (x3, u3, u3, w_dw, b_dw.reshape(1, -1), ln_g.reshape(1, -1), ln_b.reshape(1, -1),
            w_pw2.astype(BF16), b_pw2.reshape(1, -1))
    out = pl.pallas_call(
        _dwconv_kernel,
        grid=(batch, seq // tm),
        in_specs=[blk, blk, halo] + [full(a) for a in args[3:]],
        out_specs=blk,
        out_shape=jax.ShapeDtypeStruct((batch, seq, D_MODEL), F32),
        scratch_shapes=[pltpu.VMEM((HALO + tm, D_MODEL), F32), pltpu.VMEM((tm, D_MODEL), F32)],
        compiler_params=_cparams(2),
        name="conv_dw",
    )(*args)
    return out.reshape(batch * seq, D_MODEL)


def kernel(x, norm_mix, norm_mlp, norm_final, rel_bias, attn_w_in, attn_q_norm, attn_kv_norm, attn_kidx_norm, attn_w_qidx, attn_w_uq, attn_w_uk, attn_w_uv, attn_w_o, conv_w_pw1, conv_b_pw1, conv_w_dw, conv_b_dw, conv_ln_g, conv_ln_b, conv_w_pw2, conv_b_pw2, mlp_w_up, mlp_w_down):
    batch, seq, d = x.shape
    depth = norm_mix.shape[0]
    x2 = x.reshape(batch * seq, d)
    for i in range(depth):
        j = i // 2
        if i % 2 == 0:
            qidx, widx, qlat, kidx, ckv = _proj(
                x2, norm_mix[i], attn_w_in[j], attn_q_norm[j], attn_kv_norm[j], attn_kidx_norm[j],
                attn_w_qidx[j], attn_w_uq[j], attn_w_uk[j])
            olat = _attention(qidx, widx, qlat, kidx, ckv, rel_bias, batch, seq)
            x2 = _attn_out(x2, olat, attn_w_uv[j], attn_w_o[j])
        else:
            u = _glu(x2, norm_mix[i], conv_w_pw1[j], conv_b_pw1[j])
            x2 = _dwconv(x2, u, conv_w_dw[j], conv_b_dw[j], conv_ln_g[j], conv_ln_b[j],
                         conv_w_pw2[j], conv_b_pw2[j], batch, seq)
        last = i == depth - 1
        x2 = _mlp(x2, norm_mlp[i], mlp_w_up[i], mlp_w_down[i], norm_final, final_norm=last)
    if depth == 0:
        raise ValueError("depth must be positive")
    return x2.reshape(batch, seq, d)
```

```python
import functools
import math

import jax
import jax.numpy as jnp
from jax import lax
from jax.experimental import pallas as pl
from jax.experimental.pallas import tpu as pltpu

F32 = jnp.float32
BF16 = jnp.bfloat16
I32 = jnp.int32
I16 = jnp.int16

D_MODEL = 1024
N_HEADS = 16
D_NOPE = 64
D_V = 64
D_CQ = 256
D_CKV = 128
IDX_HEADS = 8
IDX_DIM = 64
TOPK_MAX = 256
CONV_WIDTH = 31
NUM_BUCKETS = 32
MAX_EXACT = 16
MAX_DISTANCE = 128
EPS = 1e-6

LANES = 128
PACKED_ROWS = 16
TQ = 128
SCORE_TILE = 512
ATT_TK = 512
BIAS_TK = 2 * TQ
PAD_KEYS = ATT_TK - TQ
NEG_MASK = -1e30
LOG2E = math.log2(math.e)
LOW_BITS_PER_TEST = 4
VMEM_LIMIT = 56 * 1024 * 1024


def _cparams(n_axes):
    return pltpu.CompilerParams(dimension_semantics=("arbitrary",) * n_axes,
                                vmem_limit_bytes=VMEM_LIMIT)


def _rms(x, g):
    return x * lax.rsqrt(jnp.mean(x * x, axis=-1, keepdims=True) + EPS) * g


def _proj_kernel(x_ref, g_ref, win_ref, qn_ref, kvn_ref, kin_ref, wqidx_ref, wuq_ref, wukt_ref,
                 qidx_ref, widx_ref, qlat_ref, kidx_ref, ckv_ref):
    h = _rms(x_ref[...], g_ref[...])
    proj = jnp.dot(h.astype(BF16), win_ref[...], preferred_element_type=F32)
    o1, o2, o3 = D_CQ, D_CQ + D_CKV, D_CQ + D_CKV + IDX_DIM
    cq = _rms(proj[:, :o1], qn_ref[...])
    ckv = _rms(proj[:, o1:o2], kvn_ref[...])
    kid = _rms(proj[:, o2:o3], kin_ref[...])
    widx_ref[...] = proj[:, o3:o3 + IDX_HEADS] * (IDX_HEADS ** -0.5)
    ckv_ref[...] = ckv.astype(BF16)
    kidx_ref[...] = kid.astype(BF16)
    cqb = cq.astype(BF16)
    qidx = jnp.dot(cqb, wqidx_ref[...], preferred_element_type=F32) * (IDX_DIM ** -0.5)
    qidx_ref[...] = qidx.astype(BF16)
    qh = jnp.dot(cqb, wuq_ref[...], preferred_element_type=F32).astype(BF16)
    for hh in range(N_HEADS):
        ql = jnp.dot(qh[:, hh * D_NOPE:(hh + 1) * D_NOPE], wukt_ref[hh],
                     preferred_element_type=F32) * (D_NOPE ** -0.5 * LOG2E)
        qlat_ref[:, hh * D_CKV:(hh + 1) * D_CKV] = ql.astype(BF16)


def _proj(x2, g, w_in, qn, kvn, kin, w_qidx, w_uq, w_uk, tm=512):
    n = x2.shape[0]
    ncol = w_in.shape[1]
    npad = -ncol % LANES
    win = jnp.pad(w_in, ((0, 0), (0, npad))).astype(BF16)
    wukt = jnp.transpose(w_uk, (1, 2, 0)).astype(BF16)
    full = lambda a: pl.BlockSpec(a.shape, lambda i: (0,) * a.ndim)
    row = lambda c: pl.BlockSpec((tm, c), lambda i: (i, 0))
    args = (x2, g.reshape(1, -1), win, qn.reshape(1, -1), kvn.reshape(1, -1), kin.reshape(1, -1),
            w_qidx.astype(BF16), w_uq.astype(BF16), wukt)
    return pl.pallas_call(
        _proj_kernel,
        grid=(n // tm,),
        in_specs=[row(D_MODEL)] + [full(a) for a in args[1:]],
        out_specs=[row(IDX_HEADS * IDX_DIM), row(IDX_HEADS), row(N_HEADS * D_CKV), row(IDX_DIM),
                   row(D_CKV)],
        out_shape=[jax.ShapeDtypeStruct((n, IDX_HEADS * IDX_DIM), BF16),
                   jax.ShapeDtypeStruct((n, IDX_HEADS), F32),
                   jax.ShapeDtypeStruct((n, N_HEADS * D_CKV), BF16),
                   jax.ShapeDtypeStruct((n, IDX_DIM), BF16),
                   jax.ShapeDtypeStruct((n, D_CKV), BF16)],
        compiler_params=_cparams(1),
        name="dsa_proj",
    )(*args)


def _attn_kernel(qidxr_ref, widxt_ref, qlat_ref, kidx_ref, vaug_ref, bias_ref, olat_ref,
                 khi_ref, klo_ref, klm_ref, maskt_ref, a_ref, acc_ref, m_ref, s_ref, *, k_sel,
                 idx_bits):
    i = pl.program_id(1)
    q0 = i * TQ
    n_rows = N_HEADS * TQ
    qpos = q0 + lax.broadcasted_iota(I32, (1, TQ), 1)
    krow = lax.broadcasted_iota(I32, (SCORE_TILE, TQ), 0)
    n_tiles = i // (SCORE_TILE // TQ) + 1
    i16_min = -2 ** 15

    def tile_rows(tt):
        return pl.ds(pl.multiple_of(tt * SCORE_TILE, SCORE_TILE), SCORE_TILE)

    def key_index(tt):
        return (tt * SCORE_TILE + krow).astype(I16)

    def score_body(tt, carry):
        qk = jnp.dot(kidx_ref[tile_rows(tt), :], qidxr_ref[...],
                     preferred_element_type=F32)
        sc = jnp.zeros((SCORE_TILE, TQ), F32)
        for h in range(IDX_HEADS):
            sc = sc + widxt_ref[h:h + 1, :] * jnp.maximum(qk[:, h * TQ:(h + 1) * TQ], 0.0)
        sc = jnp.where(tt * SCORE_TILE + krow <= qpos, sc, -jnp.inf)
        bits = pltpu.bitcast(sc, I32)
        key = bits ^ ((bits >> 31) & 0x7FFFFFFF)
        khi_ref[tile_rows(tt), :] = (key >> 16).astype(I16)
        klo_ref[tile_rows(tt), :] = ((key & 0xFFFF) + i16_min).astype(I16)
        return carry

    lax.fori_loop(0, n_tiles, score_body, 0)

    def count(pred):
        def body(tt, acc):
            ind = jnp.where(pred(tt), jnp.int16(1), jnp.int16(0))
            parts = [ind[c * PACKED_ROWS:(c + 1) * PACKED_ROWS]
                     for c in range(SCORE_TILE // PACKED_ROWS)]
            while len(parts) > 1:
                parts = [a + b for a, b in zip(parts[::2], parts[1::2])]
            return acc + parts[0]
        acc = lax.fori_loop(0, n_tiles, body, jnp.zeros((PACKED_ROWS, TQ), I16))
        return jnp.sum(acc.astype(I32), axis=0, keepdims=True)

    def half_step(step, cand, cntc, ref, target):
        trial = cand + (jnp.int32(1) << (15 - step))
        t16 = trial.astype(I16)
        cnt = count(lambda tt: ref[tile_rows(tt), :] >= t16)
        ok = cnt >= target
        return jnp.where(ok, trial, cand), jnp.where(ok, cnt, cntc)

    cand0 = jnp.full((1, TQ), i16_min, I32)
    hi_t, cnt_ge_hi = lax.fori_loop(
        0, 16, lambda s, st: half_step(s, st[0], st[1], khi_ref, k_sel),
        (cand0, jnp.full((1, TQ), n_tiles * SCORE_TILE, I32)))
    hi16 = hi_t.astype(I16)
    cnt_gt_hi = count(lambda tt: khi_ref[tile_rows(tt), :] > hi16)
    need_lo = k_sel - cnt_gt_hi

    def low_body(tt, carry):
        rows = tile_rows(tt)
        klm_ref[rows, :] = jnp.where(khi_ref[rows, :] == hi16, klo_ref[rows, :], jnp.int16(i16_min))
        return carry

    lax.fori_loop(0, n_tiles, low_body, 0)
    trivial = qpos < k_sel

    def n_open(cntc):
        return jnp.sum(jnp.where(jnp.logical_and(cntc != need_lo, jnp.logical_not(trivial)), 1, 0))

    def low_group(st):
        step, cand, cntc, _ = st
        for b in range(LOW_BITS_PER_TEST):
            cand, cntc = half_step(step + b, cand, cntc, klm_ref, need_lo)
        return step + LOW_BITS_PER_TEST, cand, cntc, n_open(cntc)

    cnt_eq_hi = cnt_ge_hi - cnt_gt_hi
    _, lo_t, cnt_sel_lo, _ = lax.while_loop(
        lambda st: jnp.logical_and(st[0] < 16, st[3] > 0), low_group,
        (jnp.int32(0), cand0, cnt_eq_hi, n_open(cnt_eq_hi)))
    hi_t = jnp.where(trivial, i16_min, hi_t)
    lo_t = jnp.where(trivial, i16_min, lo_t)
    hi16 = hi_t.astype(I16)
    lo16 = lo_t.astype(I16)

    def tie_keys(tt):
        rows = tile_rows(tt)
        return jnp.logical_and(khi_ref[rows, :] == hi16, klo_ref[rows, :] == lo16)

    excess = jnp.logical_and(cnt_sel_lo > need_lo, jnp.logical_not(trivial))
    big_j = jnp.int32(2 ** 15 - 1)

    def tie_fn():
        above = count(lambda tt: jnp.logical_and(khi_ref[tile_rows(tt), :] == hi16,
                                                 klo_ref[tile_rows(tt), :] > lo16))
        need = need_lo - above

        def jb(b, jv):
            trial = jv | (jnp.int32(1) << (idx_bits - 1 - b))
            t16 = trial.astype(I16)
            c = count(lambda tt: jnp.logical_and(tie_keys(tt), key_index(tt) < t16))
            return jnp.where(c < need, trial, jv)

        jv = lax.fori_loop(0, idx_bits, jb, jnp.zeros((1, TQ), I32))
        return jnp.where(excess, jv, big_j)

    jv = lax.cond(jnp.sum(jnp.where(excess, 1, 0)) > 0, tie_fn,
                  lambda: jnp.full((1, TQ), big_j, I32))

    jv16 = jv.astype(I16)
    qpos16 = qpos.astype(I16)

    def mask_body(tt, carry):
        rows = tile_rows(tt)
        khi = khi_ref[rows, :]
        klo = klo_ref[rows, :]
        idx = key_index(tt)
        at_lo = jnp.logical_or(klo > lo16, jnp.logical_and(klo == lo16, idx <= jv16))
        sel = jnp.logical_or(khi > hi16, jnp.logical_and(khi == hi16, at_lo))
        sel = jnp.logical_and(sel, idx <= qpos16)
        out_rows = pl.ds(pl.multiple_of(PAD_KEYS + tt * SCORE_TILE, TQ), SCORE_TILE)
        maskt_ref[out_rows, :] = jnp.where(sel, jnp.asarray(0.0, BF16), jnp.asarray(NEG_MASK, BF16))
        return carry

    lax.fori_loop(0, n_tiles, mask_body, 0)
    maskt_ref[:PAD_KEYS, :] = jnp.full((PAD_KEYS, TQ), NEG_MASK, BF16)

    eye = (lax.broadcasted_iota(I32, (TQ, TQ), 0) == lax.broadcasted_iota(I32, (TQ, TQ), 1))
    eye = jnp.where(eye, 1.0, 0.0).astype(BF16)
    for h in range(N_HEADS):
        a_ref[h * TQ:(h + 1) * TQ, :D_CKV] = qlat_ref[:, h * D_CKV:(h + 1) * D_CKV]
        a_ref[h * TQ:(h + 1) * TQ, D_CKV:] = eye
    m_ref[...] = jnp.full((n_rows, LANES), -jnp.inf, F32)
    acc_ref[...] = jnp.zeros((n_rows, 2 * D_CKV), F32)
    n_chunks = ATT_TK // LANES

    def key_rows(j):
        r0 = jnp.maximum(q0 + TQ + PAD_KEYS - (j + 1) * ATT_TK, 0)
        return pl.ds(pl.multiple_of(r0, TQ), ATT_TK)

    def logits(j, slot):
        rows = key_rows(j)
        rhs = jnp.concatenate([vaug_ref[rows, :D_CKV], maskt_ref[rows, :]], axis=1)
        s_ref[slot] = lax.dot_general(a_ref[...], rhs, (((1,), (1,)), ((), ())),
                                      preferred_element_type=F32)

    def consume(j, slot):
        vk = vaug_ref[key_rows(j), :]
        near = jnp.minimum(j, 1)
        ps = []
        for h in range(N_HEADS):
            rows = slice(h * TQ, (h + 1) * TQ)
            sh = s_ref[slot, rows, :]
            sh = jnp.concatenate([sh[:, :ATT_TK - BIAS_TK],
                                  sh[:, ATT_TK - BIAS_TK:] + bias_ref[near, h]], axis=1)
            m_old = m_ref[rows, :]
            m_new = jnp.maximum(m_old, jnp.max(sh, axis=1, keepdims=True))
            alpha = jnp.exp2(m_old - m_new)
            m_ref[rows, :] = m_new
            p = jnp.exp2(sh - jnp.concatenate([m_new] * n_chunks, axis=1))
            ps.append(p.astype(BF16))
            acc_ref[rows, :] = acc_ref[rows, :] * jnp.concatenate([alpha, alpha], axis=1)
        acc_ref[...] += jnp.dot(jnp.concatenate(ps, axis=0), vk, preferred_element_type=F32)

    n_steps = (q0 + TQ + ATT_TK - 1) // ATT_TK
    logits(0, 0)

    def att_pair(jj, carry):
        j = 2 * jj
        logits(j + 1, 1)
        consume(j, 0)
        logits(j + 2, 0)
        consume(j + 1, 1)
        return carry

    lax.fori_loop(0, n_steps // 2, att_pair, 0)

    @pl.when(n_steps % 2 == 1)
    def _():
        consume(n_steps - 1, 0)

    for h in range(N_HEADS):
        acc = acc_ref[h * TQ:(h + 1) * TQ, :]
        olat_ref[:, h * D_CKV:(h + 1) * D_CKV] = (acc[:, :D_CKV] / acc[:, D_CKV:]).astype(BF16)


def _t5_bucket(dist):
    n = jnp.maximum(dist, 0)
    nf = jnp.maximum(n, 1).astype(F32)
    large = MAX_EXACT + (jnp.log(nf / MAX_EXACT) / math.log(MAX_DISTANCE / MAX_EXACT)
                         * (NUM_BUCKETS - MAX_EXACT)).astype(I32)
    large = jnp.minimum(large, NUM_BUCKETS - 1)
    return jnp.where(n < MAX_EXACT, n, large)


def _near_bias(rel_bias):
    assert BIAS_TK >= TQ + MAX_DISTANCE - 1
    period = TQ + BIAS_TK
    e = jnp.arange(period)
    e = jnp.where(e < BIAS_TK, e, e - period)
    tab = rel_bias[_t5_bucket(BIAS_TK - TQ - e)] - rel_bias[NUM_BUCKETS - 1]
    flat = jnp.tile(tab, (TQ, 1))[:TQ * (period - 1)]
    near = flat.reshape(TQ, period - 1, N_HEADS)[:, :BIAS_TK]
    return (jnp.transpose(near, (2, 0, 1)) * LOG2E).astype(F32)


def _attention(qidx, widx, qlat, kidx, ckv, rel_bias, batch, seq):
    assert seq % SCORE_TILE == 0 and SCORE_TILE % TQ == 0 and ATT_TK % TQ == 0
    assert seq < 2 ** 15
    k_sel = min(TOPK_MAX, seq // 4)
    nq = seq // TQ
    r3 = lambda a: a.reshape(batch, seq, a.shape[-1])
    qidxr = jnp.transpose(qidx.reshape(batch, nq, TQ, IDX_HEADS, IDX_DIM), (0, 1, 4, 3, 2))
    qidxr = qidxr.reshape(batch, nq, IDX_DIM, IDX_HEADS * TQ)
    widxt = jnp.transpose(widx.reshape(batch, nq, TQ, IDX_HEADS), (0, 1, 3, 2))
    ckv3 = jnp.pad(r3(ckv), ((0, 0), (PAD_KEYS, 0), (0, 0)))
    vaug = jnp.concatenate([ckv3, jnp.ones_like(ckv3)], axis=-1)
    bias = _near_bias(rel_bias)
    bias = jnp.stack([bias, jnp.zeros_like(bias)])

    qblk = lambda c: pl.BlockSpec((None, TQ, c), lambda b, i: (b, i, 0))
    qtile = lambda a: pl.BlockSpec((None, None) + a.shape[2:], lambda b, i: (b, i, 0, 0))
    per_b = lambda a: pl.BlockSpec((None,) + a.shape[1:], lambda b, i: (b,) + (0,) * (a.ndim - 1))
    n_rows = N_HEADS * TQ
    kidx3 = r3(kidx)
    out = pl.pallas_call(
        functools.partial(_attn_kernel, k_sel=k_sel, idx_bits=int(seq).bit_length()),
        grid=(batch, nq),
        in_specs=[qtile(qidxr), qtile(widxt), qblk(N_HEADS * D_CKV), per_b(kidx3), per_b(vaug),
                  pl.BlockSpec(bias.shape, lambda b, i: (0, 0, 0, 0),
                               pipeline_mode=pl.Buffered(1))],
        out_specs=qblk(N_HEADS * D_CKV),
        out_shape=jax.ShapeDtypeStruct((batch, seq, N_HEADS * D_CKV), BF16),
        scratch_shapes=[
            pltpu.VMEM((seq, TQ), I16),
            pltpu.VMEM((seq, TQ), I16),
            pltpu.VMEM((seq, TQ), I16),
            pltpu.VMEM((seq + PAD_KEYS, TQ), BF16),
            pltpu.VMEM((n_rows, 2 * D_CKV), BF16),
            pltpu.VMEM((n_rows, 2 * D_CKV), F32),
            pltpu.VMEM((n_rows, LANES), F32),
            pltpu.VMEM((2, n_rows, ATT_TK), F32),
        ],
        compiler_params=_cparams(2),
        name="dsa_attn",
    )(qidxr, widxt, r3(qlat), kidx3, vaug, bias)
    return out.reshape(batch * seq, N_HEADS * D_CKV)


def _attn_out_kernel(x_ref, olat_ref, wuv_ref, wo_ref, y_ref):
    o = jnp.dot(olat_ref[...], wuv_ref[...], preferred_element_type=F32)
    y_ref[...] = x_ref[...] + jnp.dot(o.astype(BF16), wo_ref[...], preferred_element_type=F32)


def _attn_out(x2, olat, w_uv, w_o, tm=512):
    n = x2.shape[0]
    wuv = jnp.transpose(w_uv, (1, 0, 2))
    eye = jnp.eye(N_HEADS, dtype=w_uv.dtype)
    wbd = (wuv[:, :, None, :] * eye[:, None, :, None]).reshape(N_HEADS * D_CKV, N_HEADS * D_V)
    full = lambda a: pl.BlockSpec(a.shape, lambda i: (0,) * a.ndim)
    args = (x2, olat, wbd.astype(BF16), w_o.astype(BF16))
    return pl.pallas_call(
        _attn_out_kernel,
        grid=(n // tm,),
        in_specs=[pl.BlockSpec((tm, D_MODEL), lambda i: (i, 0)),
                  pl.BlockSpec((tm, N_HEADS * D_CKV), lambda i: (i, 0)), full(args[2]), full(args[3])],
        out_specs=pl.BlockSpec((tm, D_MODEL), lambda i: (i, 0)),
        out_shape=jax.ShapeDtypeStruct((n, D_MODEL), F32),
        compiler_params=_cparams(1),
        name="attn_out",
    )(*args)


def _mlp_kernel(x_ref, g_ref, wup_ref, wdn_ref, gf_ref, y_ref, acc_ref, *, f_chunk, final_norm):
    x = x_ref[...]
    h = _rms(x, g_ref[...]).astype(BF16)
    d_ff = wup_ref.shape[1]
    for c in range(d_ff // f_chunk):
        u = jnp.dot(h, wup_ref[:, c * f_chunk:(c + 1) * f_chunk], preferred_element_type=F32)
        a = jnp.square(jnp.maximum(u, 0.0)).astype(BF16)
        d = jnp.dot(a, wdn_ref[c * f_chunk:(c + 1) * f_chunk, :], preferred_element_type=F32)
        if c == 0:
            acc_ref[...] = d
        else:
            acc_ref[...] += d
    y = x + acc_ref[...]
    if final_norm:
        y = _rms(y, gf_ref[...])
    y_ref[...] = y


def _mlp(x2, g, w_up, w_down, g_final, final_norm, tm=512, f_chunk=512):
    n = x2.shape[0]
    full = lambda a: pl.BlockSpec(a.shape, lambda i: (0,) * a.ndim)
    args = (x2, g.reshape(1, -1), w_up.astype(BF16), w_down.astype(BF16), g_final.reshape(1, -1))
    return pl.pallas_call(
        functools.partial(_mlp_kernel, f_chunk=f_chunk, final_norm=final_norm),
        grid=(n // tm,),
        in_specs=[pl.BlockSpec((tm, D_MODEL), lambda i: (i, 0))] + [full(a) for a in args[1:]],
        out_specs=pl.BlockSpec((tm, D_MODEL), lambda i: (i, 0)),
        out_shape=jax.ShapeDtypeStruct((n, D_MODEL), F32),
        scratch_shapes=[pltpu.VMEM((tm, D_MODEL), F32)],
        compiler_params=_cparams(1),
        name="mlp_final" if final_norm else "mlp",
    )(*args)


def _glu_kernel(x_ref, g_ref, w_ref, b_ref, u_ref):
    h = _rms(x_ref[...], g_ref[...]).astype(BF16)
    u = jnp.dot(h, w_ref[...], preferred_element_type=F32) + b_ref[...]
    d = u.shape[1] // 2
    u_ref[...] = u[:, :d] * jax.nn.sigmoid(u[:, d:])


def _glu(x2, g, w_pw1, b_pw1, tm=512):
    n = x2.shape[0]
    full = lambda a: pl.BlockSpec(a.shape, lambda i: (0,) * a.ndim)
    args = (x2, g.reshape(1, -1), w_pw1.astype(BF16), b_pw1.reshape(1, -1))
    return pl.pallas_call(
        _glu_kernel,
        grid=(n // tm,),
        in_specs=[pl.BlockSpec((tm, D_MODEL), lambda i: (i, 0))] + [full(a) for a in args[1:]],
        out_specs=pl.BlockSpec((tm, D_MODEL), lambda i: (i, 0)),
        out_shape=jax.ShapeDtypeStruct((n, D_MODEL), F32),
        compiler_params=_cparams(1),
        name="conv_glu",
    )(*args)


HALO = 32
CONV_ROWS = 32


def _dwconv_kernel(x_ref, u_ref, halo_ref, wdw_ref, bdw_ref, lng_ref, lnb_ref, w2_ref, b2_ref, y_ref,
                   ext_ref, cv_ref):
    i = pl.program_id(1)
    tm = u_ref.shape[0]
    ext_ref[:HALO, :] = jnp.where(i == 0, 0.0, halo_ref[...])
    ext_ref[HALO:, :] = u_ref[...]
    off = HALO - (CONV_WIDTH - 1)
    for r in range(tm // CONV_ROWS):
        r0 = r * CONV_ROWS
        acc = jnp.broadcast_to(bdw_ref[...], (CONV_ROWS, D_MODEL))
        for j in range(CONV_WIDTH):
            acc = acc + wdw_ref[j:j + 1, :] * ext_ref[r0 + off + j:r0 + off + j + CONV_ROWS, :]
        cv_ref[r0:r0 + CONV_ROWS, :] = acc
    v = cv_ref[...]
    mu = jnp.mean(v, axis=-1, keepdims=True)
    var = jnp.mean(jnp.square(v - mu), axis=-1, keepdims=True)
    v = (v - mu) * lax.rsqrt(var + EPS) * lng_ref[...] + lnb_ref[...]
    v = v * jax.nn.sigmoid(v)
    y_ref[...] = (x_ref[...] + jnp.dot(v.astype(BF16), w2_ref[...], preferred_element_type=F32)
                  + b2_ref[...])


def _dwconv(x2, u2, w_dw, b_dw, ln_g, ln_b, w_pw2, b_pw2, batch, seq, tm=512):
    x3 = x2.reshape(batch, seq, D_MODEL)
    u3 = u2.reshape(batch, seq, D_MODEL)
    hb = tm // HALO
    blk = pl.BlockSpec((None, tm, D_MODEL), lambda b, i: (b, i, 0))
    halo = pl.BlockSpec((None, HALO, D_MODEL), lambda b, i: (b, jnp.maximum(i * hb - 1, 0), 0))
    full = lambda a: pl.BlockSpec(a.shape, lambda b, i: (0,) * a.ndim)
    args = (x3, u3, u3, w_dw, b_dw.reshape(1, -1), ln_g.reshape(1, -1), ln_b.reshape(1, -1),
            w_pw2.astype(BF16), b_pw2.reshape(1, -1))
    out = pl.pallas_call(
        _dwconv_kernel,
        grid=(batch, seq // tm),
        in_specs=[blk, blk, halo] + [full(a) for a in args[3:]],
        out_specs=blk,
        out_shape=jax.ShapeDtypeStruct((batch, seq, D_MODEL), F32),
        scratch_shapes=[pltpu.VMEM((HALO + tm, D_MODEL), F32), pltpu.VMEM((tm, D_MODEL), F32)],
        compiler_params=_cparams(2),
        name="conv_dw",
    )(*args)
    return out.reshape(batch * seq, D_MODEL)


def kernel(x, norm_mix, norm_mlp, norm_final, rel_bias, attn_w_in, attn_q_norm, attn_kv_norm, attn_kidx_norm, attn_w_qidx, attn_w_uq, attn_w_uk, attn_w_uv, attn_w_o, conv_w_pw1, conv_b_pw1, conv_w_dw, conv_b_dw, conv_ln_g, conv_ln_b, conv_w_pw2, conv_b_pw2, mlp_w_up, mlp_w_down):
    batch, seq, d = x.shape
    depth = norm_mix.shape[0]
    x2 = x.reshape(batch * seq, d)
    for i in range(depth):
        j = i // 2
        if i % 2 == 0:
            qidx, widx, qlat, kidx, ckv = _proj(
                x2, norm_mix[i], attn_w_in[j], attn_q_norm[j], attn_kv_norm[j], attn_kidx_norm[j],
                attn_w_qidx[j], attn_w_uq[j], attn_w_uk[j])
            olat = _attention(qidx, widx, qlat, kidx, ckv, rel_bias, batch, seq)
            x2 = _attn_out(x2, olat, attn_w_uv[j], attn_w_o[j])
        else:
            u = _glu(x2, norm_mix[i], conv_w_pw1[j], conv_b_pw1[j])
            x2 = _dwconv(x2, u, conv_w_dw[j], conv_b_dw[j], conv_ln_g[j], conv_ln_b[j],
                         conv_w_pw2[j], conv_b_pw2[j], batch, seq)
        last = i == depth - 1
        x2 = _mlp(x2, norm_mlp[i], mlp_w_up[i], mlp_w_down[i], norm_final, final_norm=last)
    if depth == 0:
        raise ValueError("depth must be positive")
    return x2.reshape(batch, seq, d)
```

```python
import functools
import math

import jax
import jax.numpy as jnp
from jax import lax
from jax.experimental import pallas as pl
from jax.experimental.pallas import tpu as pltpu

F32 = jnp.float32
BF16 = jnp.bfloat16
I32 = jnp.int32

D_MODEL = 1024
N_HEADS = 16
D_NOPE = 64
D_V = 64
D_CQ = 256
D_CKV = 128
IDX_HEADS = 8
IDX_DIM = 64
TOPK_MAX = 256
CONV_WIDTH = 31
NUM_BUCKETS = 32
MAX_EXACT = 16
MAX_DISTANCE = 128
EPS = 1e-6

LANES = 128
SUBLANES = 8
TQ = 128
SCORE_TILE = 512
ATT_TK = 512
BIAS_TK = 2 * TQ
PAD_KEYS = ATT_TK - TQ
NEG_MASK = -1e30
LOG2E = math.log2(math.e)
FIXED_BITS = 16
BITS_PER_TEST = 4
PREFIX_ROWS = 256
VMEM_LIMIT = 56 * 1024 * 1024


def _cparams(n_axes):
    return pltpu.CompilerParams(dimension_semantics=("arbitrary",) * n_axes,
                                vmem_limit_bytes=VMEM_LIMIT)


def _rms(x, g):
    return x * lax.rsqrt(jnp.mean(x * x, axis=-1, keepdims=True) + EPS) * g


def _proj_kernel(x_ref, g_ref, win_ref, qn_ref, kvn_ref, kin_ref, wqidx_ref, wuq_ref, wukt_ref,
                 qidx_ref, widx_ref, qlat_ref, kidx_ref, ckv_ref):
    h = _rms(x_ref[...], g_ref[...])
    proj = jnp.dot(h.astype(BF16), win_ref[...], preferred_element_type=F32)
    o1, o2, o3 = D_CQ, D_CQ + D_CKV, D_CQ + D_CKV + IDX_DIM
    cq = _rms(proj[:, :o1], qn_ref[...])
    ckv = _rms(proj[:, o1:o2], kvn_ref[...])
    kid = _rms(proj[:, o2:o3], kin_ref[...])
    widx_ref[...] = proj[:, o3:o3 + IDX_HEADS] * (IDX_HEADS ** -0.5)
    ckv_ref[...] = ckv.astype(BF16)
    kidx_ref[...] = kid.astype(BF16)
    cqb = cq.astype(BF16)
    qidx = jnp.dot(cqb, wqidx_ref[...], preferred_element_type=F32) * (IDX_DIM ** -0.5)
    qidx_ref[...] = qidx.astype(BF16)
    qh = jnp.dot(cqb, wuq_ref[...], preferred_element_type=F32).astype(BF16)
    for hh in range(N_HEADS):
        ql = jnp.dot(qh[:, hh * D_NOPE:(hh + 1) * D_NOPE], wukt_ref[hh],
                     preferred_element_type=F32) * (D_NOPE ** -0.5 * LOG2E)
        qlat_ref[:, hh * D_CKV:(hh + 1) * D_CKV] = ql.astype(BF16)


def _proj(x2, g, w_in, qn, kvn, kin, w_qidx, w_uq, w_uk, tm=512):
    n = x2.shape[0]
    ncol = w_in.shape[1]
    npad = -ncol % LANES
    win = jnp.pad(w_in, ((0, 0), (0, npad))).astype(BF16)
    wukt = jnp.transpose(w_uk, (1, 2, 0)).astype(BF16)
    full = lambda a: pl.BlockSpec(a.shape, lambda i: (0,) * a.ndim)
    row = lambda c: pl.BlockSpec((tm, c), lambda i: (i, 0))
    args = (x2, g.reshape(1, -1), win, qn.reshape(1, -1), kvn.reshape(1, -1), kin.reshape(1, -1),
            w_qidx.astype(BF16), w_uq.astype(BF16), wukt)
    return pl.pallas_call(
        _proj_kernel,
        grid=(n // tm,),
        in_specs=[row(D_MODEL)] + [full(a) for a in args[1:]],
        out_specs=[row(IDX_HEADS * IDX_DIM), row(IDX_HEADS), row(N_HEADS * D_CKV), row(IDX_DIM),
                   row(D_CKV)],
        out_shape=[jax.ShapeDtypeStruct((n, IDX_HEADS * IDX_DIM), BF16),
                   jax.ShapeDtypeStruct((n, IDX_HEADS), F32),
                   jax.ShapeDtypeStruct((n, N_HEADS * D_CKV), BF16),
                   jax.ShapeDtypeStruct((n, IDX_DIM), BF16),
                   jax.ShapeDtypeStruct((n, D_CKV), BF16)],
        compiler_params=_cparams(1),
        name="dsa_proj",
    )(*args)


def _attn_kernel(qidxr_ref, widxt_ref, qlat_ref, kidx_ref, vaug_ref, bias_ref, tri_ref, olat_ref,
                 keys_ref, maskt_ref, a_ref, acc_ref, m_ref, s_ref, *, k_sel):
    i = pl.program_id(1)
    q0 = i * TQ
    n_rows = N_HEADS * TQ
    qpos = q0 + lax.broadcasted_iota(I32, (1, TQ), 1)
    krow = lax.broadcasted_iota(I32, (SCORE_TILE, TQ), 0)
    n_tiles = i // (SCORE_TILE // TQ) + 1

    def tile_rows(tt):
        return pl.ds(pl.multiple_of(tt * SCORE_TILE, SCORE_TILE), SCORE_TILE)

    def score_body(tt, carry):
        qk = jnp.dot(kidx_ref[tile_rows(tt), :], qidxr_ref[...],
                     preferred_element_type=F32)
        sc = jnp.zeros((SCORE_TILE, TQ), F32)
        for h in range(IDX_HEADS):
            sc = sc + widxt_ref[h:h + 1, :] * jnp.maximum(qk[:, h * TQ:(h + 1) * TQ], 0.0)
        sc = jnp.where(tt * SCORE_TILE + krow <= qpos, sc, -jnp.inf)
        bits = pltpu.bitcast(sc, I32)
        keys_ref[tile_rows(tt), :] = bits ^ ((bits >> 31) & 0x7FFFFFFF)
        return carry

    lax.fori_loop(0, n_tiles, score_body, 0)

    def tile_reduce(x, op):
        return op(x.reshape(SCORE_TILE // SUBLANES, SUBLANES, TQ), axis=0)

    def count_ge(thr):
        def body(tt, acc):
            return acc + tile_reduce(jnp.where(keys_ref[tile_rows(tt), :] >= thr, 1, 0), jnp.sum)
        acc = lax.fori_loop(0, n_tiles, body, jnp.zeros((SUBLANES, TQ), I32))
        return jnp.sum(acc, axis=0, keepdims=True)

    trivial = qpos < k_sel

    def bit_step(step, st):
        cand, cntc, cnt_ub = st
        trial = cand ^ (jnp.int32(1) << (31 - step))
        cnt = count_ge(trial)
        ok = cnt >= k_sel
        return jnp.where(ok, trial, cand), jnp.where(ok, cnt, cntc), jnp.where(ok, cnt_ub, cnt)

    def resolve(step, st, tval, res):
        cand, cntc, _ = st
        top = cand | ((jnp.int32(1) << (32 - step)) - 1)

        def body(tt, c):
            k = keys_ref[tile_rows(tt), :]
            inb = jnp.logical_and(k >= cand, k <= top)
            return (jnp.minimum(c[0], tile_reduce(jnp.where(inb, k, 2 ** 31 - 1), jnp.min)),
                    jnp.maximum(c[1], tile_reduce(jnp.where(inb, k, -2 ** 31), jnp.max)))

        mn, mx = lax.fori_loop(0, n_tiles, body, (jnp.full((SUBLANES, TQ), 2 ** 31 - 1, I32),
                                                  jnp.full((SUBLANES, TQ), -2 ** 31, I32)))
        mn = jnp.min(mn, axis=0, keepdims=True)
        mx = jnp.max(mx, axis=0, keepdims=True)
        single = mn == mx
        tval = jnp.where(single, mn, tval)
        res = jnp.where(single, 1, res)
        settled = jnp.logical_or(jnp.logical_or(res > 0, cntc == k_sel), trivial)
        return tval, res, jnp.sum(jnp.where(settled, 0, 1))

    st = (jnp.full((1, TQ), -2 ** 31, I32), jnp.full((1, TQ), 2 ** 30, I32),
          jnp.zeros((1, TQ), I32))
    st = lax.fori_loop(0, FIXED_BITS, bit_step, st)
    tval, res, n_open = resolve(FIXED_BITS, st, st[0], jnp.zeros((1, TQ), I32))

    def search_body(c):
        step, st, tval, res, _ = c
        for b in range(BITS_PER_TEST):
            st = bit_step(step + b, st)
        tval, res, n_open = resolve(step + BITS_PER_TEST, st, tval, res)
        return step + BITS_PER_TEST, st, tval, res, n_open

    _, st, tval, res, _ = lax.while_loop(
        lambda c: jnp.logical_and(c[0] < 32, c[4] > 0), search_body,
        (jnp.int32(FIXED_BITS), st, tval, res, n_open))
    cand, cntc, cnt_ub = st
    exact = jnp.logical_or(cntc == k_sel, trivial)
    thr = jnp.where(exact, cand, jnp.where(res > 0, tval, cand))
    thr = jnp.where(trivial, -2 ** 31, thr)
    tie_take = jnp.where(exact, 2.0 ** 30, (k_sel - cnt_ub).astype(F32))

    def mask_body(tt, seen):
        k = keys_ref[tile_rows(tt), :]
        idx = tt * SCORE_TILE + krow
        tie = k == thr
        tie_bf = jnp.where(tie, 1.0, 0.0).astype(BF16)
        ranks = []
        for c in range(SCORE_TILE // PREFIX_ROWS):
            rank = jnp.dot(tri_ref[...], tie_bf[c * PREFIX_ROWS:(c + 1) * PREFIX_ROWS],
                           preferred_element_type=F32) + seen
            seen = rank[PREFIX_ROWS - 1:PREFIX_ROWS, :]
            ranks.append(rank)
        rank = jnp.concatenate(ranks, axis=0)
        sel = jnp.logical_or(k > thr, jnp.logical_and(tie, rank <= tie_take))
        sel = jnp.logical_and(sel, idx <= qpos)
        rows = pl.ds(pl.multiple_of(PAD_KEYS + tt * SCORE_TILE, TQ), SCORE_TILE)
        maskt_ref[rows, :] = jnp.where(sel, 0.0, NEG_MASK).astype(BF16)
        return seen

    lax.fori_loop(0, n_tiles, mask_body, jnp.zeros((1, TQ), F32))
    maskt_ref[:PAD_KEYS, :] = jnp.full((PAD_KEYS, TQ), NEG_MASK, BF16)

    eye = (lax.broadcasted_iota(I32, (TQ, TQ), 0) == lax.broadcasted_iota(I32, (TQ, TQ), 1))
    eye = jnp.where(eye, 1.0, 0.0).astype(BF16)
    for h in range(N_HEADS):
        a_ref[h * TQ:(h + 1) * TQ, :D_CKV] = qlat_ref[:, h * D_CKV:(h + 1) * D_CKV]
        a_ref[h * TQ:(h + 1) * TQ, D_CKV:] = eye
    m_ref[...] = jnp.full((n_rows, LANES), -jnp.inf, F32)
    acc_ref[...] = jnp.zeros((n_rows, 2 * D_CKV), F32)
    n_chunks = ATT_TK // LANES

    def key_rows(j):
        r0 = jnp.maximum(q0 + TQ + PAD_KEYS - (j + 1) * ATT_TK, 0)
        return pl.ds(pl.multiple_of(r0, TQ), ATT_TK)

    def logits(j, slot):
        rows = key_rows(j)
        rhs = jnp.concatenate([vaug_ref[rows, :D_CKV], maskt_ref[rows, :]], axis=1)
        s_ref[slot] = lax.dot_general(a_ref[...], rhs, (((1,), (1,)), ((), ())),
                                      preferred_element_type=F32)

    def consume(j, slot):
        vk = vaug_ref[key_rows(j), :]
        near = jnp.minimum(j, 1)
        ps = []
        for h in range(N_HEADS):
            rows = slice(h * TQ, (h + 1) * TQ)
            sh = s_ref[slot, rows, :]
            sh = jnp.concatenate([sh[:, :ATT_TK - BIAS_TK],
                                  sh[:, ATT_TK - BIAS_TK:] + bias_ref[near, h]], axis=1)
            m_old = m_ref[rows, :]
            m_new = jnp.maximum(m_old, jnp.max(sh, axis=1, keepdims=True))
            alpha = jnp.exp2(m_old - m_new)
            m_ref[rows, :] = m_new
            p = jnp.exp2(sh - jnp.concatenate([m_new] * n_chunks, axis=1))
            ps.append(p.astype(BF16))
            acc_ref[rows, :] = acc_ref[rows, :] * jnp.concatenate([alpha, alpha], axis=1)
        acc_ref[...] += jnp.dot(jnp.concatenate(ps, axis=0), vk, preferred_element_type=F32)

    n_steps = (q0 + TQ + ATT_TK - 1) // ATT_TK
    logits(0, 0)

    def att_pair(jj, carry):
        j = 2 * jj
        logits(j + 1, 1)
        consume(j, 0)
        logits(j + 2, 0)
        consume(j + 1, 1)
        return carry

    lax.fori_loop(0, n_steps // 2, att_pair, 0)

    @pl.when(n_steps % 2 == 1)
    def _():
        consume(n_steps - 1, 0)

    for h in range(N_HEADS):
        acc = acc_ref[h * TQ:(h + 1) * TQ, :]
        olat_ref[:, h * D_CKV:(h + 1) * D_CKV] = (acc[:, :D_CKV] / acc[:, D_CKV:]).astype(BF16)


def _t5_bucket(dist):
    n = jnp.maximum(dist, 0)
    nf = jnp.maximum(n, 1).astype(F32)
    large = MAX_EXACT + (jnp.log(nf / MAX_EXACT) / math.log(MAX_DISTANCE / MAX_EXACT)
                         * (NUM_BUCKETS - MAX_EXACT)).astype(I32)
    large = jnp.minimum(large, NUM_BUCKETS - 1)
    return jnp.where(n < MAX_EXACT, n, large)


def _near_bias(rel_bias):
    assert BIAS_TK >= TQ + MAX_DISTANCE - 1
    period = TQ + BIAS_TK
    e = jnp.arange(period)
    e = jnp.where(e < BIAS_TK, e, e - period)
    tab = rel_bias[_t5_bucket(BIAS_TK - TQ - e)] - rel_bias[NUM_BUCKETS - 1]
    flat = jnp.tile(tab, (TQ, 1))[:TQ * (period - 1)]
    near = flat.reshape(TQ, period - 1, N_HEADS)[:, :BIAS_TK]
    return (jnp.transpose(near, (2, 0, 1)) * LOG2E).astype(F32)


def _attention(qidx, widx, qlat, kidx, ckv, rel_bias, batch, seq):
    assert seq % SCORE_TILE == 0 and SCORE_TILE % TQ == 0 and ATT_TK % TQ == 0
    k_sel = min(TOPK_MAX, seq // 4)
    nq = seq // TQ
    r3 = lambda a: a.reshape(batch, seq, a.shape[-1])
    qidxr = jnp.transpose(qidx.reshape(batch, nq, TQ, IDX_HEADS, IDX_DIM), (0, 1, 4, 3, 2))
    qidxr = qidxr.reshape(batch, nq, IDX_DIM, IDX_HEADS * TQ)
    widxt = jnp.transpose(widx.reshape(batch, nq, TQ, IDX_HEADS), (0, 1, 3, 2))
    ckv3 = jnp.pad(r3(ckv), ((0, 0), (PAD_KEYS, 0), (0, 0)))
    vaug = jnp.concatenate([ckv3, jnp.ones_like(ckv3)], axis=-1)
    bias = _near_bias(rel_bias)
    bias = jnp.stack([bias, jnp.zeros_like(bias)])
    tri = jnp.tril(jnp.ones((PREFIX_ROWS, PREFIX_ROWS), BF16))

    qblk = lambda c: pl.BlockSpec((None, TQ, c), lambda b, i: (b, i, 0))
    qtile = lambda a: pl.BlockSpec((None, None) + a.shape[2:], lambda b, i: (b, i, 0, 0))
    per_b = lambda a: pl.BlockSpec((None,) + a.shape[1:], lambda b, i: (b,) + (0,) * (a.ndim - 1))
    n_rows = N_HEADS * TQ
    kidx3 = r3(kidx)
    out = pl.pallas_call(
        functools.partial(_attn_kernel, k_sel=k_sel),
        grid=(batch, nq),
        in_specs=[qtile(qidxr), qtile(widxt), qblk(N_HEADS * D_CKV), per_b(kidx3), per_b(vaug),
                  pl.BlockSpec(bias.shape, lambda b, i: (0, 0, 0, 0),
                               pipeline_mode=pl.Buffered(1)),
                  pl.BlockSpec(tri.shape, lambda b, i: (0, 0))],
        out_specs=qblk(N_HEADS * D_CKV),
        out_shape=jax.ShapeDtypeStruct((batch, seq, N_HEADS * D_CKV), BF16),
        scratch_shapes=[
            pltpu.VMEM((seq, TQ), I32),
            pltpu.VMEM((seq + PAD_KEYS, TQ), BF16),
            pltpu.VMEM((n_rows, 2 * D_CKV), BF16),
            pltpu.VMEM((n_rows, 2 * D_CKV), F32),
            pltpu.VMEM((n_rows, LANES), F32),
            pltpu.VMEM((2, n_rows, ATT_TK), F32),
        ],
        compiler_params=_cparams(2),
        name="dsa_attn",
    )(qidxr, widxt, r3(qlat), kidx3, vaug, bias, tri)
    return out.reshape(batch * seq, N_HEADS * D_CKV)


def _attn_out_kernel(x_ref, olat_ref, wuv_ref, wo_ref, y_ref):
    o = jnp.dot(olat_ref[...], wuv_ref[...], preferred_element_type=F32)
    y_ref[...] = x_ref[...] + jnp.dot(o.astype(BF16), wo_ref[...], preferred_element_type=F32)


def _attn_out(x2, olat, w_uv, w_o, tm=512):
    n = x2.shape[0]
    wuv = jnp.transpose(w_uv, (1, 0, 2))
    eye = jnp.eye(N_HEADS, dtype=w_uv.dtype)
    wbd = (wuv[:, :, None, :] * eye[:, None, :, None]).reshape(N_HEADS * D_CKV, N_HEADS * D_V)
    full = lambda a: pl.BlockSpec(a.shape, lambda i: (0,) * a.ndim)
    args = (x2, olat, wbd.astype(BF16), w_o.astype(BF16))
    return pl.pallas_call(
        _attn_out_kernel,
        grid=(n // tm,),
        in_specs=[pl.BlockSpec((tm, D_MODEL), lambda i: (i, 0)),
                  pl.BlockSpec((tm, N_HEADS * D_CKV), lambda i: (i, 0)), full(args[2]), full(args[3])],
        out_specs=pl.BlockSpec((tm, D_MODEL), lambda i: (i, 0)),
        out_shape=jax.ShapeDtypeStruct((n, D_MODEL), F32),
        compiler_params=_cparams(1),
        name="attn_out",
    )(*args)


def _mlp_kernel(x_ref, g_ref, wup_ref, wdn_ref, gf_ref, y_ref, acc_ref, *, f_chunk, final_norm):
    x = x_ref[...]
    h = _rms(x, g_ref[...]).astype(BF16)
    d_ff = wup_ref.shape[1]
    for c in range(d_ff // f_chunk):
        u = jnp.dot(h, wup_ref[:, c * f_chunk:(c + 1) * f_chunk], preferred_element_type=F32)
        a = jnp.square(jnp.maximum(u, 0.0)).astype(BF16)
        d = jnp.dot(a, wdn_ref[c * f_chunk:(c + 1) * f_chunk, :], preferred_element_type=F32)
        if c == 0:
            acc_ref[...] = d
        else:
            acc_ref[...] += d
    y = x + acc_ref[...]
    if final_norm:
        y = _rms(y, gf_ref[...])
    y_ref[...] = y


def _mlp(x2, g, w_up, w_down, g_final, final_norm, tm=512, f_chunk=512):
    n = x2.shape[0]
    full = lambda a: pl.BlockSpec(a.shape, lambda i: (0,) * a.ndim)
    args = (x2, g.reshape(1, -1), w_up.astype(BF16), w_down.astype(BF16), g_final.reshape(1, -1))
    return pl.pallas_call(
        functools.partial(_mlp_kernel, f_chunk=f_chunk, final_norm=final_norm),
        grid=(n // tm,),
        in_specs=[pl.BlockSpec((tm, D_MODEL), lambda i: (i, 0))] + [full(a) for a in args[1:]],
        out_specs=pl.BlockSpec((tm, D_MODEL), lambda i: (i, 0)),
        out_shape=jax.ShapeDtypeStruct((n, D_MODEL), F32),
        scratch_shapes=[pltpu.VMEM((tm, D_MODEL), F32)],
        compiler_params=_cparams(1),
        name="mlp_final" if final_norm else "mlp",
    )(*args)


def _glu_kernel(x_ref, g_ref, w_ref, b_ref, u_ref):
    h = _rms(x_ref[...], g_ref[...]).astype(BF16)
    u = jnp.dot(h, w_ref[...], preferred_element_type=F32) + b_ref[...]
    d = u.shape[1] // 2
    u_ref[...] = u[:, :d] * jax.nn.sigmoid(u[:, d:])


def _glu(x2, g, w_pw1, b_pw1, tm=512):
    n = x2.shape[0]
    full = lambda a: pl.BlockSpec(a.shape, lambda i: (0,) * a.ndim)
    args = (x2, g.reshape(1, -1), w_pw1.astype(BF16), b_pw1.reshape(1, -1))
    return pl.pallas_call(
        _glu_kernel,
        grid=(n // tm,),
        in_specs=[pl.BlockSpec((tm, D_MODEL), lambda i: (i, 0))] + [full(a) for a in args[1:]],
        out_specs=pl.BlockSpec((tm, D_MODEL), lambda i: (i, 0)),
        out_shape=jax.ShapeDtypeStruct((n, D_MODEL), F32),
        compiler_params=_cparams(1),
        name="conv_glu",
    )(*args)


HALO = 32
CONV_ROWS = 64
SUBLANE_PAD = 8


def _dwconv_kernel(x_ref, u_ref, halo_ref, wdw_ref, bdw_ref, lng_ref, lnb_ref, w2_ref, b2_ref, y_ref,
                   ext_ref, cv_ref):
    i = pl.program_id(1)
    tm = u_ref.shape[0]
    ext_ref[:HALO, :] = jnp.where(i == 0, 0.0, halo_ref[...])
    ext_ref[HALO:HALO + tm, :] = u_ref[...]
    ext_ref[HALO + tm:, :] = jnp.zeros((SUBLANE_PAD, D_MODEL), F32)
    off = HALO - (CONV_WIDTH - 1)

    def rows_body(r, carry):
        r0 = pl.multiple_of(r * CONV_ROWS, CONV_ROWS)
        for lc in range(D_MODEL // LANES):
            lanes = slice(lc * LANES, (lc + 1) * LANES)
            acc = jnp.broadcast_to(bdw_ref[:, lanes], (CONV_ROWS, LANES))
            for s in range(SUBLANE_PAD):
                part = None
                for j in range(CONV_WIDTH):
                    if (off + j) % SUBLANE_PAD != s:
                        continue
                    rows = pl.ds(pl.multiple_of(r0 + (off + j - s), SUBLANE_PAD),
                                 CONV_ROWS + SUBLANE_PAD)
                    term = wdw_ref[j:j + 1, lanes] * ext_ref[rows, lanes]
                    part = term if part is None else part + term
                if part is not None:
                    acc = acc + part[s:s + CONV_ROWS]
            cv_ref[pl.ds(r0, CONV_ROWS), lanes] = acc
        return carry

    lax.fori_loop(0, tm // CONV_ROWS, rows_body, 0)
    v = cv_ref[...]
    mu = jnp.mean(v, axis=-1, keepdims=True)
    var = jnp.mean(jnp.square(v - mu), axis=-1, keepdims=True)
    v = (v - mu) * lax.rsqrt(var + EPS) * lng_ref[...] + lnb_ref[...]
    v = v * jax.nn.sigmoid(v)
    y_ref[...] = (x_ref[...] + jnp.dot(v.astype(BF16), w2_ref[...], preferred_element_type=F32)
                  + b2_ref[...])


def _dwconv(x2, u2, w_dw, b_dw, ln_g, ln_b, w_pw2, b_pw2, batch, seq, tm=512):
    x3 = x2.reshape(batch, seq, D_MODEL)
    u3 = u2.reshape(batch, seq, D_MODEL)
    hb = tm // HALO
    blk = pl.BlockSpec((None, tm, D_MODEL), lambda b, i: (b, i, 0))
    halo = pl.BlockSpec((None, HALO, D_MODEL), lambda b, i: (b, jnp.maximum(i * hb - 1, 0), 0))
    full = lambda a: pl.BlockSpec(a.shape, lambda b, i: (0,) * a.ndim)
    args = (x3, u3, u3, w_dw, b_dw.reshape(1, -1), ln_g.reshape(1, -1), ln_b.reshape(1, -1),
            w_pw2.astype(BF16), b_pw2.reshape(1, -1))
    out = pl.pallas_call(
        _dwconv_kernel,
        grid=(batch, seq // tm),
        in_specs=[blk, blk, halo] + [full(a) for a in args[3:]],
        out_specs=blk,
        out_shape=jax.ShapeDtypeStruct((batch, seq, D_MODEL), F32),
        scratch_shapes=[pltpu.VMEM((HALO + tm + SUBLANE_PAD, D_MODEL), F32),
                        pltpu.VMEM((tm, D_MODEL), F32)],
        compiler_params=_cparams(2),
        name="conv_dw",
    )(*args)
    return out.reshape(batch * seq, D_MODEL)


def kernel(x, norm_mix, norm_mlp, norm_final, rel_bias, attn_w_in, attn_q_norm, attn_kv_norm, attn_kidx_norm, attn_w_qidx, attn_w_uq, attn_w_uk, attn_w_uv, attn_w_o, conv_w_pw1, conv_b_pw1, conv_w_dw, conv_b_dw, conv_ln_g, conv_ln_b, conv_w_pw2, conv_b_pw2, mlp_w_up, mlp_w_down):
    batch, seq, d = x.shape
    depth = norm_mix.shape[0]
    x2 = x.reshape(batch * seq, d)
    for i in range(depth):
        j = i // 2
        if i % 2 == 0:
            qidx, widx, qlat, kidx, ckv = _proj(
                x2, norm_mix[i], attn_w_in[j], attn_q_norm[j], attn_kv_norm[j], attn_kidx_norm[j],
                attn_w_qidx[j], attn_w_uq[j], attn_w_uk[j])
            olat = _attention(qidx, widx, qlat, kidx, ckv, rel_bias, batch, seq)
            x2 = _attn_out(x2, olat, attn_w_uv[j], attn_w_o[j])
        else:
            u = _glu(x2, norm_mix[i], conv_w_pw1[j], conv_b_pw1[j])
            x2 = _dwconv(x2, u, conv_w_dw[j], conv_b_dw[j], conv_ln_g[j], conv_ln_b[j],
                         conv_w_pw2[j], conv_b_pw2[j], batch, seq)
        last = i == depth - 1
        x2 = _mlp(x2, norm_mlp[i], mlp_w_up[i], mlp_w_down[i], norm_final, final_norm=last)
    if depth == 0:
        raise ValueError("depth must be positive")
    return x2.reshape(batch, seq, d)
```

```python
import functools
import math

import jax
import jax.numpy as jnp
from jax import lax
from jax.experimental import pallas as pl
from jax.experimental.pallas import tpu as pltpu

F32 = jnp.float32
BF16 = jnp.bfloat16
I32 = jnp.int32

D_MODEL = 1024
N_HEADS = 16
D_NOPE = 64
D_V = 64
D_CQ = 256
D_CKV = 128
IDX_HEADS = 8
IDX_DIM = 64
TOPK_MAX = 256
CONV_WIDTH = 31
NUM_BUCKETS = 32
MAX_EXACT = 16
MAX_DISTANCE = 128
EPS = 1e-6

LANES = 128
SUBLANES = 8
TQ = 128
SCORE_TILE = 512
ATT_TK = 512
BIAS_TK = 2 * TQ
PAD_KEYS = ATT_TK - TQ
NEG_MASK = -1e30
LOG2E = math.log2(math.e)
FIXED_BITS = 20
BITS_PER_TEST = 2
PREFIX_ROWS = 256
VMEM_LIMIT = 56 * 1024 * 1024


def _cparams(n_axes):
    return pltpu.CompilerParams(dimension_semantics=("arbitrary",) * n_axes,
                                vmem_limit_bytes=VMEM_LIMIT)


def _rms(x, g):
    return x * lax.rsqrt(jnp.mean(x * x, axis=-1, keepdims=True) + EPS) * g


def _proj_kernel(x_ref, g_ref, win_ref, qn_ref, kvn_ref, kin_ref, wqidx_ref, wuq_ref, wukt_ref,
                 qidx_ref, widx_ref, qlat_ref, kidx_ref, ckv_ref):
    h = _rms(x_ref[...], g_ref[...])
    proj = jnp.dot(h.astype(BF16), win_ref[...], preferred_element_type=F32)
    o1, o2, o3 = D_CQ, D_CQ + D_CKV, D_CQ + D_CKV + IDX_DIM
    cq = _rms(proj[:, :o1], qn_ref[...])
    ckv = _rms(proj[:, o1:o2], kvn_ref[...])
    kid = _rms(proj[:, o2:o3], kin_ref[...])
    widx_ref[...] = proj[:, o3:o3 + IDX_HEADS] * (IDX_HEADS ** -0.5)
    ckv_ref[...] = ckv.astype(BF16)
    kidx_ref[...] = kid.astype(BF16)
    cqb = cq.astype(BF16)
    qidx = jnp.dot(cqb, wqidx_ref[...], preferred_element_type=F32) * (IDX_DIM ** -0.5)
    qidx_ref[...] = qidx.astype(BF16)
    qh = jnp.dot(cqb, wuq_ref[...], preferred_element_type=F32).astype(BF16)
    for hh in range(N_HEADS):
        ql = jnp.dot(qh[:, hh * D_NOPE:(hh + 1) * D_NOPE], wukt_ref[hh],
                     preferred_element_type=F32) * (D_NOPE ** -0.5 * LOG2E)
        qlat_ref[:, hh * D_CKV:(hh + 1) * D_CKV] = ql.astype(BF16)


def _proj(x2, g, w_in, qn, kvn, kin, w_qidx, w_uq, w_uk, tm=512):
    n = x2.shape[0]
    ncol = w_in.shape[1]
    npad = -ncol % LANES
    win = jnp.pad(w_in, ((0, 0), (0, npad))).astype(BF16)
    wukt = jnp.transpose(w_uk, (1, 2, 0)).astype(BF16)
    full = lambda a: pl.BlockSpec(a.shape, lambda i: (0,) * a.ndim)
    row = lambda c: pl.BlockSpec((tm, c), lambda i: (i, 0))
    args = (x2, g.reshape(1, -1), win, qn.reshape(1, -1), kvn.reshape(1, -1), kin.reshape(1, -1),
            w_qidx.astype(BF16), w_uq.astype(BF16), wukt)
    return pl.pallas_call(
        _proj_kernel,
        grid=(n // tm,),
        in_specs=[row(D_MODEL)] + [full(a) for a in args[1:]],
        out_specs=[row(IDX_HEADS * IDX_DIM), row(IDX_HEADS), row(N_HEADS * D_CKV), row(IDX_DIM),
                   row(D_CKV)],
        out_shape=[jax.ShapeDtypeStruct((n, IDX_HEADS * IDX_DIM), BF16),
                   jax.ShapeDtypeStruct((n, IDX_HEADS), F32),
                   jax.ShapeDtypeStruct((n, N_HEADS * D_CKV), BF16),
                   jax.ShapeDtypeStruct((n, IDX_DIM), BF16),
                   jax.ShapeDtypeStruct((n, D_CKV), BF16)],
        compiler_params=_cparams(1),
        name="dsa_proj",
    )(*args)


def _attn_kernel(qidxr_ref, widxt_ref, qlat_ref, kidx_ref, vaug_ref, bias_ref, tri_ref, olat_ref,
                 keys_ref, maskt_ref, a_ref, acc_ref, m_ref, s_ref, *, k_sel):
    i = pl.program_id(1)
    q0 = i * TQ
    n_rows = N_HEADS * TQ
    qpos = q0 + lax.broadcasted_iota(I32, (1, TQ), 1)
    krow = lax.broadcasted_iota(I32, (SCORE_TILE, TQ), 0)
    n_tiles = i // (SCORE_TILE // TQ) + 1

    def tile_rows(tt):
        return pl.ds(pl.multiple_of(tt * SCORE_TILE, SCORE_TILE), SCORE_TILE)

    def score_body(tt, carry):
        qk = jnp.dot(kidx_ref[tile_rows(tt), :], qidxr_ref[...],
                     preferred_element_type=F32)
        sc = jnp.zeros((SCORE_TILE, TQ), F32)
        for h in range(IDX_HEADS):
            sc = sc + widxt_ref[h:h + 1, :] * jnp.maximum(qk[:, h * TQ:(h + 1) * TQ], 0.0)
        sc = jnp.where(tt * SCORE_TILE + krow <= qpos, sc, -jnp.inf)
        bits = pltpu.bitcast(sc, I32)
        keys_ref[tile_rows(tt), :] = bits ^ ((bits >> 31) & 0x7FFFFFFF)
        return carry

    lax.fori_loop(0, n_tiles, score_body, 0)

    def tile_reduce(x, op):
        return op(x.reshape(SCORE_TILE // SUBLANES, SUBLANES, TQ), axis=0)

    def count_ge(thr):
        def body(tt, acc):
            return acc + tile_reduce(jnp.where(keys_ref[tile_rows(tt), :] >= thr, 1, 0), jnp.sum)
        acc = lax.fori_loop(0, n_tiles, body, jnp.zeros((SUBLANES, TQ), I32))
        return jnp.sum(acc, axis=0, keepdims=True)

    trivial = qpos < k_sel

    def bit_step(step, st):
        cand, cntc, cnt_ub = st
        trial = cand ^ (jnp.int32(1) << (31 - step))
        cnt = count_ge(trial)
        ok = cnt >= k_sel
        return jnp.where(ok, trial, cand), jnp.where(ok, cnt, cntc), jnp.where(ok, cnt_ub, cnt)

    def resolve(step, st, tval, res):
        cand, cntc, cnt_ub = st
        top = cand | ((jnp.int32(1) << (32 - step)) - 1)

        def body(tt, c):
            k = keys_ref[tile_rows(tt), :]
            inb = jnp.logical_and(k >= cand, k <= top)
            return (jnp.minimum(c[0], tile_reduce(jnp.where(inb, k, 2 ** 31 - 1), jnp.min)),
                    jnp.maximum(c[1], tile_reduce(jnp.where(inb, k, -2 ** 31), jnp.max)))

        mn, mx = lax.fori_loop(0, n_tiles, body, (jnp.full((SUBLANES, TQ), 2 ** 31 - 1, I32),
                                                  jnp.full((SUBLANES, TQ), -2 ** 31, I32)))
        mn = jnp.min(mn, axis=0, keepdims=True)
        mx = jnp.max(mx, axis=0, keepdims=True)
        known = jnp.logical_or(mn == mx, cnt_ub == k_sel - 1)
        tval = jnp.where(known, mx, tval)
        res = jnp.where(known, 1, res)
        settled = jnp.logical_or(jnp.logical_or(res > 0, cntc == k_sel), trivial)
        return tval, res, jnp.sum(jnp.where(settled, 0, 1))

    st = (jnp.full((1, TQ), -2 ** 31, I32), jnp.full((1, TQ), 2 ** 30, I32),
          jnp.zeros((1, TQ), I32))
    st = lax.fori_loop(0, FIXED_BITS, bit_step, st)
    tval, res, n_open = resolve(FIXED_BITS, st, st[0], jnp.zeros((1, TQ), I32))

    def search_body(c):
        step, st, tval, res, _ = c
        for b in range(BITS_PER_TEST):
            st = bit_step(step + b, st)
        tval, res, n_open = resolve(step + BITS_PER_TEST, st, tval, res)
        return step + BITS_PER_TEST, st, tval, res, n_open

    _, st, tval, res, _ = lax.while_loop(
        lambda c: jnp.logical_and(c[0] < 32, c[4] > 0), search_body,
        (jnp.int32(FIXED_BITS), st, tval, res, n_open))
    cand, cntc, cnt_ub = st
    exact = jnp.logical_or(cntc == k_sel, trivial)
    thr = jnp.where(exact, cand, jnp.where(res > 0, tval, cand))
    thr = jnp.where(trivial, -2 ** 31, thr)
    tie_take = jnp.where(exact, 2.0 ** 30, (k_sel - cnt_ub).astype(F32))

    def mask_body(tt, seen):
        k = keys_ref[tile_rows(tt), :]
        idx = tt * SCORE_TILE + krow
        tie = k == thr
        tie_bf = jnp.where(tie, 1.0, 0.0).astype(BF16)
        ranks = []
        for c in range(SCORE_TILE // PREFIX_ROWS):
            rank = jnp.dot(tri_ref[...], tie_bf[c * PREFIX_ROWS:(c + 1) * PREFIX_ROWS],
                           preferred_element_type=F32) + seen
            seen = rank[PREFIX_ROWS - 1:PREFIX_ROWS, :]
            ranks.append(rank)
        rank = jnp.concatenate(ranks, axis=0)
        sel = jnp.logical_or(k > thr, jnp.logical_and(tie, rank <= tie_take))
        sel = jnp.logical_and(sel, idx <= qpos)
        rows = pl.ds(pl.multiple_of(PAD_KEYS + tt * SCORE_TILE, TQ), SCORE_TILE)
        maskt_ref[rows, :] = jnp.where(sel, 0.0, NEG_MASK).astype(BF16)
        return seen

    lax.fori_loop(0, n_tiles, mask_body, jnp.zeros((1, TQ), F32))
    maskt_ref[:PAD_KEYS, :] = jnp.full((PAD_KEYS, TQ), NEG_MASK, BF16)

    eye = (lax.broadcasted_iota(I32, (TQ, TQ), 0) == lax.broadcasted_iota(I32, (TQ, TQ), 1))
    eye = jnp.where(eye, 1.0, 0.0).astype(BF16)
    for h in range(N_HEADS):
        a_ref[h * TQ:(h + 1) * TQ, :D_CKV] = qlat_ref[:, h * D_CKV:(h + 1) * D_CKV]
        a_ref[h * TQ:(h + 1) * TQ, D_CKV:] = eye
    m_ref[...] = jnp.full((n_rows, LANES), -jnp.inf, F32)
    acc_ref[...] = jnp.zeros((n_rows, 2 * D_CKV), F32)
    n_chunks = ATT_TK // LANES

    def key_rows(j):
        r0 = jnp.maximum(q0 + TQ + PAD_KEYS - (j + 1) * ATT_TK, 0)
        return pl.ds(pl.multiple_of(r0, TQ), ATT_TK)

    def logits(j, slot):
        rows = key_rows(j)
        rhs = jnp.concatenate([vaug_ref[rows, :D_CKV], maskt_ref[rows, :]], axis=1)
        s_ref[slot] = lax.dot_general(a_ref[...], rhs, (((1,), (1,)), ((), ())),
                                      preferred_element_type=F32)

    def consume(j, slot):
        vk = vaug_ref[key_rows(j), :]
        near = jnp.minimum(j, 1)
        ps = []
        for h in range(N_HEADS):
            rows = slice(h * TQ, (h + 1) * TQ)
            sh = s_ref[slot, rows, :]
            sh = jnp.concatenate([sh[:, :ATT_TK - BIAS_TK],
                                  sh[:, ATT_TK - BIAS_TK:] + bias_ref[near, h]], axis=1)
            m_old = m_ref[rows, :]
            m_new = jnp.maximum(m_old, jnp.max(sh, axis=1, keepdims=True))
            alpha = jnp.exp2(m_old - m_new)
            m_ref[rows, :] = m_new
            p = jnp.exp2(sh - jnp.concatenate([m_new] * n_chunks, axis=1))
            ps.append(p.astype(BF16))
            acc_ref[rows, :] = acc_ref[rows, :] * jnp.concatenate([alpha, alpha], axis=1)
        acc_ref[...] += jnp.dot(jnp.concatenate(ps, axis=0), vk, preferred_element_type=F32)

    n_steps = (q0 + TQ + ATT_TK - 1) // ATT_TK
    logits(0, 0)

    def att_pair(jj, carry):
        j = 2 * jj
        logits(j + 1, 1)
        consume(j, 0)
        logits(j + 2, 0)
        consume(j + 1, 1)
        return carry

    lax.fori_loop(0, n_steps // 2, att_pair, 0)

    @pl.when(n_steps % 2 == 1)
    def _():
        consume(n_steps - 1, 0)

    for h in range(N_HEADS):
        acc = acc_ref[h * TQ:(h + 1) * TQ, :]
        olat_ref[:, h * D_CKV:(h + 1) * D_CKV] = (acc[:, :D_CKV] / acc[:, D_CKV:]).astype(BF16)


def _t5_bucket(dist):
    n = jnp.maximum(dist, 0)
    nf = jnp.maximum(n, 1).astype(F32)
    large = MAX_EXACT + (jnp.log(nf / MAX_EXACT) / math.log(MAX_DISTANCE / MAX_EXACT)
                         * (NUM_BUCKETS - MAX_EXACT)).astype(I32)
    large = jnp.minimum(large, NUM_BUCKETS - 1)
    return jnp.where(n < MAX_EXACT, n, large)


def _near_bias(rel_bias):
    assert BIAS_TK >= TQ + MAX_DISTANCE - 1
    period = TQ + BIAS_TK
    e = jnp.arange(period)
    e = jnp.where(e < BIAS_TK, e, e - period)
    tab = rel_bias[_t5_bucket(BIAS_TK - TQ - e)] - rel_bias[NUM_BUCKETS - 1]
    flat = jnp.tile(tab, (TQ, 1))[:TQ * (period - 1)]
    near = flat.reshape(TQ, period - 1, N_HEADS)[:, :BIAS_TK]
    return (jnp.transpose(near, (2, 0, 1)) * LOG2E).astype(F32)


def _attention(qidx, widx, qlat, kidx, ckv, rel_bias, batch, seq):
    assert seq % SCORE_TILE == 0 and SCORE_TILE % TQ == 0 and ATT_TK % TQ == 0
    k_sel = min(TOPK_MAX, seq // 4)
    nq = seq // TQ
    r3 = lambda a: a.reshape(batch, seq, a.shape[-1])
    qidxr = jnp.transpose(qidx.reshape(batch, nq, TQ, IDX_HEADS, IDX_DIM), (0, 1, 4, 3, 2))
    qidxr = qidxr.reshape(batch, nq, IDX_DIM, IDX_HEADS * TQ)
    widxt = jnp.transpose(widx.reshape(batch, nq, TQ, IDX_HEADS), (0, 1, 3, 2))
    ckv3 = jnp.pad(r3(ckv), ((0, 0), (PAD_KEYS, 0), (0, 0)))
    vaug = jnp.concatenate([ckv3, jnp.ones_like(ckv3)], axis=-1)
    bias = _near_bias(rel_bias)
    bias = jnp.stack([bias, jnp.zeros_like(bias)])
    tri = jnp.tril(jnp.ones((PREFIX_ROWS, PREFIX_ROWS), BF16))

    qblk = lambda c: pl.BlockSpec((None, TQ, c), lambda b, i: (b, i, 0))
    qtile = lambda a: pl.BlockSpec((None, None) + a.shape[2:], lambda b, i: (b, i, 0, 0))
    per_b = lambda a: pl.BlockSpec((None,) + a.shape[1:], lambda b, i: (b,) + (0,) * (a.ndim - 1))
    n_rows = N_HEADS * TQ
    kidx3 = r3(kidx)
    out = pl.pallas_call(
        functools.partial(_attn_kernel, k_sel=k_sel),
        grid=(batch, nq),
        in_specs=[qtile(qidxr), qtile(widxt), qblk(N_HEADS * D_CKV), per_b(kidx3), per_b(vaug),
                  pl.BlockSpec(bias.shape, lambda b, i: (0, 0, 0, 0),
                               pipeline_mode=pl.Buffered(1)),
                  pl.BlockSpec(tri.shape, lambda b, i: (0, 0))],
        out_specs=qblk(N_HEADS * D_CKV),
        out_shape=jax.ShapeDtypeStruct((batch, seq, N_HEADS * D_CKV), BF16),
        scratch_shapes=[
            pltpu.VMEM((seq, TQ), I32),
            pltpu.VMEM((seq + PAD_KEYS, TQ), BF16),
            pltpu.VMEM((n_rows, 2 * D_CKV), BF16),
            pltpu.VMEM((n_rows, 2 * D_CKV), F32),
            pltpu.VMEM((n_rows, LANES), F32),
            pltpu.VMEM((2, n_rows, ATT_TK), F32),
        ],
        compiler_params=_cparams(2),
        name="dsa_attn",
    )(qidxr, widxt, r3(qlat), kidx3, vaug, bias, tri)
    return out.reshape(batch * seq, N_HEADS * D_CKV)


def _attn_out_kernel(x_ref, olat_ref, wuv_ref, wo_ref, y_ref):
    o = jnp.dot(olat_ref[...], wuv_ref[...], preferred_element_type=F32)
    y_ref[...] = x_ref[...] + jnp.dot(o.astype(BF16), wo_ref[...], preferred_element_type=F32)


def _attn_out(x2, olat, w_uv, w_o, tm=512):
    n = x2.shape[0]
    wuv = jnp.transpose(w_uv, (1, 0, 2))
    eye = jnp.eye(N_HEADS, dtype=w_uv.dtype)
    wbd = (wuv[:, :, None, :] * eye[:, None, :, None]).reshape(N_HEADS * D_CKV, N_HEADS * D_V)
    full = lambda a: pl.BlockSpec(a.shape, lambda i: (0,) * a.ndim)
    args = (x2, olat, wbd.astype(BF16), w_o.astype(BF16))
    return pl.pallas_call(
        _attn_out_kernel,
        grid=(n // tm,),
        in_specs=[pl.BlockSpec((tm, D_MODEL), lambda i: (i, 0)),
                  pl.BlockSpec((tm, N_HEADS * D_CKV), lambda i: (i, 0)), full(args[2]), full(args[3])],
        out_specs=pl.BlockSpec((tm, D_MODEL), lambda i: (i, 0)),
        out_shape=jax.ShapeDtypeStruct((n, D_MODEL), F32),
        compiler_params=_cparams(1),
        name="attn_out",
    )(*args)


def _mlp_kernel(x_ref, g_ref, wup_ref, wdn_ref, gf_ref, y_ref, acc_ref, *, f_chunk, final_norm):
    x = x_ref[...]
    h = _rms(x, g_ref[...]).astype(BF16)
    d_ff = wup_ref.shape[1]
    for c in range(d_ff // f_chunk):
        u = jnp.dot(h, wup_ref[:, c * f_chunk:(c + 1) * f_chunk], preferred_element_type=F32)
        a = jnp.square(jnp.maximum(u, 0.0)).astype(BF16)
        d = jnp.dot(a, wdn_ref[c * f_chunk:(c + 1) * f_chunk, :], preferred_element_type=F32)
        if c == 0:
            acc_ref[...] = d
        else:
            acc_ref[...] += d
    y = x + acc_ref[...]
    if final_norm:
        y = _rms(y, gf_ref[...])
    y_ref[...] = y


def _mlp(x2, g, w_up, w_down, g_final, final_norm, tm=512, f_chunk=512):
    n = x2.shape[0]
    full = lambda a: pl.BlockSpec(a.shape, lambda i: (0,) * a.ndim)
    args = (x2, g.reshape(1, -1), w_up.astype(BF16), w_down.astype(BF16), g_final.reshape(1, -1))
    return pl.pallas_call(
        functools.partial(_mlp_kernel, f_chunk=f_chunk, final_norm=final_norm),
        grid=(n // tm,),
        in_specs=[pl.BlockSpec((tm, D_MODEL), lambda i: (i, 0))] + [full(a) for a in args[1:]],
        out_specs=pl.BlockSpec((tm, D_MODEL), lambda i: (i, 0)),
        out_shape=jax.ShapeDtypeStruct((n, D_MODEL), F32),
        scratch_shapes=[pltpu.VMEM((tm, D_MODEL), F32)],
        compiler_params=_cparams(1),
        name="mlp_final" if final_norm else "mlp",
    )(*args)


def _glu_kernel(x_ref, g_ref, w_ref, b_ref, u_ref):
    h = _rms(x_ref[...], g_ref[...]).astype(BF16)
    u = jnp.dot(h, w_ref[...], preferred_element_type=F32) + b_ref[...]
    d = u.shape[1] // 2
    u_ref[...] = u[:, :d] * jax.nn.sigmoid(u[:, d:])


def _glu(x2, g, w_pw1, b_pw1, tm=512):
    n = x2.shape[0]
    full = lambda a: pl.BlockSpec(a.shape, lambda i: (0,) * a.ndim)
    args = (x2, g.reshape(1, -1), w_pw1.astype(BF16), b_pw1.reshape(1, -1))
    return pl.pallas_call(
        _glu_kernel,
        grid=(n // tm,),
        in_specs=[pl.BlockSpec((tm, D_MODEL), lambda i: (i, 0))] + [full(a) for a in args[1:]],
        out_specs=pl.BlockSpec((tm, D_MODEL), lambda i: (i, 0)),
        out_shape=jax.ShapeDtypeStruct((n, D_MODEL), F32),
        compiler_params=_cparams(1),
        name="conv_glu",
    )(*args)


HALO = 32
CONV_ROWS = 64
SUBLANE_PAD = 8


def _dwconv_kernel(x_ref, u_ref, halo_ref, wdw_ref, bdw_ref, lng_ref, lnb_ref, w2_ref, b2_ref, y_ref,
                   ext_ref, cv_ref):
    i = pl.program_id(1)
    tm = u_ref.shape[0]
    ext_ref[:HALO, :] = jnp.where(i == 0, 0.0, halo_ref[...])
    ext_ref[HALO:HALO + tm, :] = u_ref[...]
    ext_ref[HALO + tm:, :] = jnp.zeros((SUBLANE_PAD, D_MODEL), F32)
    off = HALO - (CONV_WIDTH - 1)

    def rows_body(r, carry):
        r0 = pl.multiple_of(r * CONV_ROWS, CONV_ROWS)
        for lc in range(D_MODEL // LANES):
            lanes = slice(lc * LANES, (lc + 1) * LANES)
            acc = jnp.broadcast_to(bdw_ref[:, lanes], (CONV_ROWS, LANES))
            for s in range(SUBLANE_PAD):
                part = None
                for j in range(CONV_WIDTH):
                    if (off + j) % SUBLANE_PAD != s:
                        continue
                    rows = pl.ds(pl.multiple_of(r0 + (off + j - s), SUBLANE_PAD),
                                 CONV_ROWS + SUBLANE_PAD)
                    term = wdw_ref[j:j + 1, lanes] * ext_ref[rows, lanes]
                    part = term if part is None else part + term
                if part is not None:
                    acc = acc + part[s:s + CONV_ROWS]
            cv_ref[pl.ds(r0, CONV_ROWS), lanes] = acc
        return carry

    lax.fori_loop(0, tm // CONV_ROWS, rows_body, 0)
    v = cv_ref[...]
    mu = jnp.mean(v, axis=-1, keepdims=True)
    var = jnp.mean(jnp.square(v - mu), axis=-1, keepdims=True)
    v = (v - mu) * lax.rsqrt(var + EPS) * lng_ref[...] + lnb_ref[...]
    v = v * jax.nn.sigmoid(v)
    y_ref[...] = (x_ref[...] + jnp.dot(v.astype(BF16), w2_ref[...], preferred_element_type=F32)
                  + b2_ref[...])


def _dwconv(x2, u2, w_dw, b_dw, ln_g, ln_b, w_pw2, b_pw2, batch, seq, tm=512):
    x3 = x2.reshape(batch, seq, D_MODEL)
    u3 = u2.reshape(batch, seq, D_MODEL)
    hb = tm // HALO
    blk = pl.BlockSpec((None, tm, D_MODEL), lambda b, i: (b, i, 0))
    halo = pl.BlockSpec((None, HALO, D_MODEL), lambda b, i: (b, jnp.maximum(i * hb - 1, 0), 0))
    full = lambda a: pl.BlockSpec(a.shape, lambda b, i: (0,) * a.ndim)
    args = (x3, u3, u3, w_dw, b_dw.reshape(1, -1), ln_g.reshape(1, -1), ln_b.reshape(1, -1),
            w_pw2.astype(BF16), b_pw2.reshape(1, -1))
    out = pl.pallas_call(
        _dwconv_kernel,
        grid=(batch, seq // tm),
        in_specs=[blk, blk, halo] + [full(a) for a in args[3:]],
        out_specs=blk,
        out_shape=jax.ShapeDtypeStruct((batch, seq, D_MODEL), F32),
        scratch_shapes=[pltpu.VMEM((HALO + tm + SUBLANE_PAD, D_MODEL), F32),
                        pltpu.VMEM((tm, D_MODEL), F32)],
        compiler_params=_cparams(2),
        name="conv_dw",
    )(*args)
    return out.reshape(batch * seq, D_MODEL)


def kernel(x, norm_mix, norm_mlp, norm_final, rel_bias, attn_w_in, attn_q_norm, attn_kv_norm, attn_kidx_norm, attn_w_qidx, attn_w_uq, attn_w_uk, attn_w_uv, attn_w_o, conv_w_pw1, conv_b_pw1, conv_w_dw, conv_b_dw, conv_ln_g, conv_ln_b, conv_w_pw2, conv_b_pw2, mlp_w_up, mlp_w_down):
    batch, seq, d = x.shape
    depth = norm_mix.shape[0]
    x2 = x.reshape(batch * seq, d)
    for i in range(depth):
        j = i // 2
        if i % 2 == 0:
            qidx, widx, qlat, kidx, ckv = _proj(
                x2, norm_mix[i], attn_w_in[j], attn_q_norm[j], attn_kv_norm[j], attn_kidx_norm[j],
                attn_w_qidx[j], attn_w_uq[j], attn_w_uk[j])
            olat = _attention(qidx, widx, qlat, kidx, ckv, rel_bias, batch, seq)
            x2 = _attn_out(x2, olat, attn_w_uv[j], attn_w_o[j])
        else:
            u = _glu(x2, norm_mix[i], conv_w_pw1[j], conv_b_pw1[j])
            x2 = _dwconv(x2, u, conv_w_dw[j], conv_b_dw[j], conv_ln_g[j], conv_ln_b[j],
                         conv_w_pw2[j], conv_b_pw2[j], batch, seq)
        last = i == depth - 1
        x2 = _mlp(x2, norm_mlp[i], mlp_w_up[i], mlp_w_down[i], norm_final, final_norm=last)
    if depth == 0:
        raise ValueError("depth must be positive")
    return x2.reshape(batch, seq, d)
```

```python
import functools
import math

import jax
import jax.numpy as jnp
from jax import lax
from jax.experimental import pallas as pl
from jax.experimental.pallas import tpu as pltpu

F32 = jnp.float32
BF16 = jnp.bfloat16
I32 = jnp.int32

D_MODEL = 1024
N_HEADS = 16
D_NOPE = 64
D_V = 64
D_CQ = 256
D_CKV = 128
IDX_HEADS = 8
IDX_DIM = 64
TOPK_MAX = 256
CONV_WIDTH = 31
NUM_BUCKETS = 32
MAX_EXACT = 16
MAX_DISTANCE = 128
EPS = 1e-6

LANES = 128
SUBLANES = 8
TQ = 128
SCORE_TILE = 512
ATT_TK = 512
BIAS_TK = 2 * TQ
PAD_KEYS = ATT_TK - TQ
NEG_MASK = -1e30
LOG2E = math.log2(math.e)
FIXED_BITS = 22
BITS_PER_TEST = 2
PREFIX_ROWS = 256
VMEM_LIMIT = 56 * 1024 * 1024


def _cparams(n_axes):
    return pltpu.CompilerParams(dimension_semantics=("arbitrary",) * n_axes,
                                vmem_limit_bytes=VMEM_LIMIT)


def _rms(x, g):
    return x * lax.rsqrt(jnp.mean(x * x, axis=-1, keepdims=True) + EPS) * g


def _proj_kernel(x_ref, g_ref, win_ref, qn_ref, kvn_ref, kin_ref, wqidx_ref, wuq_ref, wukt_ref,
                 qidx_ref, widx_ref, qlat_ref, kidx_ref, ckv_ref):
    h = _rms(x_ref[...], g_ref[...])
    proj = jnp.dot(h.astype(BF16), win_ref[...], preferred_element_type=F32)
    o1, o2, o3 = D_CQ, D_CQ + D_CKV, D_CQ + D_CKV + IDX_DIM
    cq = _rms(proj[:, :o1], qn_ref[...])
    ckv = _rms(proj[:, o1:o2], kvn_ref[...])
    kid = _rms(proj[:, o2:o3], kin_ref[...])
    widx_ref[...] = proj[:, o3:o3 + IDX_HEADS] * (IDX_HEADS ** -0.5)
    ckv_ref[...] = ckv.astype(BF16)
    kidx_ref[...] = kid.astype(BF16)
    cqb = cq.astype(BF16)
    qidx = jnp.dot(cqb, wqidx_ref[...], preferred_element_type=F32) * (IDX_DIM ** -0.5)
    qidx_ref[...] = qidx.astype(BF16)
    qh = jnp.dot(cqb, wuq_ref[...], preferred_element_type=F32).astype(BF16)
    for hh in range(N_HEADS):
        ql = jnp.dot(qh[:, hh * D_NOPE:(hh + 1) * D_NOPE], wukt_ref[hh],
                     preferred_element_type=F32) * (D_NOPE ** -0.5 * LOG2E)
        qlat_ref[:, hh * D_CKV:(hh + 1) * D_CKV] = ql.astype(BF16)


def _proj(x2, g, w_in, qn, kvn, kin, w_qidx, w_uq, w_uk, tm=512):
    n = x2.shape[0]
    ncol = w_in.shape[1]
    npad = -ncol % LANES
    win = jnp.pad(w_in, ((0, 0), (0, npad))).astype(BF16)
    wukt = jnp.transpose(w_uk, (1, 2, 0)).astype(BF16)
    full = lambda a: pl.BlockSpec(a.shape, lambda i: (0,) * a.ndim)
    row = lambda c: pl.BlockSpec((tm, c), lambda i: (i, 0))
    args = (x2, g.reshape(1, -1), win, qn.reshape(1, -1), kvn.reshape(1, -1), kin.reshape(1, -1),
            w_qidx.astype(BF16), w_uq.astype(BF16), wukt)
    return pl.pallas_call(
        _proj_kernel,
        grid=(n // tm,),
        in_specs=[row(D_MODEL)] + [full(a) for a in args[1:]],
        out_specs=[row(IDX_HEADS * IDX_DIM), row(IDX_HEADS), row(N_HEADS * D_CKV), row(IDX_DIM),
                   row(D_CKV)],
        out_shape=[jax.ShapeDtypeStruct((n, IDX_HEADS * IDX_DIM), BF16),
                   jax.ShapeDtypeStruct((n, IDX_HEADS), F32),
                   jax.ShapeDtypeStruct((n, N_HEADS * D_CKV), BF16),
                   jax.ShapeDtypeStruct((n, IDX_DIM), BF16),
                   jax.ShapeDtypeStruct((n, D_CKV), BF16)],
        compiler_params=_cparams(1),
        name="dsa_proj",
    )(*args)


def _attn_kernel(qidxr_ref, widxt_ref, qlat_ref, kidx_ref, vaug_ref, bias_ref, tri_ref, olat_ref,
                 keys_ref, maskt_ref, a_ref, acc_ref, m_ref, s_ref, qk_ref, *, k_sel):
    i = pl.program_id(1)
    q0 = i * TQ
    n_rows = N_HEADS * TQ
    qpos = q0 + lax.broadcasted_iota(I32, (1, TQ), 1)
    krow = lax.broadcasted_iota(I32, (SCORE_TILE, TQ), 0)
    n_tiles = i // (SCORE_TILE // TQ) + 1

    def tile_rows(tt):
        return pl.ds(pl.multiple_of(tt * SCORE_TILE, SCORE_TILE), SCORE_TILE)

    def score_dot(tt, slot):
        rows = tile_rows(jnp.minimum(tt, keys_ref.shape[0] // SCORE_TILE - 1))
        qk_ref[slot] = jnp.dot(kidx_ref[rows, :], qidxr_ref[...],
                               preferred_element_type=F32)

    def score_keys(tt, slot):
        sc = jnp.zeros((SCORE_TILE, TQ), F32)
        for h in range(IDX_HEADS):
            sc = sc + widxt_ref[h:h + 1, :] * jnp.maximum(qk_ref[slot, :, h * TQ:(h + 1) * TQ], 0.0)
        sc = jnp.where(tt * SCORE_TILE + krow <= qpos, sc, -jnp.inf)
        bits = pltpu.bitcast(sc, I32)
        keys_ref[tile_rows(tt), :] = bits ^ ((bits >> 31) & 0x7FFFFFFF)

    score_dot(0, 0)

    def score_pair(pp, carry):
        tt = 2 * pp
        score_dot(tt + 1, 1)
        score_keys(tt, 0)
        score_dot(tt + 2, 0)
        score_keys(tt + 1, 1)
        return carry

    lax.fori_loop(0, n_tiles // 2, score_pair, 0)

    @pl.when(n_tiles % 2 == 1)
    def _():
        score_keys(n_tiles - 1, 0)

    def tile_reduce(x, op):
        return op(x.reshape(SCORE_TILE // SUBLANES, SUBLANES, TQ), axis=0)

    def count_ge(thr):
        def body(tt, acc):
            return acc + tile_reduce(jnp.where(keys_ref[tile_rows(tt), :] >= thr, 1, 0), jnp.sum)
        acc = lax.fori_loop(0, n_tiles, body, jnp.zeros((SUBLANES, TQ), I32))
        return jnp.sum(acc, axis=0, keepdims=True)

    trivial = qpos < k_sel

    def bit_step(step, st):
        cand, cntc, cnt_ub = st
        trial = cand ^ (jnp.int32(1) << (31 - step))
        cnt = count_ge(trial)
        ok = cnt >= k_sel
        return jnp.where(ok, trial, cand), jnp.where(ok, cnt, cntc), jnp.where(ok, cnt_ub, cnt)

    def resolve(step, st, tval, res):
        cand, cntc, cnt_ub = st
        top = cand | ((jnp.int32(1) << (32 - step)) - 1)

        def body(tt, c):
            k = keys_ref[tile_rows(tt), :]
            inb = jnp.logical_and(k >= cand, k <= top)
            return (jnp.minimum(c[0], tile_reduce(jnp.where(inb, k, 2 ** 31 - 1), jnp.min)),
                    jnp.maximum(c[1], tile_reduce(jnp.where(inb, k, -2 ** 31), jnp.max)))

        mn, mx = lax.fori_loop(0, n_tiles, body, (jnp.full((SUBLANES, TQ), 2 ** 31 - 1, I32),
                                                  jnp.full((SUBLANES, TQ), -2 ** 31, I32)))
        mn = jnp.min(mn, axis=0, keepdims=True)
        mx = jnp.max(mx, axis=0, keepdims=True)
        known = jnp.logical_or(mn == mx, cnt_ub == k_sel - 1)
        tval = jnp.where(known, mx, tval)
        res = jnp.where(known, 1, res)
        settled = jnp.logical_or(jnp.logical_or(res > 0, cntc == k_sel), trivial)
        return tval, res, jnp.sum(jnp.where(settled, 0, 1))

    st = (jnp.full((1, TQ), -2 ** 31, I32), jnp.full((1, TQ), 2 ** 30, I32),
          jnp.zeros((1, TQ), I32))
    st = lax.fori_loop(0, FIXED_BITS, bit_step, st)
    tval, res, n_open = resolve(FIXED_BITS, st, st[0], jnp.zeros((1, TQ), I32))

    def search_body(c):
        step, st, tval, res, _ = c
        for b in range(BITS_PER_TEST):
            st = bit_step(step + b, st)
        tval, res, n_open = resolve(step + BITS_PER_TEST, st, tval, res)
        return step + BITS_PER_TEST, st, tval, res, n_open

    _, st, tval, res, _ = lax.while_loop(
        lambda c: jnp.logical_and(c[0] < 32, c[4] > 0), search_body,
        (jnp.int32(FIXED_BITS), st, tval, res, n_open))
    cand, cntc, cnt_ub = st
    exact = jnp.logical_or(cntc == k_sel, trivial)
    thr = jnp.where(exact, cand, jnp.where(res > 0, tval, cand))
    thr = jnp.where(trivial, -2 ** 31, thr)
    tie_take = jnp.where(exact, 2.0 ** 30, (k_sel - cnt_ub).astype(F32))

    def mask_tile(tt, seen):
        k = keys_ref[tile_rows(tt), :]
        idx = tt * SCORE_TILE + krow
        tie = k == thr
        tie_bf = jnp.where(tie, 1.0, 0.0).astype(BF16)
        wanted = []
        for c in range(SCORE_TILE // PREFIX_ROWS):
            rank = jnp.dot(tri_ref[...], tie_bf[c * PREFIX_ROWS:(c + 1) * PREFIX_ROWS],
                           preferred_element_type=F32)
            wanted.append(rank <= tie_take - seen)
            seen = seen + rank[PREFIX_ROWS - 1:PREFIX_ROWS, :]
        sel = jnp.logical_or(k > thr, jnp.logical_and(tie, jnp.concatenate(wanted, axis=0)))
        sel = jnp.logical_and(sel, idx <= qpos)
        rows = pl.ds(pl.multiple_of(PAD_KEYS + tt * SCORE_TILE, TQ), SCORE_TILE)
        maskt_ref[rows, :] = jnp.where(sel, 0.0, NEG_MASK).astype(BF16)
        return seen

    seen = lax.fori_loop(0, n_tiles // 2, lambda pp, c: mask_tile(2 * pp + 1, mask_tile(2 * pp, c)),
                         jnp.zeros((1, TQ), F32))

    @pl.when(n_tiles % 2 == 1)
    def _():
        mask_tile(n_tiles - 1, seen)

    maskt_ref[:PAD_KEYS, :] = jnp.full((PAD_KEYS, TQ), NEG_MASK, BF16)

    eye = (lax.broadcasted_iota(I32, (TQ, TQ), 0) == lax.broadcasted_iota(I32, (TQ, TQ), 1))
    eye = jnp.where(eye, 1.0, 0.0).astype(BF16)
    for h in range(N_HEADS):
        a_ref[h * TQ:(h + 1) * TQ, :D_CKV] = qlat_ref[:, h * D_CKV:(h + 1) * D_CKV]
        a_ref[h * TQ:(h + 1) * TQ, D_CKV:] = eye
    m_ref[...] = jnp.full((n_rows, LANES), -jnp.inf, F32)
    acc_ref[...] = jnp.zeros((n_rows, 2 * D_CKV), F32)
    n_chunks = ATT_TK // LANES

    def key_rows(j):
        r0 = jnp.maximum(q0 + TQ + PAD_KEYS - (j + 1) * ATT_TK, 0)
        return pl.ds(pl.multiple_of(r0, TQ), ATT_TK)

    def logits(j, slot):
        rows = key_rows(j)
        rhs = jnp.concatenate([vaug_ref[rows, :D_CKV], maskt_ref[rows, :]], axis=1)
        s_ref[slot] = lax.dot_general(a_ref[...], rhs, (((1,), (1,)), ((), ())),
                                      preferred_element_type=F32)

    def consume(j, slot):
        vk = vaug_ref[key_rows(j), :]
        near = jnp.minimum(j, 1)
        ps = []
        for h in range(N_HEADS):
            rows = slice(h * TQ, (h + 1) * TQ)
            sh = s_ref[slot, rows, :]
            sh = jnp.concatenate([sh[:, :ATT_TK - BIAS_TK],
                                  sh[:, ATT_TK - BIAS_TK:] + bias_ref[near, h]], axis=1)
            m_old = m_ref[rows, :]
            m_new = jnp.maximum(m_old, jnp.max(sh, axis=1, keepdims=True))
            alpha = jnp.exp2(m_old - m_new)
            m_ref[rows, :] = m_new
            p = jnp.exp2(sh - jnp.concatenate([m_new] * n_chunks, axis=1))
            ps.append(p.astype(BF16))
            acc_ref[rows, :] = acc_ref[rows, :] * jnp.concatenate([alpha, alpha], axis=1)
        acc_ref[...] += jnp.dot(jnp.concatenate(ps, axis=0), vk, preferred_element_type=F32)

    n_steps = (q0 + TQ + ATT_TK - 1) // ATT_TK
    logits(0, 0)

    def att_pair(jj, carry):
        j = 2 * jj
        logits(j + 1, 1)
        consume(j, 0)
        logits(j + 2, 0)
        consume(j + 1, 1)
        return carry

    lax.fori_loop(0, n_steps // 2, att_pair, 0)

    @pl.when(n_steps % 2 == 1)
    def _():
        consume(n_steps - 1, 0)

    for h in range(N_HEADS):
        acc = acc_ref[h * TQ:(h + 1) * TQ, :]
        olat_ref[:, h * D_CKV:(h + 1) * D_CKV] = (acc[:, :D_CKV] / acc[:, D_CKV:]).astype(BF16)


def _t5_bucket(dist):
    n = jnp.maximum(dist, 0)
    nf = jnp.maximum(n, 1).astype(F32)
    large = MAX_EXACT + (jnp.log(nf / MAX_EXACT) / math.log(MAX_DISTANCE / MAX_EXACT)
                         * (NUM_BUCKETS - MAX_EXACT)).astype(I32)
    large = jnp.minimum(large, NUM_BUCKETS - 1)
    return jnp.where(n < MAX_EXACT, n, large)


def _near_bias(rel_bias):
    assert BIAS_TK >= TQ + MAX_DISTANCE - 1
    period = TQ + BIAS_TK
    e = jnp.arange(period)
    e = jnp.where(e < BIAS_TK, e, e - period)
    tab = rel_bias[_t5_bucket(BIAS_TK - TQ - e)] - rel_bias[NUM_BUCKETS - 1]
    flat = jnp.tile(tab, (TQ, 1))[:TQ * (period - 1)]
    near = flat.reshape(TQ, period - 1, N_HEADS)[:, :BIAS_TK]
    return (jnp.transpose(near, (2, 0, 1)) * LOG2E).astype(F32)


def _attention(qidx, widx, qlat, kidx, ckv, rel_bias, batch, seq):
    assert seq % SCORE_TILE == 0 and SCORE_TILE % TQ == 0 and ATT_TK % TQ == 0
    k_sel = min(TOPK_MAX, seq // 4)
    nq = seq // TQ
    r3 = lambda a: a.reshape(batch, seq, a.shape[-1])
    qidxr = jnp.transpose(qidx.reshape(batch, nq, TQ, IDX_HEADS, IDX_DIM), (0, 1, 4, 3, 2))
    qidxr = qidxr.reshape(batch, nq, IDX_DIM, IDX_HEADS * TQ)
    widxt = jnp.transpose(widx.reshape(batch, nq, TQ, IDX_HEADS), (0, 1, 3, 2))
    ckv3 = jnp.pad(r3(ckv), ((0, 0), (PAD_KEYS, 0), (0, 0)))
    vaug = jnp.concatenate([ckv3, jnp.ones_like(ckv3)], axis=-1)
    bias = _near_bias(rel_bias)
    bias = jnp.stack([bias, jnp.zeros_like(bias)])
    tri = jnp.tril(jnp.ones((PREFIX_ROWS, PREFIX_ROWS), BF16))

    qblk = lambda c: pl.BlockSpec((None, TQ, c), lambda b, i: (b, i, 0))
    qtile = lambda a: pl.BlockSpec((None, None) + a.shape[2:], lambda b, i: (b, i, 0, 0))
    per_b = lambda a: pl.BlockSpec((None,) + a.shape[1:], lambda b, i: (b,) + (0,) * (a.ndim - 1))
    n_rows = N_HEADS * TQ
    kidx3 = r3(kidx)
    out = pl.pallas_call(
        functools.partial(_attn_kernel, k_sel=k_sel),
        grid=(batch, nq),
        in_specs=[qtile(qidxr), qtile(widxt), qblk(N_HEADS * D_CKV), per_b(kidx3), per_b(vaug),
                  pl.BlockSpec(bias.shape, lambda b, i: (0, 0, 0, 0),
                               pipeline_mode=pl.Buffered(1)),
                  pl.BlockSpec(tri.shape, lambda b, i: (0, 0))],
        out_specs=qblk(N_HEADS * D_CKV),
        out_shape=jax.ShapeDtypeStruct((batch, seq, N_HEADS * D_CKV), BF16),
        scratch_shapes=[
            pltpu.VMEM((seq, TQ), I32),
            pltpu.VMEM((seq + PAD_KEYS, TQ), BF16),
            pltpu.VMEM((n_rows, 2 * D_CKV), BF16),
            pltpu.VMEM((n_rows, 2 * D_CKV), F32),
            pltpu.VMEM((n_rows, LANES), F32),
            pltpu.VMEM((2, n_rows, ATT_TK), F32),
            pltpu.VMEM((2, SCORE_TILE, IDX_HEADS * TQ), F32),
        ],
        compiler_params=_cparams(2),
        name="dsa_attn",
    )(qidxr, widxt, r3(qlat), kidx3, vaug, bias, tri)
    return out.reshape(batch * seq, N_HEADS * D_CKV)


def _attn_out_kernel(x_ref, olat_ref, wuv_ref, wo_ref, y_ref):
    pair = 2 * D_CKV
    o = [jnp.dot(olat_ref[:, p * pair:(p + 1) * pair], wuv_ref[p], preferred_element_type=F32)
         for p in range(N_HEADS // 2)]
    o = jnp.concatenate(o, axis=1).astype(BF16)
    y_ref[...] = x_ref[...] + jnp.dot(o, wo_ref[...], preferred_element_type=F32)


def _attn_out(x2, olat, w_uv, w_o, tm=512):
    n = x2.shape[0]
    wuv = jnp.transpose(w_uv, (1, 0, 2)).reshape(N_HEADS // 2, 2, D_CKV, D_V)
    eye = jnp.eye(2, dtype=w_uv.dtype)
    wbd = (wuv[:, :, :, None, :] * eye[None, :, None, :, None]).reshape(N_HEADS // 2, 2 * D_CKV, 2 * D_V)
    full = lambda a: pl.BlockSpec(a.shape, lambda i: (0,) * a.ndim)
    args = (x2, olat, wbd.astype(BF16), w_o.astype(BF16))
    return pl.pallas_call(
        _attn_out_kernel,
        grid=(n // tm,),
        in_specs=[pl.BlockSpec((tm, D_MODEL), lambda i: (i, 0)),
                  pl.BlockSpec((tm, N_HEADS * D_CKV), lambda i: (i, 0)), full(args[2]), full(args[3])],
        out_specs=pl.BlockSpec((tm, D_MODEL), lambda i: (i, 0)),
        out_shape=jax.ShapeDtypeStruct((n, D_MODEL), F32),
        compiler_params=_cparams(1),
        name="attn_out",
    )(*args)


def _mlp_kernel(x_ref, g_ref, wup_ref, wdn_ref, gf_ref, y_ref, acc_ref, *, f_chunk, final_norm):
    x = x_ref[...]
    h = _rms(x, g_ref[...]).astype(BF16)
    d_ff = wup_ref.shape[1]
    for c in range(d_ff // f_chunk):
        u = jnp.dot(h, wup_ref[:, c * f_chunk:(c + 1) * f_chunk], preferred_element_type=F32)
        a = jnp.square(jnp.maximum(u, 0.0)).astype(BF16)
        d = jnp.dot(a, wdn_ref[c * f_chunk:(c + 1) * f_chunk, :], preferred_element_type=F32)
        if c == 0:
            acc_ref[...] = d
        else:
            acc_ref[...] += d
    y = x + acc_ref[...]
    if final_norm:
        y = _rms(y, gf_ref[...])
    y_ref[...] = y


def _mlp(x2, g, w_up, w_down, g_final, final_norm, tm=512, f_chunk=512):
    n = x2.shape[0]
    full = lambda a: pl.BlockSpec(a.shape, lambda i: (0,) * a.ndim)
    args = (x2, g.reshape(1, -1), w_up.astype(BF16), w_down.astype(BF16), g_final.reshape(1, -1))
    return pl.pallas_call(
        functools.partial(_mlp_kernel, f_chunk=f_chunk, final_norm=final_norm),
        grid=(n // tm,),
        in_specs=[pl.BlockSpec((tm, D_MODEL), lambda i: (i, 0))] + [full(a) for a in args[1:]],
        out_specs=pl.BlockSpec((tm, D_MODEL), lambda i: (i, 0)),
        out_shape=jax.ShapeDtypeStruct((n, D_MODEL), F32),
        scratch_shapes=[pltpu.VMEM((tm, D_MODEL), F32)],
        compiler_params=_cparams(1),
        name="mlp_final" if final_norm else "mlp",
    )(*args)


def _glu_kernel(x_ref, g_ref, w_ref, b_ref, u_ref):
    h = _rms(x_ref[...], g_ref[...]).astype(BF16)
    u = jnp.dot(h, w_ref[...], preferred_element_type=F32) + b_ref[...]
    d = u.shape[1] // 2
    u_ref[...] = u[:, :d] * jax.nn.sigmoid(u[:, d:])


def _glu(x2, g, w_pw1, b_pw1, tm=512):
    n = x2.shape[0]
    full = lambda a: pl.BlockSpec(a.shape, lambda i: (0,) * a.ndim)
    args = (x2, g.reshape(1, -1), w_pw1.astype(BF16), b_pw1.reshape(1, -1))
    return pl.pallas_call(
        _glu_kernel,
        grid=(n // tm,),
        in_specs=[pl.BlockSpec((tm, D_MODEL), lambda i: (i, 0))] + [full(a) for a in args[1:]],
        out_specs=pl.BlockSpec((tm, D_MODEL), lambda i: (i, 0)),
        out_shape=jax.ShapeDtypeStruct((n, D_MODEL), F32),
        compiler_params=_cparams(1),
        name="conv_glu",
    )(*args)


HALO = 32
CONV_ROWS = 64
SUBLANE_PAD = 8


def _dwconv_kernel(x_ref, u_ref, halo_ref, wdw_ref, bdw_ref, lng_ref, lnb_ref, w2_ref, b2_ref, y_ref,
                   ext_ref, cv_ref):
    i = pl.program_id(1)
    tm = u_ref.shape[0]
    ext_ref[:HALO, :] = jnp.where(i == 0, 0.0, halo_ref[...])
    ext_ref[HALO:HALO + tm, :] = u_ref[...]
    ext_ref[HALO + tm:, :] = jnp.zeros((SUBLANE_PAD, D_MODEL), F32)
    off = HALO - (CONV_WIDTH - 1)

    def rows_body(r, carry):
        r0 = pl.multiple_of(r * CONV_ROWS, CONV_ROWS)
        for lc in range(D_MODEL // LANES):
            lanes = slice(lc * LANES, (lc + 1) * LANES)
            acc = jnp.broadcast_to(bdw_ref[:, lanes], (CONV_ROWS, LANES))
            for s in range(SUBLANE_PAD):
                part = None
                for j in range(CONV_WIDTH):
                    if (off + j) % SUBLANE_PAD != s:
                        continue
                    rows = pl.ds(pl.multiple_of(r0 + (off + j - s), SUBLANE_PAD),
                                 CONV_ROWS + SUBLANE_PAD)
                    term = wdw_ref[j:j + 1, lanes] * ext_ref[rows, lanes]
                    part = term if part is None else part + term
                if part is not None:
                    acc = acc + part[s:s + CONV_ROWS]
            cv_ref[pl.ds(r0, CONV_ROWS), lanes] = acc
        return carry

    lax.fori_loop(0, tm // CONV_ROWS, rows_body, 0)
    v = cv_ref[...]
    mu = jnp.mean(v, axis=-1, keepdims=True)
    var = jnp.mean(jnp.square(v - mu), axis=-1, keepdims=True)
    v = (v - mu) * lax.rsqrt(var + EPS) * lng_ref[...] + lnb_ref[...]
    v = v * jax.nn.sigmoid(v)
    y_ref[...] = (x_ref[...] + jnp.dot(v.astype(BF16), w2_ref[...], preferred_element_type=F32)
                  + b2_ref[...])


def _dwconv(x2, u2, w_dw, b_dw, ln_g, ln_b, w_pw2, b_pw2, batch, seq, tm=512):
    x3 = x2.reshape(batch, seq, D_MODEL)
    u3 = u2.reshape(batch, seq, D_MODEL)
    hb = tm // HALO
    blk = pl.BlockSpec((None, tm, D_MODEL), lambda b, i: (b, i, 0))
    halo = pl.BlockSpec((None, HALO, D_MODEL), lambda b, i: (b, jnp.maximum(i * hb - 1, 0), 0))
    full = lambda a: pl.BlockSpec(a.shape, lambda b, i: (0,) * a.ndim)
    args = (x3, u3, u3, w_dw, b_dw.reshape(1, -1), ln_g.reshape(1, -1), ln_b.reshape(1, -1),
            w_pw2.astype(BF16), b_pw2.reshape(1, -1))
    out = pl.pallas_call(
        _dwconv_kernel,
        grid=(batch, seq // tm),
        in_specs=[blk, blk, halo] + [full(a) for a in args[3:]],
        out_specs=blk,
        out_shape=jax.ShapeDtypeStruct((batch, seq, D_MODEL), F32),
        scratch_shapes=[pltpu.VMEM((HALO + tm + SUBLANE_PAD, D_MODEL), F32),
                        pltpu.VMEM((tm, D_MODEL), F32)],
        compiler_params=_cparams(2),
        name="conv_dw",
    )(*args)
    return out.reshape(batch * seq, D_MODEL)


def kernel(x, norm_mix, norm_mlp, norm_final, rel_bias, attn_w_in, attn_q_norm, attn_kv_norm, attn_kidx_norm, attn_w_qidx, attn_w_uq, attn_w_uk, attn_w_uv, attn_w_o, conv_w_pw1, conv_b_pw1, conv_w_dw, conv_b_dw, conv_ln_g, conv_ln_b, conv_w_pw2, conv_b_pw2, mlp_w_up, mlp_w_down):
    batch, seq, d = x.shape
    depth = norm_mix.shape[0]
    x2 = x.reshape(batch * seq, d)
    for i in range(depth):
        j = i // 2
        if i % 2 == 0:
            qidx, widx, qlat, kidx, ckv = _proj(
                x2, norm_mix[i], attn_w_in[j], attn_q_norm[j], attn_kv_norm[j], attn_kidx_norm[j],
                attn_w_qidx[j], attn_w_uq[j], attn_w_uk[j])
            olat = _attention(qidx, widx, qlat, kidx, ckv, rel_bias, batch, seq)
            x2 = _attn_out(x2, olat, attn_w_uv[j], attn_w_o[j])
        else:
            u = _glu(x2, norm_mix[i], conv_w_pw1[j], conv_b_pw1[j])
            x2 = _dwconv(x2, u, conv_w_dw[j], conv_b_dw[j], conv_ln_g[j], conv_ln_b[j],
                         conv_w_pw2[j], conv_b_pw2[j], batch, seq)
        last = i == depth - 1
        x2 = _mlp(x2, norm_mlp[i], mlp_w_up[i], mlp_w_down[i], norm_final, final_norm=last)
    if depth == 0:
        raise ValueError("depth must be positive")
    return x2.reshape(batch, seq, d)
```

```python
import functools
import math

import jax
import jax.numpy as jnp
from jax import lax
from jax.experimental import pallas as pl
from jax.experimental.pallas import tpu as pltpu

F32 = jnp.float32
BF16 = jnp.bfloat16
I32 = jnp.int32

D_MODEL = 1024
N_HEADS = 16
D_NOPE = 64
D_V = 64
D_CQ = 256
D_CKV = 128
IDX_HEADS = 8
IDX_DIM = 64
TOPK_MAX = 256
CONV_WIDTH = 31
NUM_BUCKETS = 32
MAX_EXACT = 16
MAX_DISTANCE = 128
EPS = 1e-6

LANES = 128
SUBLANES = 8
TQ = 128
SCORE_TILE = 512
ATT_TK = 512
BIAS_TK = 2 * TQ
PAD_KEYS = ATT_TK - TQ
NEG_MASK = -1e30
LOG2E = math.log2(math.e)
FIXED_BITS = 22
BITS_PER_TEST = 2
PREFIX_ROWS = 256
VMEM_LIMIT = 56 * 1024 * 1024


def _cparams(n_axes):
    return pltpu.CompilerParams(dimension_semantics=("arbitrary",) * n_axes,
                                vmem_limit_bytes=VMEM_LIMIT)


def _rms(x, g):
    return x * lax.rsqrt(jnp.mean(x * x, axis=-1, keepdims=True) + EPS) * g


def _proj_kernel(x_ref, g_ref, win_ref, qn_ref, kvn_ref, kin_ref, wqidx_ref, wuq_ref, wukt_ref,
                 qidx_ref, widx_ref, qlat_ref, kidx_ref, ckv_ref):
    h = _rms(x_ref[...], g_ref[...])
    proj = jnp.dot(h.astype(BF16), win_ref[...], preferred_element_type=F32)
    o1, o2, o3 = D_CQ, D_CQ + D_CKV, D_CQ + D_CKV + IDX_DIM
    cq = _rms(proj[:, :o1], qn_ref[...])
    ckv = _rms(proj[:, o1:o2], kvn_ref[...])
    kid = _rms(proj[:, o2:o3], kin_ref[...])
    widx_ref[...] = proj[:, o3:o3 + IDX_HEADS] * (IDX_HEADS ** -0.5)
    ckv_ref[...] = ckv.astype(BF16)
    kidx_ref[...] = kid.astype(BF16)
    cqb = cq.astype(BF16)
    qidx = jnp.dot(cqb, wqidx_ref[...], preferred_element_type=F32) * (IDX_DIM ** -0.5)
    qidx_ref[...] = qidx.astype(BF16)
    qh = jnp.dot(cqb, wuq_ref[...], preferred_element_type=F32).astype(BF16)
    for hp in range(N_HEADS // 2):
        ql = jnp.dot(qh[:, hp * 2 * D_NOPE:(hp + 1) * 2 * D_NOPE], wukt_ref[hp],
                     preferred_element_type=F32) * (D_NOPE ** -0.5 * LOG2E)
        qlat_ref[:, hp * 2 * D_CKV:(hp + 1) * 2 * D_CKV] = ql.astype(BF16)


def _proj(x2, g, w_in, qn, kvn, kin, w_qidx, w_uq, w_uk, tm=512):
    n = x2.shape[0]
    ncol = w_in.shape[1]
    npad = -ncol % LANES
    win = jnp.pad(w_in, ((0, 0), (0, npad))).astype(BF16)
    wukt = jnp.transpose(w_uk, (1, 2, 0)).reshape(N_HEADS // 2, 2, D_NOPE, D_CKV)
    eye = jnp.eye(2, dtype=w_uk.dtype)
    wukt = (wukt[:, :, :, None, :] * eye[None, :, None, :, None]).reshape(
        N_HEADS // 2, 2 * D_NOPE, 2 * D_CKV).astype(BF16)
    full = lambda a: pl.BlockSpec(a.shape, lambda i: (0,) * a.ndim)
    row = lambda c: pl.BlockSpec((tm, c), lambda i: (i, 0))
    args = (x2, g.reshape(1, -1), win, qn.reshape(1, -1), kvn.reshape(1, -1), kin.reshape(1, -1),
            w_qidx.astype(BF16), w_uq.astype(BF16), wukt)
    return pl.pallas_call(
        _proj_kernel,
        grid=(n // tm,),
        in_specs=[row(D_MODEL)] + [full(a) for a in args[1:]],
        out_specs=[row(IDX_HEADS * IDX_DIM), row(IDX_HEADS), row(N_HEADS * D_CKV), row(IDX_DIM),
                   row(D_CKV)],
        out_shape=[jax.ShapeDtypeStruct((n, IDX_HEADS * IDX_DIM), BF16),
                   jax.ShapeDtypeStruct((n, IDX_HEADS), F32),
                   jax.ShapeDtypeStruct((n, N_HEADS * D_CKV), BF16),
                   jax.ShapeDtypeStruct((n, IDX_DIM), BF16),
                   jax.ShapeDtypeStruct((n, D_CKV), BF16)],
        compiler_params=_cparams(1),
        name="dsa_proj",
    )(*args)


def _attn_kernel(qidxr_ref, widxt_ref, qlat_ref, kidx_ref, vaug_ref, bias_ref, tri_ref, olat_ref,
                 keys_ref, maskt_ref, a_ref, acc_ref, m_ref, s_ref, qk_ref, *, k_sel):
    i = pl.program_id(1)
    q0 = i * TQ
    n_rows = N_HEADS * TQ
    qpos = q0 + lax.broadcasted_iota(I32, (1, TQ), 1)
    krow = lax.broadcasted_iota(I32, (SCORE_TILE, TQ), 0)
    n_tiles = i // (SCORE_TILE // TQ) + 1

    def tile_rows(tt):
        return pl.ds(pl.multiple_of(tt * SCORE_TILE, SCORE_TILE), SCORE_TILE)

    def score_dot(tt, slot):
        rows = tile_rows(jnp.minimum(tt, keys_ref.shape[0] // SCORE_TILE - 1))
        qk_ref[slot] = jnp.dot(kidx_ref[rows, :], qidxr_ref[...],
                               preferred_element_type=F32)

    def score_keys(tt, slot):
        sc = jnp.zeros((SCORE_TILE, TQ), F32)
        for h in range(IDX_HEADS):
            sc = sc + widxt_ref[h:h + 1, :] * jnp.maximum(qk_ref[slot, :, h * TQ:(h + 1) * TQ], 0.0)
        sc = jnp.where(tt * SCORE_TILE + krow <= qpos, sc, -jnp.inf)
        bits = pltpu.bitcast(sc, I32)
        keys_ref[tile_rows(tt), :] = bits ^ ((bits >> 31) & 0x7FFFFFFF)

    score_dot(0, 0)

    def score_pair(pp, carry):
        tt = 2 * pp
        score_dot(tt + 1, 1)
        score_keys(tt, 0)
        score_dot(tt + 2, 0)
        score_keys(tt + 1, 1)
        return carry

    lax.fori_loop(0, n_tiles // 2, score_pair, 0)

    @pl.when(n_tiles % 2 == 1)
    def _():
        score_keys(n_tiles - 1, 0)

    def tile_reduce(x, op):
        return op(x.reshape(SCORE_TILE // SUBLANES, SUBLANES, TQ), axis=0)

    def count_ge(thr):
        def body(tt, acc):
            return acc + tile_reduce(jnp.where(keys_ref[tile_rows(tt), :] >= thr, 1, 0), jnp.sum)
        acc = lax.fori_loop(0, n_tiles, body, jnp.zeros((SUBLANES, TQ), I32))
        return jnp.sum(acc, axis=0, keepdims=True)

    trivial = qpos < k_sel

    def bit_step(step, st):
        cand, cntc, cnt_ub = st
        trial = cand ^ (jnp.int32(1) << (31 - step))
        cnt = count_ge(trial)
        ok = cnt >= k_sel
        return jnp.where(ok, trial, cand), jnp.where(ok, cnt, cntc), jnp.where(ok, cnt_ub, cnt)

    def resolve(step, st, tval, res):
        cand, cntc, cnt_ub = st
        top = cand | ((jnp.int32(1) << (32 - step)) - 1)

        def body(tt, c):
            k = keys_ref[tile_rows(tt), :]
            inb = jnp.logical_and(k >= cand, k <= top)
            return (jnp.minimum(c[0], tile_reduce(jnp.where(inb, k, 2 ** 31 - 1), jnp.min)),
                    jnp.maximum(c[1], tile_reduce(jnp.where(inb, k, -2 ** 31), jnp.max)))

        mn, mx = lax.fori_loop(0, n_tiles, body, (jnp.full((SUBLANES, TQ), 2 ** 31 - 1, I32),
                                                  jnp.full((SUBLANES, TQ), -2 ** 31, I32)))
        mn = jnp.min(mn, axis=0, keepdims=True)
        mx = jnp.max(mx, axis=0, keepdims=True)
        known = jnp.logical_or(mn == mx, cnt_ub == k_sel - 1)
        tval = jnp.where(known, mx, tval)
        res = jnp.where(known, 1, res)
        settled = jnp.logical_or(jnp.logical_or(res > 0, cntc == k_sel), trivial)
        return tval, res, jnp.sum(jnp.where(settled, 0, 1))

    st = (jnp.full((1, TQ), -2 ** 31, I32), jnp.full((1, TQ), 2 ** 30, I32),
          jnp.zeros((1, TQ), I32))
    st = lax.fori_loop(0, FIXED_BITS, bit_step, st)
    tval, res, n_open = resolve(FIXED_BITS, st, st[0], jnp.zeros((1, TQ), I32))

    def search_body(c):
        step, st, tval, res, _ = c
        for b in range(BITS_PER_TEST):
            st = bit_step(step + b, st)
        tval, res, n_open = resolve(step + BITS_PER_TEST, st, tval, res)
        return step + BITS_PER_TEST, st, tval, res, n_open

    _, st, tval, res, _ = lax.while_loop(
        lambda c: jnp.logical_and(c[0] < 32, c[4] > 0), search_body,
        (jnp.int32(FIXED_BITS), st, tval, res, n_open))
    cand, cntc, cnt_ub = st
    exact = jnp.logical_or(cntc == k_sel, trivial)
    thr = jnp.where(exact, cand, jnp.where(res > 0, tval, cand))
    thr = jnp.where(trivial, -2 ** 31, thr)
    tie_take = jnp.where(exact, 2.0 ** 30, (k_sel - cnt_ub).astype(F32))

    def mask_tile(tt, seen):
        k = keys_ref[tile_rows(tt), :]
        idx = tt * SCORE_TILE + krow
        tie = k == thr
        tie_bf = jnp.where(tie, 1.0, 0.0).astype(BF16)
        wanted = []
        for c in range(SCORE_TILE // PREFIX_ROWS):
            rank = jnp.dot(tri_ref[...], tie_bf[c * PREFIX_ROWS:(c + 1) * PREFIX_ROWS],
                           preferred_element_type=F32)
            wanted.append(rank <= tie_take - seen)
            seen = seen + rank[PREFIX_ROWS - 1:PREFIX_ROWS, :]
        sel = jnp.logical_or(k > thr, jnp.logical_and(tie, jnp.concatenate(wanted, axis=0)))
        sel = jnp.logical_and(sel, idx <= qpos)
        rows = pl.ds(pl.multiple_of(PAD_KEYS + tt * SCORE_TILE, TQ), SCORE_TILE)
        maskt_ref[rows, :] = jnp.where(sel, 0.0, NEG_MASK).astype(BF16)
        return seen

    seen = lax.fori_loop(0, n_tiles // 2, lambda pp, c: mask_tile(2 * pp + 1, mask_tile(2 * pp, c)),
                         jnp.zeros((1, TQ), F32))

    @pl.when(n_tiles % 2 == 1)
    def _():
        mask_tile(n_tiles - 1, seen)

    maskt_ref[:PAD_KEYS, :] = jnp.full((PAD_KEYS, TQ), NEG_MASK, BF16)

    eye = (lax.broadcasted_iota(I32, (TQ, TQ), 0) == lax.broadcasted_iota(I32, (TQ, TQ), 1))
    eye = jnp.where(eye, 1.0, 0.0).astype(BF16)
    for h in range(N_HEADS):
        a_ref[h * TQ:(h + 1) * TQ, :D_CKV] = qlat_ref[:, h * D_CKV:(h + 1) * D_CKV]
        a_ref[h * TQ:(h + 1) * TQ, D_CKV:] = eye
    m_ref[...] = jnp.full((n_rows, LANES), -jnp.inf, F32)
    acc_ref[...] = jnp.zeros((n_rows, 2 * D_CKV), F32)
    n_chunks = ATT_TK // LANES

    def key_rows(j):
        r0 = jnp.maximum(q0 + TQ + PAD_KEYS - (j + 1) * ATT_TK, 0)
        return pl.ds(pl.multiple_of(r0, TQ), ATT_TK)

    def logits(j, slot):
        rows = key_rows(j)
        rhs = jnp.concatenate([vaug_ref[rows, :D_CKV], maskt_ref[rows, :]], axis=1)
        s_ref[slot] = lax.dot_general(a_ref[...], rhs, (((1,), (1,)), ((), ())),
                                      preferred_element_type=F32)

    def consume(j, slot):
        vk = vaug_ref[key_rows(j), :]
        near = jnp.minimum(j, 1)
        ps = []
        for h in range(N_HEADS):
            rows = slice(h * TQ, (h + 1) * TQ)
            sh = s_ref[slot, rows, :]
            sh = jnp.concatenate([sh[:, :ATT_TK - BIAS_TK],
                                  sh[:, ATT_TK - BIAS_TK:] + bias_ref[near, h]], axis=1)
            m_old = m_ref[rows, :]
            m_new = jnp.maximum(m_old, jnp.max(sh, axis=1, keepdims=True))
            alpha = jnp.exp2(m_old - m_new)
            m_ref[rows, :] = m_new
            p = jnp.exp2(sh - jnp.concatenate([m_new] * n_chunks, axis=1))
            ps.append(p.astype(BF16))
            acc_ref[rows, :] = acc_ref[rows, :] * jnp.concatenate([alpha, alpha], axis=1)
        acc_ref[...] += jnp.dot(jnp.concatenate(ps, axis=0), vk, preferred_element_type=F32)

    n_steps = (q0 + TQ + ATT_TK - 1) // ATT_TK
    logits(0, 0)

    def att_pair(jj, carry):
        j = 2 * jj
        logits(j + 1, 1)
        consume(j, 0)
        logits(j + 2, 0)
        consume(j + 1, 1)
        return carry

    lax.fori_loop(0, n_steps // 2, att_pair, 0)

    @pl.when(n_steps % 2 == 1)
    def _():
        consume(n_steps - 1, 0)

    for h in range(N_HEADS):
        acc = acc_ref[h * TQ:(h + 1) * TQ, :]
        olat_ref[:, h * D_CKV:(h + 1) * D_CKV] = (acc[:, :D_CKV] / acc[:, D_CKV:]).astype(BF16)


def _t5_bucket(dist):
    n = jnp.maximum(dist, 0)
    nf = jnp.maximum(n, 1).astype(F32)
    large = MAX_EXACT + (jnp.log(nf / MAX_EXACT) / math.log(MAX_DISTANCE / MAX_EXACT)
                         * (NUM_BUCKETS - MAX_EXACT)).astype(I32)
    large = jnp.minimum(large, NUM_BUCKETS - 1)
    return jnp.where(n < MAX_EXACT, n, large)


def _near_bias(rel_bias):
    assert BIAS_TK >= TQ + MAX_DISTANCE - 1
    period = TQ + BIAS_TK
    e = jnp.arange(period)
    e = jnp.where(e < BIAS_TK, e, e - period)
    tab = rel_bias[_t5_bucket(BIAS_TK - TQ - e)] - rel_bias[NUM_BUCKETS - 1]
    flat = jnp.tile(tab, (TQ, 1))[:TQ * (period - 1)]
    near = flat.reshape(TQ, period - 1, N_HEADS)[:, :BIAS_TK]
    return (jnp.transpose(near, (2, 0, 1)) * LOG2E).astype(F32)


def _attention(qidx, widx, qlat, kidx, ckv, rel_bias, batch, seq):
    assert seq % SCORE_TILE == 0 and SCORE_TILE % TQ == 0 and ATT_TK % TQ == 0
    k_sel = min(TOPK_MAX, seq // 4)
    nq = seq // TQ
    r3 = lambda a: a.reshape(batch, seq, a.shape[-1])
    qidxr = jnp.transpose(qidx.reshape(batch, nq, TQ, IDX_HEADS, IDX_DIM), (0, 1, 4, 3, 2))
    qidxr = qidxr.reshape(batch, nq, IDX_DIM, IDX_HEADS * TQ)
    widxt = jnp.transpose(widx.reshape(batch, nq, TQ, IDX_HEADS), (0, 1, 3, 2))
    ckv3 = jnp.pad(r3(ckv), ((0, 0), (PAD_KEYS, 0), (0, 0)))
    vaug = jnp.concatenate([ckv3, jnp.ones_like(ckv3)], axis=-1)
    bias = _near_bias(rel_bias)
    bias = jnp.stack([bias, jnp.zeros_like(bias)])
    tri = jnp.tril(jnp.ones((PREFIX_ROWS, PREFIX_ROWS), BF16))

    qblk = lambda c: pl.BlockSpec((None, TQ, c), lambda b, i: (b, i, 0))
    qtile = lambda a: pl.BlockSpec((None, None) + a.shape[2:], lambda b, i: (b, i, 0, 0))
    per_b = lambda a: pl.BlockSpec((None,) + a.shape[1:], lambda b, i: (b,) + (0,) * (a.ndim - 1))
    n_rows = N_HEADS * TQ
    kidx3 = r3(kidx)
    out = pl.pallas_call(
        functools.partial(_attn_kernel, k_sel=k_sel),
        grid=(batch, nq),
        in_specs=[qtile(qidxr), qtile(widxt), qblk(N_HEADS * D_CKV), per_b(kidx3), per_b(vaug),
                  pl.BlockSpec(bias.shape, lambda b, i: (0, 0, 0, 0),
                               pipeline_mode=pl.Buffered(1)),
                  pl.BlockSpec(tri.shape, lambda b, i: (0, 0))],
        out_specs=qblk(N_HEADS * D_CKV),
        out_shape=jax.ShapeDtypeStruct((batch, seq, N_HEADS * D_CKV), BF16),
        scratch_shapes=[
            pltpu.VMEM((seq, TQ), I32),
            pltpu.VMEM((seq + PAD_KEYS, TQ), BF16),
            pltpu.VMEM((n_rows, 2 * D_CKV), BF16),
            pltpu.VMEM((n_rows, 2 * D_CKV), F32),
            pltpu.VMEM((n_rows, LANES), F32),
            pltpu.VMEM((2, n_rows, ATT_TK), F32),
            pltpu.VMEM((2, SCORE_TILE, IDX_HEADS * TQ), F32),
        ],
        compiler_params=_cparams(2),
        name="dsa_attn",
    )(qidxr, widxt, r3(qlat), kidx3, vaug, bias, tri)
    return out.reshape(batch * seq, N_HEADS * D_CKV)


def _mlp_kernel(*refs, f_chunk, final_norm, with_attn_out):
    if with_attn_out:
        x_ref, olat_ref, wuv_ref, wo_ref, g_ref, wup_ref, wdn_ref, gf_ref, y_ref, acc_ref = refs
        pair = 2 * D_CKV
        o = [jnp.dot(olat_ref[:, p * pair:(p + 1) * pair], wuv_ref[p], preferred_element_type=F32)
             for p in range(N_HEADS // 2)]
        o = jnp.concatenate(o, axis=1).astype(BF16)
        x = x_ref[...] + jnp.dot(o, wo_ref[...], preferred_element_type=F32)
    else:
        x_ref, g_ref, wup_ref, wdn_ref, gf_ref, y_ref, acc_ref = refs
        x = x_ref[...]
    h = _rms(x, g_ref[...]).astype(BF16)
    d_ff = wup_ref.shape[1]
    for c in range(d_ff // f_chunk):
        u = jnp.dot(h, wup_ref[:, c * f_chunk:(c + 1) * f_chunk], preferred_element_type=F32)
        a = jnp.square(jnp.maximum(u, 0.0)).astype(BF16)
        d = jnp.dot(a, wdn_ref[c * f_chunk:(c + 1) * f_chunk, :], preferred_element_type=F32)
        if c == 0:
            acc_ref[...] = d
        else:
            acc_ref[...] += d
    y = x + acc_ref[...]
    if final_norm:
        y = _rms(y, gf_ref[...])
    y_ref[...] = y


def _mlp(x2, g, w_up, w_down, g_final, final_norm, attn_out=None, tm=512, f_chunk=512):
    n = x2.shape[0]
    const = lambda a: pl.BlockSpec(a.shape, lambda i: (0,) * a.ndim, pipeline_mode=pl.Buffered(1))
    rows = lambda c: pl.BlockSpec((tm, c), lambda i: (i, 0))
    args, specs = [x2], [rows(D_MODEL)]
    if attn_out is not None:
        olat, w_uv, w_o = attn_out
        wuv = jnp.transpose(w_uv, (1, 0, 2)).reshape(N_HEADS // 2, 2, D_CKV, D_V)
        eye = jnp.eye(2, dtype=w_uv.dtype)
        wbd = (wuv[:, :, :, None, :] * eye[None, :, None, :, None]).reshape(
            N_HEADS // 2, 2 * D_CKV, 2 * D_V)
        extra = [olat, wbd.astype(BF16), w_o.astype(BF16)]
        args += extra
        specs += [rows(N_HEADS * D_CKV), const(extra[1]), const(extra[2])]
    weights = [g.reshape(1, -1), w_up.astype(BF16), w_down.astype(BF16), g_final.reshape(1, -1)]
    args += weights
    specs += [const(a) for a in weights]
    return pl.pallas_call(
        functools.partial(_mlp_kernel, f_chunk=f_chunk, final_norm=final_norm,
                          with_attn_out=attn_out is not None),
        grid=(n // tm,),
        in_specs=specs,
        out_specs=rows(D_MODEL),
        out_shape=jax.ShapeDtypeStruct((n, D_MODEL), F32),
        scratch_shapes=[pltpu.VMEM((tm, D_MODEL), F32)],
        compiler_params=_cparams(1),
        name="mlp_final" if final_norm else "mlp",
    )(*args)


def _glu_kernel(x_ref, g_ref, w_ref, b_ref, u_ref):
    h = _rms(x_ref[...], g_ref[...]).astype(BF16)
    u = jnp.dot(h, w_ref[...], preferred_element_type=F32) + b_ref[...]
    d = u.shape[1] // 2
    u_ref[...] = u[:, :d] * jax.nn.sigmoid(u[:, d:])


def _glu(x2, g, w_pw1, b_pw1, tm=512):
    n = x2.shape[0]
    full = lambda a: pl.BlockSpec(a.shape, lambda i: (0,) * a.ndim)
    args = (x2, g.reshape(1, -1), w_pw1.astype(BF16), b_pw1.reshape(1, -1))
    return pl.pallas_call(
        _glu_kernel,
        grid=(n // tm,),
        in_specs=[pl.BlockSpec((tm, D_MODEL), lambda i: (i, 0))] + [full(a) for a in args[1:]],
        out_specs=pl.BlockSpec((tm, D_MODEL), lambda i: (i, 0)),
        out_shape=jax.ShapeDtypeStruct((n, D_MODEL), F32),
        compiler_params=_cparams(1),
        name="conv_glu",
    )(*args)


HALO = 32
CONV_ROWS = 64
SUBLANE_PAD = 8


def _dwconv_kernel(x_ref, u_ref, halo_ref, wdw_ref, bdw_ref, lng_ref, lnb_ref, w2_ref, b2_ref, y_ref,
                   ext_ref, cv_ref):
    i = pl.program_id(1)
    tm = u_ref.shape[0]
    ext_ref[:HALO, :] = jnp.where(i == 0, 0.0, halo_ref[...])
    ext_ref[HALO:HALO + tm, :] = u_ref[...]
    ext_ref[HALO + tm:, :] = jnp.zeros((SUBLANE_PAD, D_MODEL), F32)
    off = HALO - (CONV_WIDTH - 1)

    def rows_body(r, carry):
        r0 = pl.multiple_of(r * CONV_ROWS, CONV_ROWS)
        for lc in range(D_MODEL // LANES):
            lanes = slice(lc * LANES, (lc + 1) * LANES)
            acc = jnp.broadcast_to(bdw_ref[:, lanes], (CONV_ROWS, LANES))
            for s in range(SUBLANE_PAD):
                part = None
                for j in range(CONV_WIDTH):
                    if (off + j) % SUBLANE_PAD != s:
                        continue
                    rows = pl.ds(pl.multiple_of(r0 + (off + j - s), SUBLANE_PAD),
                                 CONV_ROWS + SUBLANE_PAD)
                    term = wdw_ref[j:j + 1, lanes] * ext_ref[rows, lanes]
                    part = term if part is None else part + term
                if part is not None:
                    acc = acc + part[s:s + CONV_ROWS]
            cv_ref[pl.ds(r0, CONV_ROWS), lanes] = acc
        return carry

    lax.fori_loop(0, tm // CONV_ROWS, rows_body, 0)
    v = cv_ref[...]
    mu = jnp.mean(v, axis=-1, keepdims=True)
    var = jnp.mean(jnp.square(v - mu), axis=-1, keepdims=True)
    v = (v - mu) * lax.rsqrt(var + EPS) * lng_ref[...] + lnb_ref[...]
    v = v * jax.nn.sigmoid(v)
    y_ref[...] = (x_ref[...] + jnp.dot(v.astype(BF16), w2_ref[...], preferred_element_type=F32)
                  + b2_ref[...])


def _dwconv(x2, u2, w_dw, b_dw, ln_g, ln_b, w_pw2, b_pw2, batch, seq, tm=512):
    x3 = x2.reshape(batch, seq, D_MODEL)
    u3 = u2.reshape(batch, seq, D_MODEL)
    hb = tm // HALO
    blk = pl.BlockSpec((None, tm, D_MODEL), lambda b, i: (b, i, 0))
    halo = pl.BlockSpec((None, HALO, D_MODEL), lambda b, i: (b, jnp.maximum(i * hb - 1, 0), 0))
    full = lambda a: pl.BlockSpec(a.shape, lambda b, i: (0,) * a.ndim)
    args = (x3, u3, u3, w_dw, b_dw.reshape(1, -1), ln_g.reshape(1, -1), ln_b.reshape(1, -1),
            w_pw2.astype(BF16), b_pw2.reshape(1, -1))
    out = pl.pallas_call(
        _dwconv_kernel,
        grid=(batch, seq // tm),
        in_specs=[blk, blk, halo] + [full(a) for a in args[3:]],
        out_specs=blk,
        out_shape=jax.ShapeDtypeStruct((batch, seq, D_MODEL), F32),
        scratch_shapes=[pltpu.VMEM((HALO + tm + SUBLANE_PAD, D_MODEL), F32),
                        pltpu.VMEM((tm, D_MODEL), F32)],
        compiler_params=_cparams(2),
        name="conv_dw",
    )(*args)
    return out.reshape(batch * seq, D_MODEL)


def kernel(x, norm_mix, norm_mlp, norm_final, rel_bias, attn_w_in, attn_q_norm, attn_kv_norm, attn_kidx_norm, attn_w_qidx, attn_w_uq, attn_w_uk, attn_w_uv, attn_w_o, conv_w_pw1, conv_b_pw1, conv_w_dw, conv_b_dw, conv_ln_g, conv_ln_b, conv_w_pw2, conv_b_pw2, mlp_w_up, mlp_w_down):
    batch, seq, d = x.shape
    depth = norm_mix.shape[0]
    x2 = x.reshape(batch * seq, d)
    for i in range(depth):
        j = i // 2
        if i % 2 == 0:
            qidx, widx, qlat, kidx, ckv = _proj(
                x2, norm_mix[i], attn_w_in[j], attn_q_norm[j], attn_kv_norm[j], attn_kidx_norm[j],
                attn_w_qidx[j], attn_w_uq[j], attn_w_uk[j])
            olat = _attention(qidx, widx, qlat, kidx, ckv, rel_bias, batch, seq)
            mixer_out = (olat, attn_w_uv[j], attn_w_o[j])
        else:
            mixer_out = None
            u = _glu(x2, norm_mix[i], conv_w_pw1[j], conv_b_pw1[j])
            x2 = _dwconv(x2, u, conv_w_dw[j], conv_b_dw[j], conv_ln_g[j], conv_ln_b[j],
                         conv_w_pw2[j], conv_b_pw2[j], batch, seq)
        last = i == depth - 1
        x2 = _mlp(x2, norm_mlp[i], mlp_w_up[i], mlp_w_down[i], norm_final, final_norm=last,
                  attn_out=mixer_out)
    if depth == 0:
        raise ValueError("depth must be positive")
    return x2.reshape(batch, seq, d)
```

```python
import functools
import math

import jax
import jax.numpy as jnp
from jax import lax
from jax.experimental import pallas as pl
from jax.experimental.pallas import tpu as pltpu

F32 = jnp.float32
BF16 = jnp.bfloat16
I32 = jnp.int32

D_MODEL = 1024
N_HEADS = 16
D_NOPE = 64
D_V = 64
D_CQ = 256
D_CKV = 128
IDX_HEADS = 8
IDX_DIM = 64
TOPK_MAX = 256
CONV_WIDTH = 31
NUM_BUCKETS = 32
MAX_EXACT = 16
MAX_DISTANCE = 128
EPS = 1e-6

LANES = 128
SUBLANES = 8
TQ = 128
SCORE_TILE = 512
ATT_TK = 512
BIAS_TK = 2 * TQ
PAD_KEYS = ATT_TK - TQ
NEG_MASK = -1e30
LOG2E = math.log2(math.e)
FIXED_BITS = 22
BITS_PER_TEST = 2
PREFIX_ROWS = 256
VMEM_LIMIT = 56 * 1024 * 1024


def _cparams(n_axes):
    return pltpu.CompilerParams(dimension_semantics=("arbitrary",) * n_axes,
                                vmem_limit_bytes=VMEM_LIMIT)


def _rms(x, g):
    return x * lax.rsqrt(jnp.mean(x * x, axis=-1, keepdims=True) + EPS) * g


def _proj_kernel(x_ref, g_ref, win_ref, qn_ref, kvn_ref, kin_ref, wqidx_ref, wuq_ref, wukt_ref,
                 qidx_ref, widx_ref, qlat_ref, kidx_ref, ckv_ref):
    h = _rms(x_ref[...], g_ref[...])
    proj = jnp.dot(h.astype(BF16), win_ref[...], preferred_element_type=F32)
    o1, o2, o3 = D_CQ, D_CQ + D_CKV, D_CQ + D_CKV + IDX_DIM
    cq = _rms(proj[:, :o1], qn_ref[...])
    ckv = _rms(proj[:, o1:o2], kvn_ref[...])
    kid = _rms(proj[:, o2:o3], kin_ref[...])
    widx_ref[...] = proj[:, o3:o3 + IDX_HEADS] * (IDX_HEADS ** -0.5)
    ckv_ref[...] = ckv.astype(BF16)
    kidx_ref[...] = kid.astype(BF16)
    cqb = cq.astype(BF16)
    qidx = jnp.dot(cqb, wqidx_ref[...], preferred_element_type=F32) * (IDX_DIM ** -0.5)
    qidx_ref[...] = qidx.astype(BF16)
    qh = jnp.dot(cqb, wuq_ref[...], preferred_element_type=F32).astype(BF16)
    for hp in range(N_HEADS // 2):
        ql = jnp.dot(qh[:, hp * 2 * D_NOPE:(hp + 1) * 2 * D_NOPE], wukt_ref[hp],
                     preferred_element_type=F32) * (D_NOPE ** -0.5 * LOG2E)
        qlat_ref[:, hp * 2 * D_CKV:(hp + 1) * 2 * D_CKV] = ql.astype(BF16)


def _proj(x2, g, w_in, qn, kvn, kin, w_qidx, w_uq, w_uk, tm=512):
    n = x2.shape[0]
    ncol = w_in.shape[1]
    npad = -ncol % LANES
    win = jnp.pad(w_in, ((0, 0), (0, npad))).astype(BF16)
    wukt = jnp.transpose(w_uk, (1, 2, 0)).reshape(N_HEADS // 2, 2, D_NOPE, D_CKV)
    eye = jnp.eye(2, dtype=w_uk.dtype)
    wukt = (wukt[:, :, :, None, :] * eye[None, :, None, :, None]).reshape(
        N_HEADS // 2, 2 * D_NOPE, 2 * D_CKV).astype(BF16)
    full = lambda a: pl.BlockSpec(a.shape, lambda i: (0,) * a.ndim)
    row = lambda c: pl.BlockSpec((tm, c), lambda i: (i, 0))
    args = (x2, g.reshape(1, -1), win, qn.reshape(1, -1), kvn.reshape(1, -1), kin.reshape(1, -1),
            w_qidx.astype(BF16), w_uq.astype(BF16), wukt)
    return pl.pallas_call(
        _proj_kernel,
        grid=(n // tm,),
        in_specs=[row(D_MODEL)] + [full(a) for a in args[1:]],
        out_specs=[row(IDX_HEADS * IDX_DIM), row(IDX_HEADS), row(N_HEADS * D_CKV), row(IDX_DIM),
                   row(D_CKV)],
        out_shape=[jax.ShapeDtypeStruct((n, IDX_HEADS * IDX_DIM), BF16),
                   jax.ShapeDtypeStruct((n, IDX_HEADS), F32),
                   jax.ShapeDtypeStruct((n, N_HEADS * D_CKV), BF16),
                   jax.ShapeDtypeStruct((n, IDX_DIM), BF16),
                   jax.ShapeDtypeStruct((n, D_CKV), BF16)],
        compiler_params=_cparams(1),
        name="dsa_proj",
    )(*args)


def _attn_kernel(qidx_ref, widxt_ref, qlat_ref, kidx_ref, vaug_ref, bias_ref, tri_ref, olat_ref,
                 keys_ref, maskt_ref, a_ref, acc_ref, m_ref, s_ref, qk_ref, qs_ref, *, k_sel):
    n_rows = N_HEADS * TQ
    q0 = 2 * pl.program_id(1) * TQ

    def select(sub):
        q0 = (2 * pl.program_id(1) + sub) * TQ
        qpos = q0 + lax.broadcasted_iota(I32, (1, TQ), 1)
        krow = lax.broadcasted_iota(I32, (SCORE_TILE, TQ), 0)
        n_tiles = q0 // SCORE_TILE + 1

        def tile_rows(tt):
            return pl.ds(pl.multiple_of(tt * SCORE_TILE, SCORE_TILE), SCORE_TILE)

        for h in range(IDX_HEADS):
            qs_ref[h * TQ:(h + 1) * TQ, :] = qidx_ref[sub * TQ:(sub + 1) * TQ,
                                                      h * IDX_DIM:(h + 1) * IDX_DIM]

        def score_dot(tt, slot):
            rows = tile_rows(jnp.minimum(tt, keys_ref.shape[0] // SCORE_TILE - 1))
            qk_ref[slot] = lax.dot_general(kidx_ref[rows, :], qs_ref[...], (((1,), (1,)), ((), ())),
                                           preferred_element_type=F32)

        def score_keys(tt, slot):
            sc = jnp.zeros((SCORE_TILE, TQ), F32)
            for h in range(IDX_HEADS):
                sc = sc + widxt_ref[sub, h:h + 1, :] * jnp.maximum(
                    qk_ref[slot, :, h * TQ:(h + 1) * TQ], 0.0)
            sc = jnp.where(tt * SCORE_TILE + krow <= qpos, sc, -jnp.inf)
            bits = pltpu.bitcast(sc, I32)
            keys_ref[tile_rows(tt), :] = bits ^ ((bits >> 31) & 0x7FFFFFFF)

        score_dot(0, 0)

        def score_pair(pp, carry):
            tt = 2 * pp
            score_dot(tt + 1, 1)
            score_keys(tt, 0)
            score_dot(tt + 2, 0)
            score_keys(tt + 1, 1)
            return carry

        lax.fori_loop(0, n_tiles // 2, score_pair, 0)

        @pl.when(n_tiles % 2 == 1)
        def _():
            score_keys(n_tiles - 1, 0)

        def tile_reduce(x, op):
            return op(x.reshape(SCORE_TILE // SUBLANES, SUBLANES, TQ), axis=0)

        def count_ge(thr):
            def body(tt, acc):
                return acc + tile_reduce(jnp.where(keys_ref[tile_rows(tt), :] >= thr, 1, 0), jnp.sum)
            acc = lax.fori_loop(0, n_tiles, body, jnp.zeros((SUBLANES, TQ), I32))
            return jnp.sum(acc, axis=0, keepdims=True)

        trivial = qpos < k_sel

        def bit_step(step, st):
            cand, cntc, cnt_ub = st
            trial = cand ^ (jnp.int32(1) << (31 - step))
            cnt = count_ge(trial)
            ok = cnt >= k_sel
            return jnp.where(ok, trial, cand), jnp.where(ok, cnt, cntc), jnp.where(ok, cnt_ub, cnt)

        def resolve(step, st, tval, res):
            cand, cntc, cnt_ub = st
            top = cand | ((jnp.int32(1) << (32 - step)) - 1)

            def body(tt, c):
                k = keys_ref[tile_rows(tt), :]
                inb = jnp.logical_and(k >= cand, k <= top)
                return (jnp.minimum(c[0], tile_reduce(jnp.where(inb, k, 2 ** 31 - 1), jnp.min)),
                        jnp.maximum(c[1], tile_reduce(jnp.where(inb, k, -2 ** 31), jnp.max)))

            mn, mx = lax.fori_loop(0, n_tiles, body, (jnp.full((SUBLANES, TQ), 2 ** 31 - 1, I32),
                                                      jnp.full((SUBLANES, TQ), -2 ** 31, I32)))
            mn = jnp.min(mn, axis=0, keepdims=True)
            mx = jnp.max(mx, axis=0, keepdims=True)
            known = jnp.logical_or(mn == mx, cnt_ub == k_sel - 1)
            tval = jnp.where(known, mx, tval)
            res = jnp.where(known, 1, res)
            settled = jnp.logical_or(jnp.logical_or(res > 0, cntc == k_sel), trivial)
            return tval, res, jnp.sum(jnp.where(settled, 0, 1))

        st = (jnp.full((1, TQ), -2 ** 31, I32), jnp.full((1, TQ), 2 ** 30, I32),
              jnp.zeros((1, TQ), I32))
        st = lax.fori_loop(0, FIXED_BITS, bit_step, st)
        tval, res, n_open = resolve(FIXED_BITS, st, st[0], jnp.zeros((1, TQ), I32))

        def search_body(c):
            step, st, tval, res, _ = c
            for b in range(BITS_PER_TEST):
                st = bit_step(step + b, st)
            tval, res, n_open = resolve(step + BITS_PER_TEST, st, tval, res)
            return step + BITS_PER_TEST, st, tval, res, n_open

        _, st, tval, res, _ = lax.while_loop(
            lambda c: jnp.logical_and(c[0] < 32, c[4] > 0), search_body,
            (jnp.int32(FIXED_BITS), st, tval, res, n_open))
        cand, cntc, cnt_ub = st
        exact = jnp.logical_or(cntc == k_sel, trivial)
        thr = jnp.where(exact, cand, jnp.where(res > 0, tval, cand))
        thr = jnp.where(trivial, -2 ** 31, thr)
        tie_take = jnp.where(exact, 2.0 ** 30, (k_sel - cnt_ub).astype(F32))

        def mask_tile(tt, seen):
            k = keys_ref[tile_rows(tt), :]
            idx = tt * SCORE_TILE + krow
            tie = k == thr
            tie_bf = jnp.where(tie, 1.0, 0.0).astype(BF16)
            wanted = []
            for c in range(SCORE_TILE // PREFIX_ROWS):
                rank = jnp.dot(tri_ref[...], tie_bf[c * PREFIX_ROWS:(c + 1) * PREFIX_ROWS],
                               preferred_element_type=F32)
                wanted.append(rank <= tie_take - seen)
                seen = seen + rank[PREFIX_ROWS - 1:PREFIX_ROWS, :]
            sel = jnp.logical_or(k > thr, jnp.logical_and(tie, jnp.concatenate(wanted, axis=0)))
            sel = jnp.logical_and(sel, idx <= qpos)
            rows = pl.ds(pl.multiple_of(PAD_KEYS + tt * SCORE_TILE, TQ), SCORE_TILE)
            maskt_ref[sub, rows, :] = jnp.where(sel, 0.0, NEG_MASK).astype(BF16)
            return seen

        seen = lax.fori_loop(0, n_tiles // 2, lambda pp, c: mask_tile(2 * pp + 1, mask_tile(2 * pp, c)),
                             jnp.zeros((1, TQ), F32))

        @pl.when(n_tiles % 2 == 1)
        def _():
            mask_tile(n_tiles - 1, seen)

        maskt_ref[sub, :PAD_KEYS, :] = jnp.full((PAD_KEYS, TQ), NEG_MASK, BF16)

    for sub in range(2):
        select(sub)

    eye = (lax.broadcasted_iota(I32, (TQ, TQ), 0) == lax.broadcasted_iota(I32, (TQ, TQ), 1))
    eye = jnp.where(eye, 1.0, 0.0).astype(BF16)
    for sub in range(2):
        for h in range(N_HEADS):
            a_ref[sub, h * TQ:(h + 1) * TQ, :D_CKV] = qlat_ref[sub * TQ:(sub + 1) * TQ,
                                                                h * D_CKV:(h + 1) * D_CKV]
            a_ref[sub, h * TQ:(h + 1) * TQ, D_CKV:] = eye
    m_ref[...] = jnp.full((2, n_rows, LANES), -jnp.inf, F32)
    acc_ref[...] = jnp.zeros((2, n_rows, 2 * D_CKV), F32)
    n_chunks = ATT_TK // LANES
    n_steps = (q0 + TQ + ATT_TK - 1) // ATT_TK

    def key_rows(sub, j):
        r0 = jnp.maximum(q0 + (sub + 1) * TQ + PAD_KEYS - (j + 1) * ATT_TK, 0)
        return pl.ds(pl.multiple_of(r0, TQ), ATT_TK)

    def logits(sub, j):
        rows = key_rows(sub, j)
        rhs = jnp.concatenate([vaug_ref[rows, :D_CKV], maskt_ref[sub, rows, :]], axis=1)
        s_ref[sub] = lax.dot_general(a_ref[sub], rhs, (((1,), (1,)), ((), ())),
                                     preferred_element_type=F32)

    def consume(sub, j):
        vk = vaug_ref[key_rows(sub, j), :]
        near = jnp.minimum(j, 1)
        ps = []
        for h in range(N_HEADS):
            rows = slice(h * TQ, (h + 1) * TQ)
            sh = s_ref[sub, rows, :]
            sh = jnp.concatenate([sh[:, :ATT_TK - BIAS_TK],
                                  sh[:, ATT_TK - BIAS_TK:] + bias_ref[near, h]], axis=1)
            m_old = m_ref[sub, rows, :]
            m_new = jnp.maximum(m_old, jnp.max(sh, axis=1, keepdims=True))
            alpha = jnp.exp2(m_old - m_new)
            m_ref[sub, rows, :] = m_new
            p = jnp.exp2(sh - jnp.concatenate([m_new] * n_chunks, axis=1))
            ps.append(p.astype(BF16))
            acc_ref[sub, rows, :] = acc_ref[sub, rows, :] * jnp.concatenate([alpha, alpha], axis=1)
        acc_ref[sub] += jnp.dot(jnp.concatenate(ps, axis=0), vk, preferred_element_type=F32)

    logits(0, 0)

    def att_step(j, carry):
        logits(1, j)
        consume(0, j)
        logits(0, j + 1)
        consume(1, j)
        return carry

    lax.fori_loop(0, n_steps, att_step, 0)

    for sub in range(2):
        for h in range(N_HEADS):
            acc = acc_ref[sub, h * TQ:(h + 1) * TQ, :]
            olat_ref[sub * TQ:(sub + 1) * TQ, h * D_CKV:(h + 1) * D_CKV] = (
                acc[:, :D_CKV] / acc[:, D_CKV:]).astype(BF16)


def _t5_bucket(dist):
    n = jnp.maximum(dist, 0)
    nf = jnp.maximum(n, 1).astype(F32)
    large = MAX_EXACT + (jnp.log(nf / MAX_EXACT) / math.log(MAX_DISTANCE / MAX_EXACT)
                         * (NUM_BUCKETS - MAX_EXACT)).astype(I32)
    large = jnp.minimum(large, NUM_BUCKETS - 1)
    return jnp.where(n < MAX_EXACT, n, large)


def _near_bias(rel_bias):
    assert BIAS_TK >= TQ + MAX_DISTANCE - 1
    period = TQ + BIAS_TK
    e = jnp.arange(period)
    e = jnp.where(e < BIAS_TK, e, e - period)
    tab = rel_bias[_t5_bucket(BIAS_TK - TQ - e)] - rel_bias[NUM_BUCKETS - 1]
    flat = jnp.tile(tab, (TQ, 1))[:TQ * (period - 1)]
    near = flat.reshape(TQ, period - 1, N_HEADS)[:, :BIAS_TK]
    return (jnp.transpose(near, (2, 0, 1)) * LOG2E).astype(F32)


def _attention(qidx, widx, qlat, kidx, ckv, rel_bias, batch, seq):
    assert seq % SCORE_TILE == 0 and SCORE_TILE % TQ == 0 and ATT_TK % (2 * TQ) == 0
    k_sel = min(TOPK_MAX, seq // 4)
    nq = seq // TQ
    r3 = lambda a: a.reshape(batch, seq, a.shape[-1])
    widxt = jnp.transpose(widx.reshape(batch, nq, TQ, IDX_HEADS), (0, 1, 3, 2))
    ckv3 = jnp.pad(r3(ckv), ((0, 0), (PAD_KEYS, 0), (0, 0)))
    vaug = jnp.concatenate([ckv3, jnp.ones_like(ckv3)], axis=-1)
    bias = _near_bias(rel_bias)
    bias = jnp.stack([bias, jnp.zeros_like(bias)])
    tri = jnp.tril(jnp.ones((PREFIX_ROWS, PREFIX_ROWS), BF16))

    qblk = lambda c: pl.BlockSpec((None, 2 * TQ, c), lambda b, i: (b, i, 0))
    qtile = lambda a: pl.BlockSpec((None, 2) + a.shape[2:], lambda b, i: (b, i, 0, 0))
    per_b = lambda a: pl.BlockSpec((None,) + a.shape[1:], lambda b, i: (b,) + (0,) * (a.ndim - 1))
    n_rows = N_HEADS * TQ
    kidx3 = r3(kidx)
    out = pl.pallas_call(
        functools.partial(_attn_kernel, k_sel=k_sel),
        grid=(batch, nq // 2),
        in_specs=[qblk(IDX_HEADS * IDX_DIM), qtile(widxt), qblk(N_HEADS * D_CKV), per_b(kidx3), per_b(vaug),
                  pl.BlockSpec(bias.shape, lambda b, i: (0, 0, 0, 0),
                               pipeline_mode=pl.Buffered(1)),
                  pl.BlockSpec(tri.shape, lambda b, i: (0, 0))],
        out_specs=qblk(N_HEADS * D_CKV),
        out_shape=jax.ShapeDtypeStruct((batch, seq, N_HEADS * D_CKV), BF16),
        scratch_shapes=[
            pltpu.VMEM((seq, TQ), I32),
            pltpu.VMEM((2, seq + PAD_KEYS, TQ), BF16),
            pltpu.VMEM((2, n_rows, 2 * D_CKV), BF16),
            pltpu.VMEM((2, n_rows, 2 * D_CKV), F32),
            pltpu.VMEM((2, n_rows, LANES), F32),
            pltpu.VMEM((2, n_rows, ATT_TK), F32),
            pltpu.VMEM((2, SCORE_TILE, IDX_HEADS * TQ), F32),
            pltpu.VMEM((IDX_HEADS * TQ, IDX_DIM), BF16),
        ],
        compiler_params=_cparams(2),
        name="dsa_attn",
    )(r3(qidx), widxt, r3(qlat), kidx3, vaug, bias, tri)
    return out.reshape(batch * seq, N_HEADS * D_CKV)


def _mlp_kernel(*refs, f_chunk, final_norm, with_attn_out):
    if with_attn_out:
        x_ref, olat_ref, wuv_ref, wo_ref, g_ref, wup_ref, wdn_ref, gf_ref, y_ref, acc_ref = refs
        pair = 2 * D_CKV
        o = [jnp.dot(olat_ref[:, p * pair:(p + 1) * pair], wuv_ref[p], preferred_element_type=F32)
             for p in range(N_HEADS // 2)]
        o = jnp.concatenate(o, axis=1).astype(BF16)
        x = x_ref[...] + jnp.dot(o, wo_ref[...], preferred_element_type=F32)
    else:
        x_ref, g_ref, wup_ref, wdn_ref, gf_ref, y_ref, acc_ref = refs
        x = x_ref[...]
    h = _rms(x, g_ref[...]).astype(BF16)
    d_ff = wup_ref.shape[1]
    for c in range(d_ff // f_chunk):
        u = jnp.dot(h, wup_ref[:, c * f_chunk:(c + 1) * f_chunk], preferred_element_type=F32)
        a = jnp.square(jnp.maximum(u, 0.0)).astype(BF16)
        d = jnp.dot(a, wdn_ref[c * f_chunk:(c + 1) * f_chunk, :], preferred_element_type=F32)
        if c == 0:
            acc_ref[...] = d
        else:
            acc_ref[...] += d
    y = x + acc_ref[...]
    if final_norm:
        y = _rms(y, gf_ref[...])
    y_ref[...] = y


def _mlp(x2, g, w_up, w_down, g_final, final_norm, attn_out=None, tm=512, f_chunk=512):
    n = x2.shape[0]
    const = lambda a: pl.BlockSpec(a.shape, lambda i: (0,) * a.ndim, pipeline_mode=pl.Buffered(1))
    rows = lambda c: pl.BlockSpec((tm, c), lambda i: (i, 0))
    args, specs = [x2], [rows(D_MODEL)]
    if attn_out is not None:
        olat, w_uv, w_o = attn_out
        wuv = jnp.transpose(w_uv, (1, 0, 2)).reshape(N_HEADS // 2, 2, D_CKV, D_V)
        eye = jnp.eye(2, dtype=w_uv.dtype)
        wbd = (wuv[:, :, :, None, :] * eye[None, :, None, :, None]).reshape(
            N_HEADS // 2, 2 * D_CKV, 2 * D_V)
        extra = [olat, wbd.astype(BF16), w_o.astype(BF16)]
        args += extra
        specs += [rows(N_HEADS * D_CKV), const(extra[1]), const(extra[2])]
    weights = [g.reshape(1, -1), w_up.astype(BF16), w_down.astype(BF16), g_final.reshape(1, -1)]
    args += weights
    specs += [const(a) for a in weights]
    return pl.pallas_call(
        functools.partial(_mlp_kernel, f_chunk=f_chunk, final_norm=final_norm,
                          with_attn_out=attn_out is not None),
        grid=(n // tm,),
        in_specs=specs,
        out_specs=rows(D_MODEL),
        out_shape=jax.ShapeDtypeStruct((n, D_MODEL), F32),
        scratch_shapes=[pltpu.VMEM((tm, D_MODEL), F32)],
        compiler_params=_cparams(1),
        name="mlp_final" if final_norm else "mlp",
    )(*args)


def _glu_kernel(x_ref, g_ref, w_ref, b_ref, u_ref):
    h = _rms(x_ref[...], g_ref[...]).astype(BF16)
    u = jnp.dot(h, w_ref[...], preferred_element_type=F32) + b_ref[...]
    d = u.shape[1] // 2
    u_ref[...] = u[:, :d] * jax.nn.sigmoid(u[:, d:])


def _glu(x2, g, w_pw1, b_pw1, tm=512):
    n = x2.shape[0]
    full = lambda a: pl.BlockSpec(a.shape, lambda i: (0,) * a.ndim)
    args = (x2, g.reshape(1, -1), w_pw1.astype(BF16), b_pw1.reshape(1, -1))
    return pl.pallas_call(
        _glu_kernel,
        grid=(n // tm,),
        in_specs=[pl.BlockSpec((tm, D_MODEL), lambda i: (i, 0))] + [full(a) for a in args[1:]],
        out_specs=pl.BlockSpec((tm, D_MODEL), lambda i: (i, 0)),
        out_shape=jax.ShapeDtypeStruct((n, D_MODEL), F32),
        compiler_params=_cparams(1),
        name="conv_glu",
    )(*args)


HALO = 32
CONV_ROWS = 128
SUBLANE_PAD = 8


def _dwconv_kernel(x_ref, u_ref, halo_ref, wdw_ref, bdw_ref, lng_ref, lnb_ref, w2_ref, b2_ref, y_ref,
                   ext_ref, cv_ref):
    i = pl.program_id(1)
    tm = u_ref.shape[0]
    ext_ref[:HALO, :] = jnp.where(i == 0, 0.0, halo_ref[...])
    ext_ref[HALO:HALO + tm, :] = u_ref[...]
    ext_ref[HALO + tm:, :] = jnp.zeros((SUBLANE_PAD, D_MODEL), F32)
    off = HALO - (CONV_WIDTH - 1)

    def rows_body(r, carry):
        r0 = pl.multiple_of(r * CONV_ROWS, CONV_ROWS)
        for lc in range(D_MODEL // LANES):
            lanes = slice(lc * LANES, (lc + 1) * LANES)
            acc = jnp.broadcast_to(bdw_ref[:, lanes], (CONV_ROWS, LANES))
            for s in range(SUBLANE_PAD):
                part = None
                for j in range(CONV_WIDTH):
                    if (off + j) % SUBLANE_PAD != s:
                        continue
                    rows = pl.ds(pl.multiple_of(r0 + (off + j - s), SUBLANE_PAD),
                                 CONV_ROWS + SUBLANE_PAD)
                    term = wdw_ref[j:j + 1, lanes] * ext_ref[rows, lanes]
                    part = term if part is None else part + term
                if part is not None:
                    acc = acc + part[s:s + CONV_ROWS]
            cv_ref[pl.ds(r0, CONV_ROWS), lanes] = acc
        return carry

    lax.fori_loop(0, tm // CONV_ROWS, rows_body, 0)
    v = cv_ref[...]
    mu = jnp.mean(v, axis=-1, keepdims=True)
    var = jnp.mean(jnp.square(v - mu), axis=-1, keepdims=True)
    v = (v - mu) * lax.rsqrt(var + EPS) * lng_ref[...] + lnb_ref[...]
    v = v * jax.nn.sigmoid(v)
    y_ref[...] = (x_ref[...] + jnp.dot(v.astype(BF16), w2_ref[...], preferred_element_type=F32)
                  + b2_ref[...])


def _dwconv(x2, u2, w_dw, b_dw, ln_g, ln_b, w_pw2, b_pw2, batch, seq, tm=512):
    x3 = x2.reshape(batch, seq, D_MODEL)
    u3 = u2.reshape(batch, seq, D_MODEL)
    hb = tm // HALO
    blk = pl.BlockSpec((None, tm, D_MODEL), lambda b, i: (b, i, 0))
    halo = pl.BlockSpec((None, HALO, D_MODEL), lambda b, i: (b, jnp.maximum(i * hb - 1, 0), 0))
    full = lambda a: pl.BlockSpec(a.shape, lambda b, i: (0,) * a.ndim)
    args = (x3, u3, u3, w_dw, b_dw.reshape(1, -1), ln_g.reshape(1, -1), ln_b.reshape(1, -1),
            w_pw2.astype(BF16), b_pw2.reshape(1, -1))
    out = pl.pallas_call(
        _dwconv_kernel,
        grid=(batch, seq // tm),
        in_specs=[blk, blk, halo] + [full(a) for a in args[3:]],
        out_specs=blk,
        out_shape=jax.ShapeDtypeStruct((batch, seq, D_MODEL), F32),
        scratch_shapes=[pltpu.VMEM((HALO + tm + SUBLANE_PAD, D_MODEL), F32),
                        pltpu.VMEM((tm, D_MODEL), F32)],
        compiler_params=_cparams(2),
        name="conv_dw",
    )(*args)
    return out.reshape(batch * seq, D_MODEL)


def kernel(x, norm_mix, norm_mlp, norm_final, rel_bias, attn_w_in, attn_q_norm, attn_kv_norm, attn_kidx_norm, attn_w_qidx, attn_w_uq, attn_w_uk, attn_w_uv, attn_w_o, conv_w_pw1, conv_b_pw1, conv_w_dw, conv_b_dw, conv_ln_g, conv_ln_b, conv_w_pw2, conv_b_pw2, mlp_w_up, mlp_w_down):
    batch, seq, d = x.shape
    depth = norm_mix.shape[0]
    x2 = x.reshape(batch * seq, d)
    for i in range(depth):
        j = i // 2
        if i % 2 == 0:
            qidx, widx, qlat, kidx, ckv = _proj(
                x2, norm_mix[i], attn_w_in[j], attn_q_norm[j], attn_kv_norm[j], attn_kidx_norm[j],
                attn_w_qidx[j], attn_w_uq[j], attn_w_uk[j])
            olat = _attention(qidx, widx, qlat, kidx, ckv, rel_bias, batch, seq)
            mixer_out = (olat, attn_w_uv[j], attn_w_o[j])
        else:
            mixer_out = None
            u = _glu(x2, norm_mix[i], conv_w_pw1[j], conv_b_pw1[j])
            x2 = _dwconv(x2, u, conv_w_dw[j], conv_b_dw[j], conv_ln_g[j], conv_ln_b[j],
                         conv_w_pw2[j], conv_b_pw2[j], batch, seq)
        last = i == depth - 1
        x2 = _mlp(x2, norm_mlp[i], mlp_w_up[i], mlp_w_down[i], norm_final, final_norm=last,
                  attn_out=mixer_out)
    if depth == 0:
        raise ValueError("depth must be positive")
    return x2.reshape(batch, seq, d)
```

```python
import functools
import math

import jax
import jax.numpy as jnp
from jax import lax
from jax.experimental import pallas as pl
from jax.experimental.pallas import tpu as pltpu

F32 = jnp.float32
BF16 = jnp.bfloat16
I32 = jnp.int32

D_MODEL = 1024
N_HEADS = 16
D_NOPE = 64
D_V = 64
D_CQ = 256
D_CKV = 128
IDX_HEADS = 8
IDX_DIM = 64
TOPK_MAX = 256
CONV_WIDTH = 31
NUM_BUCKETS = 32
MAX_EXACT = 16
MAX_DISTANCE = 128
EPS = 1e-6

LANES = 128
SUBLANES = 8
TQ = 128
SCORE_TILE = 512
ATT_TK = 512
BIAS_TK = 2 * TQ
PAD_KEYS = ATT_TK - TQ
NEG_MASK = -1e30
LOG2E = math.log2(math.e)
FIXED_BITS = 22
BITS_PER_TEST = 2
PREFIX_ROWS = 256
VMEM_LIMIT = 56 * 1024 * 1024


def _cparams(n_axes):
    return pltpu.CompilerParams(dimension_semantics=("arbitrary",) * n_axes,
                                vmem_limit_bytes=VMEM_LIMIT)


def _rms(x, g):
    return x * lax.rsqrt(jnp.mean(x * x, axis=-1, keepdims=True) + EPS) * g


def _proj_kernel(x_ref, g_ref, win_ref, qn_ref, kvn_ref, kin_ref, wqidx_ref, wuq_ref, wukt_ref,
                 qidx_ref, widx_ref, qlat_ref, kidx_ref, ckv_ref):
    h = _rms(x_ref[...], g_ref[...])
    proj = jnp.dot(h.astype(BF16), win_ref[...], preferred_element_type=F32)
    o1, o2, o3 = D_CQ, D_CQ + D_CKV, D_CQ + D_CKV + IDX_DIM
    cq = _rms(proj[:, :o1], qn_ref[...])
    ckv = _rms(proj[:, o1:o2], kvn_ref[...])
    kid = _rms(proj[:, o2:o3], kin_ref[...])
    widx_ref[...] = proj[:, o3:o3 + IDX_HEADS] * (IDX_HEADS ** -0.5)
    ckv_ref[...] = ckv.astype(BF16)
    kidx_ref[...] = kid.astype(BF16)
    cqb = cq.astype(BF16)
    qidx = jnp.dot(cqb, wqidx_ref[...], preferred_element_type=F32) * (IDX_DIM ** -0.5)
    qidx_ref[...] = qidx.astype(BF16)
    qh = jnp.dot(cqb, wuq_ref[...], preferred_element_type=F32).astype(BF16)
    for hp in range(N_HEADS // 2):
        ql = jnp.dot(qh[:, hp * 2 * D_NOPE:(hp + 1) * 2 * D_NOPE], wukt_ref[hp],
                     preferred_element_type=F32) * (D_NOPE ** -0.5 * LOG2E)
        qlat_ref[:, hp * 2 * D_CKV:(hp + 1) * 2 * D_CKV] = ql.astype(BF16)


def _proj(x2, g, w_in, qn, kvn, kin, w_qidx, w_uq, w_uk, tm=512):
    n = x2.shape[0]
    ncol = w_in.shape[1]
    npad = -ncol % LANES
    win = jnp.pad(w_in, ((0, 0), (0, npad))).astype(BF16)
    wukt = jnp.transpose(w_uk, (1, 2, 0)).reshape(N_HEADS // 2, 2, D_NOPE, D_CKV)
    eye = jnp.eye(2, dtype=w_uk.dtype)
    wukt = (wukt[:, :, :, None, :] * eye[None, :, None, :, None]).reshape(
        N_HEADS // 2, 2 * D_NOPE, 2 * D_CKV).astype(BF16)
    full = lambda a: pl.BlockSpec(a.shape, lambda i: (0,) * a.ndim)
    row = lambda c: pl.BlockSpec((tm, c), lambda i: (i, 0))
    args = (x2, g.reshape(1, -1), win, qn.reshape(1, -1), kvn.reshape(1, -1), kin.reshape(1, -1),
            w_qidx.astype(BF16), w_uq.astype(BF16), wukt)
    return pl.pallas_call(
        _proj_kernel,
        grid=(n // tm,),
        in_specs=[row(D_MODEL)] + [full(a) for a in args[1:]],
        out_specs=[row(IDX_HEADS * IDX_DIM), row(IDX_HEADS), row(N_HEADS * D_CKV), row(IDX_DIM),
                   row(D_CKV)],
        out_shape=[jax.ShapeDtypeStruct((n, IDX_HEADS * IDX_DIM), BF16),
                   jax.ShapeDtypeStruct((n, IDX_HEADS), F32),
                   jax.ShapeDtypeStruct((n, N_HEADS * D_CKV), BF16),
                   jax.ShapeDtypeStruct((n, IDX_DIM), BF16),
                   jax.ShapeDtypeStruct((n, D_CKV), BF16)],
        compiler_params=_cparams(1),
        name="dsa_proj",
    )(*args)


def _attn_kernel(qidx_ref, widxt_ref, qlat_ref, kidx_ref, vaug_ref, bias_ref, tri_ref, olat_ref,
                 keys_ref, maskt_ref, a_ref, acc_ref, m_ref, s_ref, qk_ref, qs_ref, *, k_sel):
    n_rows = N_HEADS * TQ
    q0 = 2 * pl.program_id(1) * TQ

    n_tiles = q0 // SCORE_TILE + 1
    krow = lax.broadcasted_iota(I32, (SCORE_TILE, TQ), 0)
    qpos = [q0 + sub * TQ + lax.broadcasted_iota(I32, (1, TQ), 1) for sub in range(2)]
    trivial = [qp < k_sel for qp in qpos]

    def tile_rows(tt):
        return pl.ds(pl.multiple_of(tt * SCORE_TILE, SCORE_TILE), SCORE_TILE)

    def score_pass(sub):
        for h in range(IDX_HEADS):
            qs_ref[h * TQ:(h + 1) * TQ, :] = qidx_ref[sub * TQ:(sub + 1) * TQ,
                                                      h * IDX_DIM:(h + 1) * IDX_DIM]

        def score_dot(tt, slot):
            rows = tile_rows(jnp.minimum(tt, keys_ref.shape[1] // SCORE_TILE - 1))
            qk_ref[slot] = lax.dot_general(kidx_ref[rows, :], qs_ref[...], (((1,), (1,)), ((), ())),
                                           preferred_element_type=F32)

        def score_keys(tt, slot):
            sc = jnp.zeros((SCORE_TILE, TQ), F32)
            for h in range(IDX_HEADS):
                sc = sc + widxt_ref[sub, h:h + 1, :] * jnp.maximum(
                    qk_ref[slot, :, h * TQ:(h + 1) * TQ], 0.0)
            sc = jnp.where(tt * SCORE_TILE + krow <= qpos[sub], sc, -jnp.inf)
            bits = pltpu.bitcast(sc, I32)
            keys_ref[sub, tile_rows(tt), :] = bits ^ ((bits >> 31) & 0x7FFFFFFF)

        score_dot(0, 0)

        def score_pair(pp, carry):
            tt = 2 * pp
            score_dot(tt + 1, 1)
            score_keys(tt, 0)
            score_dot(tt + 2, 0)
            score_keys(tt + 1, 1)
            return carry

        lax.fori_loop(0, n_tiles // 2, score_pair, 0)

        @pl.when(n_tiles % 2 == 1)
        def _():
            score_keys(n_tiles - 1, 0)

    for sub in range(2):
        score_pass(sub)

    def tile_reduce(x, op):
        return op(x.reshape(SCORE_TILE // SUBLANES, SUBLANES, TQ), axis=0)

    def count_ge(thr):
        def body(tt, acc):
            return tuple(acc[sub] + tile_reduce(
                jnp.where(keys_ref[sub, tile_rows(tt), :] >= thr[sub], 1, 0), jnp.sum)
                for sub in range(2))
        zero = jnp.zeros((SUBLANES, TQ), I32)
        acc = lax.fori_loop(0, n_tiles, body, (zero, zero))
        return tuple(jnp.sum(a, axis=0, keepdims=True) for a in acc)

    def bit_step(step, st):
        trial = tuple(s[0] ^ (jnp.int32(1) << (31 - step)) for s in st)
        cnt = count_ge(trial)
        out = []
        for sub in range(2):
            cand, cntc, cnt_ub = st[sub]
            ok = cnt[sub] >= k_sel
            out.append((jnp.where(ok, trial[sub], cand), jnp.where(ok, cnt[sub], cntc),
                        jnp.where(ok, cnt_ub, cnt[sub])))
        return tuple(out)

    def resolve(step, st, tval, res):
        top = tuple(s[0] | ((jnp.int32(1) << (32 - step)) - 1) for s in st)

        def body(tt, c):
            out = []
            for sub in range(2):
                k = keys_ref[sub, tile_rows(tt), :]
                inb = jnp.logical_and(k >= st[sub][0], k <= top[sub])
                out.append((jnp.minimum(c[sub][0], tile_reduce(jnp.where(inb, k, 2 ** 31 - 1), jnp.min)),
                            jnp.maximum(c[sub][1], tile_reduce(jnp.where(inb, k, -2 ** 31), jnp.max))))
            return tuple(out)

        init = (jnp.full((SUBLANES, TQ), 2 ** 31 - 1, I32), jnp.full((SUBLANES, TQ), -2 ** 31, I32))
        mnmx = lax.fori_loop(0, n_tiles, body, (init, init))
        tvals, ress, n_open = [], [], 0
        for sub in range(2):
            _, cntc, cnt_ub = st[sub]
            mn = jnp.min(mnmx[sub][0], axis=0, keepdims=True)
            mx = jnp.max(mnmx[sub][1], axis=0, keepdims=True)
            known = jnp.logical_or(mn == mx, cnt_ub == k_sel - 1)
            tvals.append(jnp.where(known, mx, tval[sub]))
            ress.append(jnp.where(known, 1, res[sub]))
            settled = jnp.logical_or(jnp.logical_or(ress[sub] > 0, cntc == k_sel), trivial[sub])
            n_open = n_open + jnp.sum(jnp.where(settled, 0, 1))
        return tuple(tvals), tuple(ress), n_open

    st0 = (jnp.full((1, TQ), -2 ** 31, I32), jnp.full((1, TQ), 2 ** 30, I32), jnp.zeros((1, TQ), I32))
    st = lax.fori_loop(0, FIXED_BITS, bit_step, (st0, st0))
    zeros = jnp.zeros((1, TQ), I32)
    tval, res, n_open = resolve(FIXED_BITS, st, (st[0][0], st[1][0]), (zeros, zeros))

    def search_body(c):
        step, st, tval, res, _ = c
        for b in range(BITS_PER_TEST):
            st = bit_step(step + b, st)
        tval, res, n_open = resolve(step + BITS_PER_TEST, st, tval, res)
        return step + BITS_PER_TEST, st, tval, res, n_open

    _, st, tval, res, _ = lax.while_loop(
        lambda c: jnp.logical_and(c[0] < 32, c[4] > 0), search_body,
        (jnp.int32(FIXED_BITS), st, tval, res, n_open))

    def mask_pass(sub):
        cand, cntc, cnt_ub = st[sub]
        exact = jnp.logical_or(cntc == k_sel, trivial[sub])
        thr = jnp.where(exact, cand, jnp.where(res[sub] > 0, tval[sub], cand))
        thr = jnp.where(trivial[sub], -2 ** 31, thr)
        tie_take = jnp.where(exact, 2.0 ** 30, (k_sel - cnt_ub).astype(F32))

        def mask_tile(tt, seen):
            k = keys_ref[sub, tile_rows(tt), :]
            idx = tt * SCORE_TILE + krow
            tie = k == thr
            tie_bf = jnp.where(tie, 1.0, 0.0).astype(BF16)
            wanted = []
            for c in range(SCORE_TILE // PREFIX_ROWS):
                rank = jnp.dot(tri_ref[...], tie_bf[c * PREFIX_ROWS:(c + 1) * PREFIX_ROWS],
                               preferred_element_type=F32)
                wanted.append(rank <= tie_take - seen)
                seen = seen + rank[PREFIX_ROWS - 1:PREFIX_ROWS, :]
            sel = jnp.logical_or(k > thr, jnp.logical_and(tie, jnp.concatenate(wanted, axis=0)))
            sel = jnp.logical_and(sel, idx <= qpos[sub])
            rows = pl.ds(pl.multiple_of(PAD_KEYS + tt * SCORE_TILE, TQ), SCORE_TILE)
            maskt_ref[sub, rows, :] = jnp.where(sel, 0.0, NEG_MASK).astype(BF16)
            return seen

        seen = lax.fori_loop(0, n_tiles // 2, lambda pp, c: mask_tile(2 * pp + 1, mask_tile(2 * pp, c)),
                             jnp.zeros((1, TQ), F32))

        @pl.when(n_tiles % 2 == 1)
        def _():
            mask_tile(n_tiles - 1, seen)

        maskt_ref[sub, :PAD_KEYS, :] = jnp.full((PAD_KEYS, TQ), NEG_MASK, BF16)

    for sub in range(2):
        mask_pass(sub)

    eye = (lax.broadcasted_iota(I32, (TQ, TQ), 0) == lax.broadcasted_iota(I32, (TQ, TQ), 1))
    eye = jnp.where(eye, 1.0, 0.0).astype(BF16)
    for sub in range(2):
        for h in range(N_HEADS):
            a_ref[sub, h * TQ:(h + 1) * TQ, :D_CKV] = qlat_ref[sub * TQ:(sub + 1) * TQ,
                                                                h * D_CKV:(h + 1) * D_CKV]
            a_ref[sub, h * TQ:(h + 1) * TQ, D_CKV:] = eye
    m_ref[...] = jnp.full((2, n_rows, LANES), -jnp.inf, F32)
    acc_ref[...] = jnp.zeros((2, n_rows, 2 * D_CKV), F32)
    n_chunks = ATT_TK // LANES
    n_steps = (q0 + TQ + ATT_TK - 1) // ATT_TK

    def key_rows(sub, j):
        r0 = jnp.maximum(q0 + (sub + 1) * TQ + PAD_KEYS - (j + 1) * ATT_TK, 0)
        return pl.ds(pl.multiple_of(r0, TQ), ATT_TK)

    def logits(sub, j):
        rows = key_rows(sub, j)
        rhs = jnp.concatenate([vaug_ref[rows, :D_CKV], maskt_ref[sub, rows, :]], axis=1)
        s_ref[sub] = lax.dot_general(a_ref[sub], rhs, (((1,), (1,)), ((), ())),
                                     preferred_element_type=F32)

    def consume(sub, j):
        vk = vaug_ref[key_rows(sub, j), :]
        near = jnp.minimum(j, 1)
        ps = []
        for h in range(N_HEADS):
            rows = slice(h * TQ, (h + 1) * TQ)
            sh = s_ref[sub, rows, :]
            sh = jnp.concatenate([sh[:, :ATT_TK - BIAS_TK],
                                  sh[:, ATT_TK - BIAS_TK:] + bias_ref[near, h]], axis=1)
            m_old = m_ref[sub, rows, :]
            m_new = jnp.maximum(m_old, jnp.max(sh, axis=1, keepdims=True))
            alpha = jnp.exp2(m_old - m_new)
            m_ref[sub, rows, :] = m_new
            p = jnp.exp2(sh - jnp.concatenate([m_new] * n_chunks, axis=1))
            ps.append(p.astype(BF16))
            acc_ref[sub, rows, :] = acc_ref[sub, rows, :] * jnp.concatenate([alpha, alpha], axis=1)
        acc_ref[sub] += jnp.dot(jnp.concatenate(ps, axis=0), vk, preferred_element_type=F32)

    logits(0, 0)

    def att_step(j, carry):
        logits(1, j)
        consume(0, j)
        logits(0, j + 1)
        consume(1, j)
        return carry

    lax.fori_loop(0, n_steps, att_step, 0)

    for sub in range(2):
        for h in range(N_HEADS):
            acc = acc_ref[sub, h * TQ:(h + 1) * TQ, :]
            olat_ref[sub * TQ:(sub + 1) * TQ, h * D_CKV:(h + 1) * D_CKV] = (
                acc[:, :D_CKV] / acc[:, D_CKV:]).astype(BF16)


def _t5_bucket(dist):
    n = jnp.maximum(dist, 0)
    nf = jnp.maximum(n, 1).astype(F32)
    large = MAX_EXACT + (jnp.log(nf / MAX_EXACT) / math.log(MAX_DISTANCE / MAX_EXACT)
                         * (NUM_BUCKETS - MAX_EXACT)).astype(I32)
    large = jnp.minimum(large, NUM_BUCKETS - 1)
    return jnp.where(n < MAX_EXACT, n, large)


def _near_bias(rel_bias):
    assert BIAS_TK >= TQ + MAX_DISTANCE - 1
    period = TQ + BIAS_TK
    e = jnp.arange(period)
    e = jnp.where(e < BIAS_TK, e, e - period)
    tab = rel_bias[_t5_bucket(BIAS_TK - TQ - e)] - rel_bias[NUM_BUCKETS - 1]
    flat = jnp.tile(tab, (TQ, 1))[:TQ * (period - 1)]
    near = flat.reshape(TQ, period - 1, N_HEADS)[:, :BIAS_TK]
    return (jnp.transpose(near, (2, 0, 1)) * LOG2E).astype(F32)


def _attention(qidx, widx, qlat, kidx, ckv, rel_bias, batch, seq):
    assert seq % SCORE_TILE == 0 and SCORE_TILE % TQ == 0 and ATT_TK % (2 * TQ) == 0
    k_sel = min(TOPK_MAX, seq // 4)
    nq = seq // TQ
    r3 = lambda a: a.reshape(batch, seq, a.shape[-1])
    widxt = jnp.transpose(widx.reshape(batch, nq, TQ, IDX_HEADS), (0, 1, 3, 2))
    ckv3 = jnp.pad(r3(ckv), ((0, 0), (PAD_KEYS, 0), (0, 0)))
    vaug = jnp.concatenate([ckv3, jnp.ones_like(ckv3)], axis=-1)
    bias = _near_bias(rel_bias)
    bias = jnp.stack([bias, jnp.zeros_like(bias)])
    tri = jnp.tril(jnp.ones((PREFIX_ROWS, PREFIX_ROWS), BF16))

    qblk = lambda c: pl.BlockSpec((None, 2 * TQ, c), lambda b, i: (b, i, 0))
    qtile = lambda a: pl.BlockSpec((None, 2) + a.shape[2:], lambda b, i: (b, i, 0, 0))
    per_b = lambda a: pl.BlockSpec((None,) + a.shape[1:], lambda b, i: (b,) + (0,) * (a.ndim - 1),
                                   pipeline_mode=pl.Buffered(1))
    n_rows = N_HEADS * TQ
    kidx3 = r3(kidx)
    out = pl.pallas_call(
        functools.partial(_attn_kernel, k_sel=k_sel),
        grid=(batch, nq // 2),
        in_specs=[qblk(IDX_HEADS * IDX_DIM), qtile(widxt), qblk(N_HEADS * D_CKV), per_b(kidx3), per_b(vaug),
                  pl.BlockSpec(bias.shape, lambda b, i: (0, 0, 0, 0),
                               pipeline_mode=pl.Buffered(1)),
                  pl.BlockSpec(tri.shape, lambda b, i: (0, 0))],
        out_specs=qblk(N_HEADS * D_CKV),
        out_shape=jax.ShapeDtypeStruct((batch, seq, N_HEADS * D_CKV), BF16),
        scratch_shapes=[
            pltpu.VMEM((2, seq, TQ), I32),
            pltpu.VMEM((2, seq + PAD_KEYS, TQ), BF16),
            pltpu.VMEM((2, n_rows, 2 * D_CKV), BF16),
            pltpu.VMEM((2, n_rows, 2 * D_CKV), F32),
            pltpu.VMEM((2, n_rows, LANES), F32),
            pltpu.VMEM((2, n_rows, ATT_TK), F32),
            pltpu.VMEM((2, SCORE_TILE, IDX_HEADS * TQ), F32),
            pltpu.VMEM((IDX_HEADS * TQ, IDX_DIM), BF16),
        ],
        compiler_params=_cparams(2),
        name="dsa_attn",
    )(r3(qidx), widxt, r3(qlat), kidx3, vaug, bias, tri)
    return out.reshape(batch * seq, N_HEADS * D_CKV)


def _mlp_kernel(*refs, f_chunk, final_norm, with_attn_out):
    if with_attn_out:
        x_ref, olat_ref, wuv_ref, wo_ref, g_ref, wup_ref, wdn_ref, gf_ref, y_ref, acc_ref = refs
        pair = 2 * D_CKV
        o = [jnp.dot(olat_ref[:, p * pair:(p + 1) * pair], wuv_ref[p], preferred_element_type=F32)
             for p in range(N_HEADS // 2)]
        o = jnp.concatenate(o, axis=1).astype(BF16)
        x = x_ref[...] + jnp.dot(o, wo_ref[...], preferred_element_type=F32)
    else:
        x_ref, g_ref, wup_ref, wdn_ref, gf_ref, y_ref, acc_ref = refs
        x = x_ref[...]
    h = _rms(x, g_ref[...]).astype(BF16)
    d_ff = wup_ref.shape[1]
    for c in range(d_ff // f_chunk):
        u = jnp.dot(h, wup_ref[:, c * f_chunk:(c + 1) * f_chunk], preferred_element_type=F32)
        a = jnp.square(jnp.maximum(u, 0.0)).astype(BF16)
        d = jnp.dot(a, wdn_ref[c * f_chunk:(c + 1) * f_chunk, :], preferred_element_type=F32)
        if c == 0:
            acc_ref[...] = d
        else:
            acc_ref[...] += d
    y = x + acc_ref[...]
    if final_norm:
        y = _rms(y, gf_ref[...])
    y_ref[...] = y


def _mlp(x2, g, w_up, w_down, g_final, final_norm, attn_out=None, tm=512, f_chunk=512):
    n = x2.shape[0]
    const = lambda a: pl.BlockSpec(a.shape, lambda i: (0,) * a.ndim, pipeline_mode=pl.Buffered(1))
    rows = lambda c: pl.BlockSpec((tm, c), lambda i: (i, 0))
    args, specs = [x2], [rows(D_MODEL)]
    if attn_out is not None:
        olat, w_uv, w_o = attn_out
        wuv = jnp.transpose(w_uv, (1, 0, 2)).reshape(N_HEADS // 2, 2, D_CKV, D_V)
        eye = jnp.eye(2, dtype=w_uv.dtype)
        wbd = (wuv[:, :, :, None, :] * eye[None, :, None, :, None]).reshape(
            N_HEADS // 2, 2 * D_CKV, 2 * D_V)
        extra = [olat, wbd.astype(BF16), w_o.astype(BF16)]
        args += extra
        specs += [rows(N_HEADS * D_CKV), const(extra[1]), const(extra[2])]
    weights = [g.reshape(1, -1), w_up.astype(BF16), w_down.astype(BF16), g_final.reshape(1, -1)]
    args += weights
    specs += [const(a) for a in weights]
    return pl.pallas_call(
        functools.partial(_mlp_kernel, f_chunk=f_chunk, final_norm=final_norm,
                          with_attn_out=attn_out is not None),
        grid=(n // tm,),
        in_specs=specs,
        out_specs=rows(D_MODEL),
        out_shape=jax.ShapeDtypeStruct((n, D_MODEL), F32),
        scratch_shapes=[pltpu.VMEM((tm, D_MODEL), F32)],
        compiler_params=_cparams(1),
        name="mlp_final" if final_norm else "mlp",
    )(*args)


def _glu_kernel(x_ref, g_ref, w_ref, b_ref, u_ref):
    h = _rms(x_ref[...], g_ref[...]).astype(BF16)
    u = jnp.dot(h, w_ref[...], preferred_element_type=F32) + b_ref[...]
    d = u.shape[1] // 2
    u_ref[...] = u[:, :d] * jax.nn.sigmoid(u[:, d:])


def _glu(x2, g, w_pw1, b_pw1, tm=512):
    n = x2.shape[0]
    full = lambda a: pl.BlockSpec(a.shape, lambda i: (0,) * a.ndim)
    args = (x2, g.reshape(1, -1), w_pw1.astype(BF16), b_pw1.reshape(1, -1))
    return pl.pallas_call(
        _glu_kernel,
        grid=(n // tm,),
        in_specs=[pl.BlockSpec((tm, D_MODEL), lambda i: (i, 0))] + [full(a) for a in args[1:]],
        out_specs=pl.BlockSpec((tm, D_MODEL), lambda i: (i, 0)),
        out_shape=jax.ShapeDtypeStruct((n, D_MODEL), F32),
        compiler_params=_cparams(1),
        name="conv_glu",
    )(*args)


HALO = 32
CONV_ROWS = 128
SUBLANE_PAD = 8


def _dwconv_kernel(x_ref, u_ref, halo_ref, wdw_ref, bdw_ref, lng_ref, lnb_ref, w2_ref, b2_ref, y_ref,
                   ext_ref, cv_ref):
    i = pl.program_id(1)
    tm = u_ref.shape[0]
    ext_ref[:HALO, :] = jnp.where(i == 0, 0.0, halo_ref[...])
    ext_ref[HALO:HALO + tm, :] = u_ref[...]
    ext_ref[HALO + tm:, :] = jnp.zeros((SUBLANE_PAD, D_MODEL), F32)
    off = HALO - (CONV_WIDTH - 1)

    def rows_body(r, carry):
        r0 = pl.multiple_of(r * CONV_ROWS, CONV_ROWS)
        for lc in range(D_MODEL // LANES):
            lanes = slice(lc * LANES, (lc + 1) * LANES)
            acc = jnp.broadcast_to(bdw_ref[:, lanes], (CONV_ROWS, LANES))
            for s in range(SUBLANE_PAD):
                part = None
                for j in range(CONV_WIDTH):
                    if (off + j) % SUBLANE_PAD != s:
                        continue
                    rows = pl.ds(pl.multiple_of(r0 + (off + j - s), SUBLANE_PAD),
                                 CONV_ROWS + SUBLANE_PAD)
                    term = wdw_ref[j:j + 1, lanes] * ext_ref[rows, lanes]
                    part = term if part is None else part + term
                if part is not None:
                    acc = acc + part[s:s + CONV_ROWS]
            cv_ref[pl.ds(r0, CONV_ROWS), lanes] = acc
        return carry

    lax.fori_loop(0, tm // CONV_ROWS, rows_body, 0)
    v = cv_ref[...]
    mu = jnp.mean(v, axis=-1, keepdims=True)
    var = jnp.mean(jnp.square(v - mu), axis=-1, keepdims=True)
    v = (v - mu) * lax.rsqrt(var + EPS) * lng_ref[...] + lnb_ref[...]
    v = v * jax.nn.sigmoid(v)
    y_ref[...] = (x_ref[...] + jnp.dot(v.astype(BF16), w2_ref[...], preferred_element_type=F32)
                  + b2_ref[...])


def _dwconv(x2, u2, w_dw, b_dw, ln_g, ln_b, w_pw2, b_pw2, batch, seq, tm=512):
    x3 = x2.reshape(batch, seq, D_MODEL)
    u3 = u2.reshape(batch, seq, D_MODEL)
    hb = tm // HALO
    blk = pl.BlockSpec((None, tm, D_MODEL), lambda b, i: (b, i, 0))
    halo = pl.BlockSpec((None, HALO, D_MODEL), lambda b, i: (b, jnp.maximum(i * hb - 1, 0), 0))
    full = lambda a: pl.BlockSpec(a.shape, lambda b, i: (0,) * a.ndim)
    args = (x3, u3, u3, w_dw, b_dw.reshape(1, -1), ln_g.reshape(1, -1), ln_b.reshape(1, -1),
            w_pw2.astype(BF16), b_pw2.reshape(1, -1))
    out = pl.pallas_call(
        _dwconv_kernel,
        grid=(batch, seq // tm),
        in_specs=[blk, blk, halo] + [full(a) for a in args[3:]],
        out_specs=blk,
        out_shape=jax.ShapeDtypeStruct((batch, seq, D_MODEL), F32),
        scratch_shapes=[pltpu.VMEM((HALO + tm + SUBLANE_PAD, D_MODEL), F32),
                        pltpu.VMEM((tm, D_MODEL), F32)],
        compiler_params=_cparams(2),
        name="conv_dw",
    )(*args)
    return out.reshape(batch * seq, D_MODEL)


def kernel(x, norm_mix, norm_mlp, norm_final, rel_bias, attn_w_in, attn_q_norm, attn_kv_norm, attn_kidx_norm, attn_w_qidx, attn_w_uq, attn_w_uk, attn_w_uv, attn_w_o, conv_w_pw1, conv_b_pw1, conv_w_dw, conv_b_dw, conv_ln_g, conv_ln_b, conv_w_pw2, conv_b_pw2, mlp_w_up, mlp_w_down):
    batch, seq, d = x.shape
    depth = norm_mix.shape[0]
    x2 = x.reshape(batch * seq, d)
    for i in range(depth):
        j = i // 2
        if i % 2 == 0:
            qidx, widx, qlat, kidx, ckv = _proj(
                x2, norm_mix[i], attn_w_in[j], attn_q_norm[j], attn_kv_norm[j], attn_kidx_norm[j],
                attn_w_qidx[j], attn_w_uq[j], attn_w_uk[j])
            olat = _attention(qidx, widx, qlat, kidx, ckv, rel_bias, batch, seq)
            mixer_out = (olat, attn_w_uv[j], attn_w_o[j])
        else:
            mixer_out = None
            u = _glu(x2, norm_mix[i], conv_w_pw1[j], conv_b_pw1[j])
            x2 = _dwconv(x2, u, conv_w_dw[j], conv_b_dw[j], conv_ln_g[j], conv_ln_b[j],
                         conv_w_pw2[j], conv_b_pw2[j], batch, seq)
        last = i == depth - 1
        x2 = _mlp(x2, norm_mlp[i], mlp_w_up[i], mlp_w_down[i], norm_final, final_norm=last,
                  attn_out=mixer_out)
    if depth == 0:
        raise ValueError("depth must be positive")
    return x2.reshape(batch, seq, d)
```

```python
import functools
import math

import jax
import jax.numpy as jnp
from jax import lax
from jax.experimental import pallas as pl
from jax.experimental.pallas import tpu as pltpu

F32 = jnp.float32
BF16 = jnp.bfloat16
I32 = jnp.int32

D_MODEL = 1024
N_HEADS = 16
D_NOPE = 64
D_V = 64
D_CQ = 256
D_CKV = 128
IDX_HEADS = 8
IDX_DIM = 64
TOPK_MAX = 256
CONV_WIDTH = 31
NUM_BUCKETS = 32
MAX_EXACT = 16
MAX_DISTANCE = 128
EPS = 1e-6

LANES = 128
SUBLANES = 8
TQ = 128
SCORE_TILE = 512
ATT_TK = 512
BIAS_TK = 2 * TQ
PAD_KEYS = ATT_TK - TQ
NEG_MASK = -1e30
LOG2E = math.log2(math.e)
FIXED_BITS = 22
BITS_PER_TEST = 2
PREFIX_ROWS = 256
VMEM_LIMIT = 56 * 1024 * 1024


def _cparams(n_axes):
    return pltpu.CompilerParams(dimension_semantics=("arbitrary",) * n_axes,
                                vmem_limit_bytes=VMEM_LIMIT)


def _rms(x, g):
    return x * lax.rsqrt(jnp.mean(x * x, axis=-1, keepdims=True) + EPS) * g


def _proj_kernel(x_ref, g_ref, win_ref, qn_ref, kvn_ref, kin_ref, wqidx_ref, wuq_ref, wukt_ref,
                 qidx_ref, widx_ref, qlat_ref, kidx_ref, ckv_ref):
    h = _rms(x_ref[...], g_ref[...])
    proj = jnp.dot(h.astype(BF16), win_ref[...], preferred_element_type=F32)
    o1, o2, o3 = D_CQ, D_CQ + D_CKV, D_CQ + D_CKV + IDX_DIM
    cq = _rms(proj[:, :o1], qn_ref[...])
    ckv = _rms(proj[:, o1:o2], kvn_ref[...])
    kid = _rms(proj[:, o2:o3], kin_ref[...])
    widx_ref[...] = proj[:, o3:o3 + IDX_HEADS] * (IDX_HEADS ** -0.5)
    ckv_ref[...] = ckv.astype(BF16)
    kidx_ref[...] = kid.astype(BF16)
    cqb = cq.astype(BF16)
    qidx = jnp.dot(cqb, wqidx_ref[...], preferred_element_type=F32) * (IDX_DIM ** -0.5)
    qidx_ref[...] = qidx.astype(BF16)
    qh = jnp.dot(cqb, wuq_ref[...], preferred_element_type=F32).astype(BF16)
    for hp in range(N_HEADS // 2):
        ql = jnp.dot(qh[:, hp * 2 * D_NOPE:(hp + 1) * 2 * D_NOPE], wukt_ref[hp],
                     preferred_element_type=F32) * (D_NOPE ** -0.5 * LOG2E)
        qlat_ref[:, hp * 2 * D_CKV:(hp + 1) * 2 * D_CKV] = ql.astype(BF16)


def _proj(x2, g, w_in, qn, kvn, kin, w_qidx, w_uq, w_uk, tm=512):
    n = x2.shape[0]
    ncol = w_in.shape[1]
    npad = -ncol % LANES
    win = jnp.pad(w_in, ((0, 0), (0, npad))).astype(BF16)
    wukt = jnp.transpose(w_uk, (1, 2, 0)).reshape(N_HEADS // 2, 2, D_NOPE, D_CKV)
    eye = jnp.eye(2, dtype=w_uk.dtype)
    wukt = (wukt[:, :, :, None, :] * eye[None, :, None, :, None]).reshape(
        N_HEADS // 2, 2 * D_NOPE, 2 * D_CKV).astype(BF16)
    full = lambda a: pl.BlockSpec(a.shape, lambda i: (0,) * a.ndim)
    row = lambda c: pl.BlockSpec((tm, c), lambda i: (i, 0))
    args = (x2, g.reshape(1, -1), win, qn.reshape(1, -1), kvn.reshape(1, -1), kin.reshape(1, -1),
            w_qidx.astype(BF16), w_uq.astype(BF16), wukt)
    return pl.pallas_call(
        _proj_kernel,
        grid=(n // tm,),
        in_specs=[row(D_MODEL)] + [full(a) for a in args[1:]],
        out_specs=[row(IDX_HEADS * IDX_DIM), row(IDX_HEADS), row(N_HEADS * D_CKV), row(IDX_DIM),
                   row(D_CKV)],
        out_shape=[jax.ShapeDtypeStruct((n, IDX_HEADS * IDX_DIM), BF16),
                   jax.ShapeDtypeStruct((n, IDX_HEADS), F32),
                   jax.ShapeDtypeStruct((n, N_HEADS * D_CKV), BF16),
                   jax.ShapeDtypeStruct((n, IDX_DIM), BF16),
                   jax.ShapeDtypeStruct((n, D_CKV), BF16)],
        compiler_params=_cparams(1),
        name="dsa_proj",
    )(*args)


def _attn_kernel(qidx_ref, widxt_ref, qlat_ref, kidx_ref, vaug_ref, bias_ref, tri_ref, olat_ref,
                 keys_ref, maskt_ref, a_ref, acc_ref, m_ref, s_ref, qk_ref, qs_ref, *, k_sel):
    n_rows = N_HEADS * TQ
    q0 = 2 * pl.program_id(1) * TQ

    n_tiles = q0 // SCORE_TILE + 1
    krow = lax.broadcasted_iota(I32, (SCORE_TILE, TQ), 0)
    qpos = [q0 + sub * TQ + lax.broadcasted_iota(I32, (1, TQ), 1) for sub in range(2)]
    trivial = [qp < k_sel for qp in qpos]

    def tile_rows(tt):
        return pl.ds(pl.multiple_of(tt * SCORE_TILE, SCORE_TILE), SCORE_TILE)

    def score_pass(sub):
        for h in range(IDX_HEADS):
            qs_ref[h * TQ:(h + 1) * TQ, :] = qidx_ref[sub * TQ:(sub + 1) * TQ,
                                                      h * IDX_DIM:(h + 1) * IDX_DIM]

        def score_dot(tt, slot):
            rows = tile_rows(jnp.minimum(tt, keys_ref.shape[1] // SCORE_TILE - 1))
            qk_ref[slot] = lax.dot_general(kidx_ref[rows, :], qs_ref[...], (((1,), (1,)), ((), ())),
                                           preferred_element_type=F32)

        def score_keys(tt, slot):
            sc = jnp.zeros((SCORE_TILE, TQ), F32)
            for h in range(IDX_HEADS):
                sc = sc + widxt_ref[sub, h:h + 1, :] * jnp.maximum(
                    qk_ref[slot, :, h * TQ:(h + 1) * TQ], 0.0)
            sc = jnp.where(tt * SCORE_TILE + krow <= qpos[sub], sc, -jnp.inf)
            bits = pltpu.bitcast(sc, I32)
            keys_ref[sub, tile_rows(tt), :] = bits ^ ((bits >> 31) & 0x7FFFFFFF)

        score_dot(0, 0)

        def score_pair(pp, carry):
            tt = 2 * pp
            score_dot(tt + 1, 1)
            score_keys(tt, 0)
            score_dot(tt + 2, 0)
            score_keys(tt + 1, 1)
            return carry

        lax.fori_loop(0, n_tiles // 2, score_pair, 0)

        @pl.when(n_tiles % 2 == 1)
        def _():
            score_keys(n_tiles - 1, 0)

    for sub in range(2):
        score_pass(sub)

    def tile_reduce(x, op):
        return op(x.reshape(SCORE_TILE // SUBLANES, SUBLANES, TQ), axis=0)

    def count_ge(thr):
        def body(tt, acc):
            return tuple(acc[sub] + tile_reduce(
                jnp.where(keys_ref[sub, tile_rows(tt), :] >= thr[sub], 1, 0), jnp.sum)
                for sub in range(2))
        zero = jnp.zeros((SUBLANES, TQ), I32)
        acc = lax.fori_loop(0, n_tiles, body, (zero, zero))
        return tuple(jnp.sum(a, axis=0, keepdims=True) for a in acc)

    def bit_step(step, st):
        trial = tuple(s[0] ^ (jnp.int32(1) << (31 - step)) for s in st)
        cnt = count_ge(trial)
        out = []
        for sub in range(2):
            cand, cntc, cnt_ub = st[sub]
            ok = cnt[sub] >= k_sel
            out.append((jnp.where(ok, trial[sub], cand), jnp.where(ok, cnt[sub], cntc),
                        jnp.where(ok, cnt_ub, cnt[sub])))
        return tuple(out)

    def resolve(step, st, tval, res):
        top = tuple(s[0] | ((jnp.int32(1) << (32 - step)) - 1) for s in st)

        def body(tt, c):
            out = []
            for sub in range(2):
                k = keys_ref[sub, tile_rows(tt), :]
                inb = jnp.logical_and(k >= st[sub][0], k <= top[sub])
                out.append((jnp.minimum(c[sub][0], tile_reduce(jnp.where(inb, k, 2 ** 31 - 1), jnp.min)),
                            jnp.maximum(c[sub][1], tile_reduce(jnp.where(inb, k, -2 ** 31), jnp.max))))
            return tuple(out)

        init = (jnp.full((SUBLANES, TQ), 2 ** 31 - 1, I32), jnp.full((SUBLANES, TQ), -2 ** 31, I32))
        mnmx = lax.fori_loop(0, n_tiles, body, (init, init))
        tvals, ress, n_open = [], [], 0
        for sub in range(2):
            _, cntc, cnt_ub = st[sub]
            mn = jnp.min(mnmx[sub][0], axis=0, keepdims=True)
            mx = jnp.max(mnmx[sub][1], axis=0, keepdims=True)
            known = jnp.logical_or(mn == mx, cnt_ub == k_sel - 1)
            tvals.append(jnp.where(known, mx, tval[sub]))
            ress.append(jnp.where(known, 1, res[sub]))
            settled = jnp.logical_or(jnp.logical_or(ress[sub] > 0, cntc == k_sel), trivial[sub])
            n_open = n_open + jnp.sum(jnp.where(settled, 0, 1))
        return tuple(tvals), tuple(ress), n_open

    st0 = (jnp.full((1, TQ), -2 ** 31, I32), jnp.full((1, TQ), 2 ** 30, I32), jnp.zeros((1, TQ), I32))
    st = lax.fori_loop(0, FIXED_BITS, bit_step, (st0, st0))
    zeros = jnp.zeros((1, TQ), I32)
    tval, res, n_open = resolve(FIXED_BITS, st, (st[0][0], st[1][0]), (zeros, zeros))

    def search_body(c):
        step, st, tval, res, _ = c
        for b in range(BITS_PER_TEST):
            st = bit_step(step + b, st)
        tval, res, n_open = resolve(step + BITS_PER_TEST, st, tval, res)
        return step + BITS_PER_TEST, st, tval, res, n_open

    _, st, tval, res, _ = lax.while_loop(
        lambda c: jnp.logical_and(c[0] < 32, c[4] > 0), search_body,
        (jnp.int32(FIXED_BITS), st, tval, res, n_open))

    thr, tie_take = [], []
    for sub in range(2):
        cand, cntc, cnt_ub = st[sub]
        exact = jnp.logical_or(cntc == k_sel, trivial[sub])
        t = jnp.where(exact, cand, jnp.where(res[sub] > 0, tval[sub], cand))
        thr.append(jnp.where(trivial[sub], -2 ** 31, t))
        tie_take.append(jnp.where(exact, 2.0 ** 30, (k_sel - cnt_ub).astype(F32)))

    def mask_tile(sub, tt, seen):
        k = keys_ref[sub, tile_rows(tt), :]
        idx = tt * SCORE_TILE + krow
        tie = k == thr[sub]
        tie_bf = jnp.where(tie, 1.0, 0.0).astype(BF16)
        wanted = []
        for c in range(SCORE_TILE // PREFIX_ROWS):
            rank = jnp.dot(tri_ref[...], tie_bf[c * PREFIX_ROWS:(c + 1) * PREFIX_ROWS],
                           preferred_element_type=F32)
            wanted.append(rank <= tie_take[sub] - seen)
            seen = seen + rank[PREFIX_ROWS - 1:PREFIX_ROWS, :]
        sel = jnp.logical_or(k > thr[sub], jnp.logical_and(tie, jnp.concatenate(wanted, axis=0)))
        sel = jnp.logical_and(sel, idx <= qpos[sub])
        rows = pl.ds(pl.multiple_of(PAD_KEYS + tt * SCORE_TILE, TQ), SCORE_TILE)
        maskt_ref[sub, rows, :] = jnp.where(sel, 0.0, NEG_MASK).astype(BF16)
        return seen

    def mask_trip(pp, seen):
        return tuple(mask_tile(sub, 2 * pp + 1, mask_tile(sub, 2 * pp, seen[sub])) for sub in range(2))

    zero_f = jnp.zeros((1, TQ), F32)
    seen = lax.fori_loop(0, n_tiles // 2, mask_trip, (zero_f, zero_f))

    @pl.when(n_tiles % 2 == 1)
    def _():
        for sub in range(2):
            mask_tile(sub, n_tiles - 1, seen[sub])

    for sub in range(2):
        maskt_ref[sub, :PAD_KEYS, :] = jnp.full((PAD_KEYS, TQ), NEG_MASK, BF16)

    eye = (lax.broadcasted_iota(I32, (TQ, TQ), 0) == lax.broadcasted_iota(I32, (TQ, TQ), 1))
    eye = jnp.where(eye, 1.0, 0.0).astype(BF16)
    for sub in range(2):
        for h in range(N_HEADS):
            a_ref[sub, h * TQ:(h + 1) * TQ, :D_CKV] = qlat_ref[sub * TQ:(sub + 1) * TQ,
                                                                h * D_CKV:(h + 1) * D_CKV]
            a_ref[sub, h * TQ:(h + 1) * TQ, D_CKV:] = eye
    m_ref[...] = jnp.full((2, n_rows, LANES), -jnp.inf, F32)
    acc_ref[...] = jnp.zeros((2, n_rows, 2 * D_CKV), F32)
    n_chunks = ATT_TK // LANES
    n_steps = (q0 + TQ + ATT_TK - 1) // ATT_TK

    def key_rows(sub, j):
        r0 = jnp.maximum(q0 + (sub + 1) * TQ + PAD_KEYS - (j + 1) * ATT_TK, 0)
        return pl.ds(pl.multiple_of(r0, TQ), ATT_TK)

    def logits(sub, j):
        rows = key_rows(sub, j)
        rhs = jnp.concatenate([vaug_ref[rows, :D_CKV], maskt_ref[sub, rows, :]], axis=1)
        s_ref[sub] = lax.dot_general(a_ref[sub], rhs, (((1,), (1,)), ((), ())),
                                     preferred_element_type=F32)

    def consume(sub, j):
        vk = vaug_ref[key_rows(sub, j), :]
        near = jnp.minimum(j, 1)
        ps = []
        for h in range(N_HEADS):
            rows = slice(h * TQ, (h + 1) * TQ)
            sh = s_ref[sub, rows, :]
            sh = jnp.concatenate([sh[:, :ATT_TK - BIAS_TK],
                                  sh[:, ATT_TK - BIAS_TK:] + bias_ref[near, h]], axis=1)
            m_old = m_ref[sub, rows, :]
            m_new = jnp.maximum(m_old, jnp.max(sh, axis=1, keepdims=True))
            alpha = jnp.exp2(m_old - m_new)
            m_ref[sub, rows, :] = m_new
            p = jnp.exp2(sh - jnp.concatenate([m_new] * n_chunks, axis=1))
            ps.append(p.astype(BF16))
            acc_ref[sub, rows, :] = acc_ref[sub, rows, :] * jnp.concatenate([alpha, alpha], axis=1)
        acc_ref[sub] += jnp.dot(jnp.concatenate(ps, axis=0), vk, preferred_element_type=F32)

    logits(0, 0)

    def att_step(j, carry):
        logits(1, j)
        consume(0, j)
        logits(0, j + 1)
        consume(1, j)
        return carry

    lax.fori_loop(0, n_steps, att_step, 0)

    for sub in range(2):
        for h in range(N_HEADS):
            acc = acc_ref[sub, h * TQ:(h + 1) * TQ, :]
            olat_ref[sub * TQ:(sub + 1) * TQ, h * D_CKV:(h + 1) * D_CKV] = (
                acc[:, :D_CKV] / acc[:, D_CKV:]).astype(BF16)


def _t5_bucket(dist):
    n = jnp.maximum(dist, 0)
    nf = jnp.maximum(n, 1).astype(F32)
    large = MAX_EXACT + (jnp.log(nf / MAX_EXACT) / math.log(MAX_DISTANCE / MAX_EXACT)
                         * (NUM_BUCKETS - MAX_EXACT)).astype(I32)
    large = jnp.minimum(large, NUM_BUCKETS - 1)
    return jnp.where(n < MAX_EXACT, n, large)


def _near_bias(rel_bias):
    assert BIAS_TK >= TQ + MAX_DISTANCE - 1
    period = TQ + BIAS_TK
    e = jnp.arange(period)
    e = jnp.where(e < BIAS_TK, e, e - period)
    tab = rel_bias[_t5_bucket(BIAS_TK - TQ - e)] - rel_bias[NUM_BUCKETS - 1]
    flat = jnp.tile(tab, (TQ, 1))[:TQ * (period - 1)]
    near = flat.reshape(TQ, period - 1, N_HEADS)[:, :BIAS_TK]
    return (jnp.transpose(near, (2, 0, 1)) * LOG2E).astype(F32)


def _attention(qidx, widx, qlat, kidx, ckv, rel_bias, batch, seq):
    assert seq % SCORE_TILE == 0 and SCORE_TILE % TQ == 0 and ATT_TK % (2 * TQ) == 0
    k_sel = min(TOPK_MAX, seq // 4)
    nq = seq // TQ
    r3 = lambda a: a.reshape(batch, seq, a.shape[-1])
    widxt = jnp.transpose(widx.reshape(batch, nq, TQ, IDX_HEADS), (0, 1, 3, 2))
    ckv3 = jnp.pad(r3(ckv), ((0, 0), (PAD_KEYS, 0), (0, 0)))
    vaug = jnp.concatenate([ckv3, jnp.ones_like(ckv3)], axis=-1)
    bias = _near_bias(rel_bias)
    bias = jnp.stack([bias, jnp.zeros_like(bias)])
    tri = jnp.tril(jnp.ones((PREFIX_ROWS, PREFIX_ROWS), BF16))

    qblk = lambda c: pl.BlockSpec((None, 2 * TQ, c), lambda b, i: (b, i, 0))
    qtile = lambda a: pl.BlockSpec((None, 2) + a.shape[2:], lambda b, i: (b, i, 0, 0))
    per_b = lambda a: pl.BlockSpec((None,) + a.shape[1:], lambda b, i: (b,) + (0,) * (a.ndim - 1),
                                   pipeline_mode=pl.Buffered(1))
    n_rows = N_HEADS * TQ
    kidx3 = r3(kidx)
    out = pl.pallas_call(
        functools.partial(_attn_kernel, k_sel=k_sel),
        grid=(batch, nq // 2),
        in_specs=[qblk(IDX_HEADS * IDX_DIM), qtile(widxt), qblk(N_HEADS * D_CKV), per_b(kidx3), per_b(vaug),
                  pl.BlockSpec(bias.shape, lambda b, i: (0, 0, 0, 0),
                               pipeline_mode=pl.Buffered(1)),
                  pl.BlockSpec(tri.shape, lambda b, i: (0, 0))],
        out_specs=qblk(N_HEADS * D_CKV),
        out_shape=jax.ShapeDtypeStruct((batch, seq, N_HEADS * D_CKV), BF16),
        scratch_shapes=[
            pltpu.VMEM((2, seq, TQ), I32),
            pltpu.VMEM((2, seq + PAD_KEYS, TQ), BF16),
            pltpu.VMEM((2, n_rows, 2 * D_CKV), BF16),
            pltpu.VMEM((2, n_rows, 2 * D_CKV), F32),
            pltpu.VMEM((2, n_rows, LANES), F32),
            pltpu.VMEM((2, n_rows, ATT_TK), F32),
            pltpu.VMEM((2, SCORE_TILE, IDX_HEADS * TQ), F32),
            pltpu.VMEM((IDX_HEADS * TQ, IDX_DIM), BF16),
        ],
        compiler_params=_cparams(2),
        name="dsa_attn",
    )(r3(qidx), widxt, r3(qlat), kidx3, vaug, bias, tri)
    return out.reshape(batch * seq, N_HEADS * D_CKV)


def _mlp_kernel(*refs, f_chunk, final_norm, with_attn_out):
    if with_attn_out:
        x_ref, olat_ref, wuv_ref, wo_ref, g_ref, wup_ref, wdn_ref, gf_ref, y_ref, acc_ref = refs
        pair = 2 * D_CKV
        o = [jnp.dot(olat_ref[:, p * pair:(p + 1) * pair], wuv_ref[p], preferred_element_type=F32)
             for p in range(N_HEADS // 2)]
        o = jnp.concatenate(o, axis=1).astype(BF16)
        x = x_ref[...] + jnp.dot(o, wo_ref[...], preferred_element_type=F32)
    else:
        x_ref, g_ref, wup_ref, wdn_ref, gf_ref, y_ref, acc_ref = refs
        x = x_ref[...]
    h = _rms(x, g_ref[...]).astype(BF16)
    d_ff = wup_ref.shape[1]
    for c in range(d_ff // f_chunk):
        u = jnp.dot(h, wup_ref[:, c * f_chunk:(c + 1) * f_chunk], preferred_element_type=F32)
        a = jnp.square(jnp.maximum(u, 0.0)).astype(BF16)
        d = jnp.dot(a, wdn_ref[c * f_chunk:(c + 1) * f_chunk, :], preferred_element_type=F32)
        if c == 0:
            acc_ref[...] = d
        else:
            acc_ref[...] += d
    y = x + acc_ref[...]
    if final_norm:
        y = _rms(y, gf_ref[...])
    y_ref[...] = y


def _mlp(x2, g, w_up, w_down, g_final, final_norm, attn_out=None, tm=512, f_chunk=512):
    n = x2.shape[0]
    const = lambda a: pl.BlockSpec(a.shape, lambda i: (0,) * a.ndim, pipeline_mode=pl.Buffered(1))
    rows = lambda c: pl.BlockSpec((tm, c), lambda i: (i, 0))
    args, specs = [x2], [rows(D_MODEL)]
    if attn_out is not None:
        olat, w_uv, w_o = attn_out
        wuv = jnp.transpose(w_uv, (1, 0, 2)).reshape(N_HEADS // 2, 2, D_CKV, D_V)
        eye = jnp.eye(2, dtype=w_uv.dtype)
        wbd = (wuv[:, :, :, None, :] * eye[None, :, None, :, None]).reshape(
            N_HEADS // 2, 2 * D_CKV, 2 * D_V)
        extra = [olat, wbd.astype(BF16), w_o.astype(BF16)]
        args += extra
        specs += [rows(N_HEADS * D_CKV), const(extra[1]), const(extra[2])]
    weights = [g.reshape(1, -1), w_up.astype(BF16), w_down.astype(BF16), g_final.reshape(1, -1)]
    args += weights
    specs += [const(a) for a in weights]
    return pl.pallas_call(
        functools.partial(_mlp_kernel, f_chunk=f_chunk, final_norm=final_norm,
                          with_attn_out=attn_out is not None),
        grid=(n // tm,),
        in_specs=specs,
        out_specs=rows(D_MODEL),
        out_shape=jax.ShapeDtypeStruct((n, D_MODEL), F32),
        scratch_shapes=[pltpu.VMEM((tm, D_MODEL), F32)],
        compiler_params=_cparams(1),
        name="mlp_final" if final_norm else "mlp",
    )(*args)


def _glu_kernel(x_ref, g_ref, w_ref, b_ref, u_ref):
    h = _rms(x_ref[...], g_ref[...]).astype(BF16)
    u = jnp.dot(h, w_ref[...], preferred_element_type=F32) + b_ref[...]
    d = u.shape[1] // 2
    u_ref[...] = u[:, :d] * jax.nn.sigmoid(u[:, d:])


def _glu(x2, g, w_pw1, b_pw1, tm=512):
    n = x2.shape[0]
    full = lambda a: pl.BlockSpec(a.shape, lambda i: (0,) * a.ndim)
    args = (x2, g.reshape(1, -1), w_pw1.astype(BF16), b_pw1.reshape(1, -1))
    return pl.pallas_call(
        _glu_kernel,
        grid=(n // tm,),
        in_specs=[pl.BlockSpec((tm, D_MODEL), lambda i: (i, 0))] + [full(a) for a in args[1:]],
        out_specs=pl.BlockSpec((tm, D_MODEL), lambda i: (i, 0)),
        out_shape=jax.ShapeDtypeStruct((n, D_MODEL), F32),
        compiler_params=_cparams(1),
        name="conv_glu",
    )(*args)


HALO = 32
CONV_ROWS = 128
SUBLANE_PAD = 8


def _dwconv_kernel(x_ref, u_ref, halo_ref, wdw_ref, bdw_ref, lng_ref, lnb_ref, w2_ref, b2_ref, y_ref,
                   ext_ref, cv_ref):
    i = pl.program_id(1)
    tm = u_ref.shape[0]
    ext_ref[:HALO, :] = jnp.where(i == 0, 0.0, halo_ref[...])
    ext_ref[HALO:HALO + tm, :] = u_ref[...]
    ext_ref[HALO + tm:, :] = jnp.zeros((SUBLANE_PAD, D_MODEL), F32)
    off = HALO - (CONV_WIDTH - 1)

    def rows_body(r, carry):
        r0 = pl.multiple_of(r * CONV_ROWS, CONV_ROWS)
        for lc in range(D_MODEL // LANES):
            lanes = slice(lc * LANES, (lc + 1) * LANES)
            acc = jnp.broadcast_to(bdw_ref[:, lanes], (CONV_ROWS, LANES))
            for s in range(SUBLANE_PAD):
                part = None
                for j in range(CONV_WIDTH):
                    if (off + j) % SUBLANE_PAD != s:
                        continue
                    rows = pl.ds(pl.multiple_of(r0 + (off + j - s), SUBLANE_PAD),
                                 CONV_ROWS + SUBLANE_PAD)
                    term = wdw_ref[j:j + 1, lanes] * ext_ref[rows, lanes]
                    part = term if part is None else part + term
                if part is not None:
                    acc = acc + part[s:s + CONV_ROWS]
            cv_ref[pl.ds(r0, CONV_ROWS), lanes] = acc
        return carry

    lax.fori_loop(0, tm // CONV_ROWS, rows_body, 0)
    v = cv_ref[...]
    mu = jnp.mean(v, axis=-1, keepdims=True)
    var = jnp.mean(jnp.square(v - mu), axis=-1, keepdims=True)
    v = (v - mu) * lax.rsqrt(var + EPS) * lng_ref[...] + lnb_ref[...]
    v = v * jax.nn.sigmoid(v)
    y_ref[...] = (x_ref[...] + jnp.dot(v.astype(BF16), w2_ref[...], preferred_element_type=F32)
                  + b2_ref[...])


def _dwconv(x2, u2, w_dw, b_dw, ln_g, ln_b, w_pw2, b_pw2, batch, seq, tm=512):
    x3 = x2.reshape(batch, seq, D_MODEL)
    u3 = u2.reshape(batch, seq, D_MODEL)
    hb = tm // HALO
    blk = pl.BlockSpec((None, tm, D_MODEL), lambda b, i: (b, i, 0))
    halo = pl.BlockSpec((None, HALO, D_MODEL), lambda b, i: (b, jnp.maximum(i * hb - 1, 0), 0))
    full = lambda a: pl.BlockSpec(a.shape, lambda b, i: (0,) * a.ndim)
    args = (x3, u3, u3, w_dw, b_dw.reshape(1, -1), ln_g.reshape(1, -1), ln_b.reshape(1, -1),
            w_pw2.astype(BF16), b_pw2.reshape(1, -1))
    out = pl.pallas_call(
        _dwconv_kernel,
        grid=(batch, seq // tm),
        in_specs=[blk, blk, halo] + [full(a) for a in args[3:]],
        out_specs=blk,
        out_shape=jax.ShapeDtypeStruct((batch, seq, D_MODEL), F32),
        scratch_shapes=[pltpu.VMEM((HALO + tm + SUBLANE_PAD, D_MODEL), F32),
                        pltpu.VMEM((tm, D_MODEL), F32)],
        compiler_params=_cparams(2),
        name="conv_dw",
    )(*args)
    return out.reshape(batch * seq, D_MODEL)


def kernel(x, norm_mix, norm_mlp, norm_final, rel_bias, attn_w_in, attn_q_norm, attn_kv_norm, attn_kidx_norm, attn_w_qidx, attn_w_uq, attn_w_uk, attn_w_uv, attn_w_o, conv_w_pw1, conv_b_pw1, conv_w_dw, conv_b_dw, conv_ln_g, conv_ln_b, conv_w_pw2, conv_b_pw2, mlp_w_up, mlp_w_down):
    batch, seq, d = x.shape
    depth = norm_mix.shape[0]
    x2 = x.reshape(batch * seq, d)
    for i in range(depth):
        j = i // 2
        if i % 2 == 0:
            qidx, widx, qlat, kidx, ckv = _proj(
                x2, norm_mix[i], attn_w_in[j], attn_q_norm[j], attn_kv_norm[j], attn_kidx_norm[j],
                attn_w_qidx[j], attn_w_uq[j], attn_w_uk[j])
            olat = _attention(qidx, widx, qlat, kidx, ckv, rel_bias, batch, seq)
            mixer_out = (olat, attn_w_uv[j], attn_w_o[j])
        else:
            mixer_out = None
            u = _glu(x2, norm_mix[i], conv_w_pw1[j], conv_b_pw1[j])
            x2 = _dwconv(x2, u, conv_w_dw[j], conv_b_dw[j], conv_ln_g[j], conv_ln_b[j],
                         conv_w_pw2[j], conv_b_pw2[j], batch, seq)
        last = i == depth - 1
        x2 = _mlp(x2, norm_mlp[i], mlp_w_up[i], mlp_w_down[i], norm_final, final_norm=last,
                  attn_out=mixer_out)
    if depth == 0:
        raise ValueError("depth must be positive")
    return x2.reshape(batch, seq, d)
```

```python
import functools
import math

import jax
import jax.numpy as jnp
from jax import lax
from jax.experimental import pallas as pl
from jax.experimental.pallas import tpu as pltpu

F32 = jnp.float32
BF16 = jnp.bfloat16
I32 = jnp.int32

D_MODEL = 1024
N_HEADS = 16
D_NOPE = 64
D_V = 64
D_CQ = 256
D_CKV = 128
IDX_HEADS = 8
IDX_DIM = 64
TOPK_MAX = 256
CONV_WIDTH = 31
NUM_BUCKETS = 32
MAX_EXACT = 16
MAX_DISTANCE = 128
EPS = 1e-6

LANES = 128
SUBLANES = 8
TQ = 128
SCORE_TILE = 512
ATT_TK = 512
BIAS_TK = 2 * TQ
PAD_KEYS = ATT_TK - TQ
NEG_MASK = -1e30
LOG2E = math.log2(math.e)
FIXED_BITS = 24
BITS_PER_TEST = 2
PREFIX_ROWS = 256
VMEM_LIMIT = 56 * 1024 * 1024


def _cparams(n_axes):
    return pltpu.CompilerParams(dimension_semantics=("arbitrary",) * n_axes,
                                vmem_limit_bytes=VMEM_LIMIT)


def _rms(x, g):
    return x * lax.rsqrt(jnp.mean(x * x, axis=-1, keepdims=True) + EPS) * g


def _proj_kernel(x_ref, g_ref, win_ref, qn_ref, kvn_ref, kin_ref, wqidx_ref, wuq_ref, wukt_ref,
                 qidx_ref, widx_ref, qlat_ref, kidx_ref, ckv_ref):
    h = _rms(x_ref[...], g_ref[...])
    proj = jnp.dot(h.astype(BF16), win_ref[...], preferred_element_type=F32)
    o1, o2, o3 = D_CQ, D_CQ + D_CKV, D_CQ + D_CKV + IDX_DIM
    cq = _rms(proj[:, :o1], qn_ref[...])
    ckv = _rms(proj[:, o1:o2], kvn_ref[...])
    kid = _rms(proj[:, o2:o3], kin_ref[...])
    widx_ref[...] = proj[:, o3:o3 + IDX_HEADS] * (IDX_HEADS ** -0.5)
    ckv_ref[...] = ckv.astype(BF16)
    kidx_ref[...] = kid.astype(BF16)
    cqb = cq.astype(BF16)
    qidx = jnp.dot(cqb, wqidx_ref[...], preferred_element_type=F32) * (IDX_DIM ** -0.5)
    qidx_ref[...] = qidx.astype(BF16)
    qh = jnp.dot(cqb, wuq_ref[...], preferred_element_type=F32).astype(BF16)
    for hp in range(N_HEADS // 2):
        ql = jnp.dot(qh[:, hp * 2 * D_NOPE:(hp + 1) * 2 * D_NOPE], wukt_ref[hp],
                     preferred_element_type=F32) * (D_NOPE ** -0.5 * LOG2E)
        qlat_ref[:, hp * 2 * D_CKV:(hp + 1) * 2 * D_CKV] = ql.astype(BF16)


def _proj(x2, g, w_in, qn, kvn, kin, w_qidx, w_uq, w_uk, tm=512):
    n = x2.shape[0]
    ncol = w_in.shape[1]
    npad = -ncol % LANES
    win = jnp.pad(w_in, ((0, 0), (0, npad))).astype(BF16)
    wukt = jnp.transpose(w_uk, (1, 2, 0)).reshape(N_HEADS // 2, 2, D_NOPE, D_CKV)
    eye = jnp.eye(2, dtype=w_uk.dtype)
    wukt = (wukt[:, :, :, None, :] * eye[None, :, None, :, None]).reshape(
        N_HEADS // 2, 2 * D_NOPE, 2 * D_CKV).astype(BF16)
    full = lambda a: pl.BlockSpec(a.shape, lambda i: (0,) * a.ndim)
    row = lambda c: pl.BlockSpec((tm, c), lambda i: (i, 0))
    args = (x2, g.reshape(1, -1), win, qn.reshape(1, -1), kvn.reshape(1, -1), kin.reshape(1, -1),
            w_qidx.astype(BF16), w_uq.astype(BF16), wukt)
    return pl.pallas_call(
        _proj_kernel,
        grid=(n // tm,),
        in_specs=[row(D_MODEL)] + [full(a) for a in args[1:]],
        out_specs=[row(IDX_HEADS * IDX_DIM), row(IDX_HEADS), row(N_HEADS * D_CKV), row(IDX_DIM),
                   row(D_CKV)],
        out_shape=[jax.ShapeDtypeStruct((n, IDX_HEADS * IDX_DIM), BF16),
                   jax.ShapeDtypeStruct((n, IDX_HEADS), F32),
                   jax.ShapeDtypeStruct((n, N_HEADS * D_CKV), BF16),
                   jax.ShapeDtypeStruct((n, IDX_DIM), BF16),
                   jax.ShapeDtypeStruct((n, D_CKV), BF16)],
        compiler_params=_cparams(1),
        name="dsa_proj",
    )(*args)


def _attn_kernel(qidx_ref, widxt_ref, qlat_ref, kidx_ref, vaug_ref, bias_ref, tri_ref, olat_ref,
                 keys_ref, maskt_ref, a_ref, acc_ref, m_ref, s_ref, qk_ref, qs_ref, *, k_sel):
    n_rows = N_HEADS * TQ
    q0 = 2 * pl.program_id(1) * TQ

    n_tiles = q0 // SCORE_TILE + 1
    krow = lax.broadcasted_iota(I32, (SCORE_TILE, TQ), 0)
    qpos = [q0 + sub * TQ + lax.broadcasted_iota(I32, (1, TQ), 1) for sub in range(2)]
    trivial = [qp < k_sel for qp in qpos]

    def tile_rows(tt):
        return pl.ds(pl.multiple_of(tt * SCORE_TILE, SCORE_TILE), SCORE_TILE)

    def score_pass(sub):
        for h in range(IDX_HEADS):
            qs_ref[h * TQ:(h + 1) * TQ, :] = qidx_ref[sub * TQ:(sub + 1) * TQ,
                                                      h * IDX_DIM:(h + 1) * IDX_DIM]

        def score_dot(tt, slot):
            rows = tile_rows(jnp.minimum(tt, keys_ref.shape[1] // SCORE_TILE - 1))
            qk_ref[slot] = lax.dot_general(kidx_ref[rows, :], qs_ref[...], (((1,), (1,)), ((), ())),
                                           preferred_element_type=F32)

        def score_keys(tt, slot):
            sc = jnp.zeros((SCORE_TILE, TQ), F32)
            for h in range(IDX_HEADS):
                sc = sc + widxt_ref[sub, h:h + 1, :] * jnp.maximum(
                    qk_ref[slot, :, h * TQ:(h + 1) * TQ], 0.0)
            sc = jnp.where(tt * SCORE_TILE + krow <= qpos[sub], sc, -jnp.inf)
            bits = pltpu.bitcast(sc, I32)
            keys_ref[sub, tile_rows(tt), :] = bits ^ ((bits >> 31) & 0x7FFFFFFF)

        score_dot(0, 0)

        def score_pair(pp, carry):
            tt = 2 * pp
            score_dot(tt + 1, 1)
            score_keys(tt, 0)
            score_dot(tt + 2, 0)
            score_keys(tt + 1, 1)
            return carry

        lax.fori_loop(0, n_tiles // 2, score_pair, 0)

        @pl.when(n_tiles % 2 == 1)
        def _():
            score_keys(n_tiles - 1, 0)

    for sub in range(2):
        score_pass(sub)

    def tile_reduce(x, op):
        return op(x.reshape(SCORE_TILE // SUBLANES, SUBLANES, TQ), axis=0)

    def count_ge(thr):
        def body(tt, acc):
            return tuple(acc[sub] + tile_reduce(
                jnp.where(keys_ref[sub, tile_rows(tt), :] >= thr[sub], 1, 0), jnp.sum)
                for sub in range(2))
        zero = jnp.zeros((SUBLANES, TQ), I32)
        acc = lax.fori_loop(0, n_tiles, body, (zero, zero))
        return tuple(jnp.sum(a, axis=0, keepdims=True) for a in acc)

    def bit_step(step, st):
        trial = tuple(s[0] ^ (jnp.int32(1) << (31 - step)) for s in st)
        cnt = count_ge(trial)
        out = []
        for sub in range(2):
            cand, cntc, cnt_ub = st[sub]
            ok = cnt[sub] >= k_sel
            out.append((jnp.where(ok, trial[sub], cand), jnp.where(ok, cnt[sub], cntc),
                        jnp.where(ok, cnt_ub, cnt[sub])))
        return tuple(out)

    def resolve(step, st, tval, res):
        top = tuple(s[0] | ((jnp.int32(1) << (32 - step)) - 1) for s in st)

        def body(tt, c):
            out = []
            for sub in range(2):
                k = keys_ref[sub, tile_rows(tt), :]
                inb = jnp.logical_and(k >= st[sub][0], k <= top[sub])
                out.append((jnp.minimum(c[sub][0], tile_reduce(jnp.where(inb, k, 2 ** 31 - 1), jnp.min)),
                            jnp.maximum(c[sub][1], tile_reduce(jnp.where(inb, k, -2 ** 31), jnp.max))))
            return tuple(out)

        init = (jnp.full((SUBLANES, TQ), 2 ** 31 - 1, I32), jnp.full((SUBLANES, TQ), -2 ** 31, I32))
        mnmx = lax.fori_loop(0, n_tiles, body, (init, init))
        tvals, ress, n_open = [], [], 0
        for sub in range(2):
            _, cntc, cnt_ub = st[sub]
            mn = jnp.min(mnmx[sub][0], axis=0, keepdims=True)
            mx = jnp.max(mnmx[sub][1], axis=0, keepdims=True)
            known = jnp.logical_or(mn == mx, cnt_ub == k_sel - 1)
            tvals.append(jnp.where(known, mx, tval[sub]))
            ress.append(jnp.where(known, 1, res[sub]))
            settled = jnp.logical_or(jnp.logical_or(ress[sub] > 0, cntc == k_sel), trivial[sub])
            n_open = n_open + jnp.sum(jnp.where(settled, 0, 1))
        return tuple(tvals), tuple(ress), n_open

    st0 = (jnp.full((1, TQ), -2 ** 31, I32), jnp.full((1, TQ), 2 ** 30, I32), jnp.zeros((1, TQ), I32))
    st = lax.fori_loop(0, FIXED_BITS, bit_step, (st0, st0))
    zeros = jnp.zeros((1, TQ), I32)
    tval, res, n_open = resolve(FIXED_BITS, st, (st[0][0], st[1][0]), (zeros, zeros))

    def search_body(c):
        step, st, tval, res, _ = c
        for b in range(BITS_PER_TEST):
            st = bit_step(step + b, st)
        tval, res, n_open = resolve(step + BITS_PER_TEST, st, tval, res)
        return step + BITS_PER_TEST, st, tval, res, n_open

    _, st, tval, res, _ = lax.while_loop(
        lambda c: jnp.logical_and(c[0] < 32, c[4] > 0), search_body,
        (jnp.int32(FIXED_BITS), st, tval, res, n_open))

    thr, tie_take = [], []
    for sub in range(2):
        cand, cntc, cnt_ub = st[sub]
        exact = jnp.logical_or(cntc == k_sel, trivial[sub])
        t = jnp.where(exact, cand, jnp.where(res[sub] > 0, tval[sub], cand))
        thr.append(jnp.where(trivial[sub], -2 ** 31, t))
        tie_take.append(jnp.where(exact, 2.0 ** 30, (k_sel - cnt_ub).astype(F32)))

    def mask_tile(sub, tt, seen):
        k = keys_ref[sub, tile_rows(tt), :]
        idx = tt * SCORE_TILE + krow
        tie = k == thr[sub]
        tie_bf = jnp.where(tie, 1.0, 0.0).astype(BF16)
        wanted = []
        for c in range(SCORE_TILE // PREFIX_ROWS):
            rank = jnp.dot(tri_ref[...], tie_bf[c * PREFIX_ROWS:(c + 1) * PREFIX_ROWS],
                           preferred_element_type=F32)
            wanted.append(rank <= tie_take[sub] - seen)
            seen = seen + rank[PREFIX_ROWS - 1:PREFIX_ROWS, :]
        sel = jnp.logical_or(k > thr[sub], jnp.logical_and(tie, jnp.concatenate(wanted, axis=0)))
        sel = jnp.logical_and(sel, idx <= qpos[sub])
        rows = pl.ds(pl.multiple_of(PAD_KEYS + tt * SCORE_TILE, TQ), SCORE_TILE)
        maskt_ref[sub, rows, :] = jnp.where(sel, 0.0, NEG_MASK).astype(BF16)
        return seen

    def mask_trip(pp, seen):
        return tuple(mask_tile(sub, 2 * pp + 1, mask_tile(sub, 2 * pp, seen[sub])) for sub in range(2))

    zero_f = jnp.zeros((1, TQ), F32)
    seen = lax.fori_loop(0, n_tiles // 2, mask_trip, (zero_f, zero_f))

    @pl.when(n_tiles % 2 == 1)
    def _():
        for sub in range(2):
            mask_tile(sub, n_tiles - 1, seen[sub])

    for sub in range(2):
        maskt_ref[sub, :PAD_KEYS, :] = jnp.full((PAD_KEYS, TQ), NEG_MASK, BF16)

    eye = (lax.broadcasted_iota(I32, (TQ, TQ), 0) == lax.broadcasted_iota(I32, (TQ, TQ), 1))
    eye = jnp.where(eye, 1.0, 0.0).astype(BF16)
    for sub in range(2):
        for h in range(N_HEADS):
            a_ref[sub, h * TQ:(h + 1) * TQ, :D_CKV] = qlat_ref[sub * TQ:(sub + 1) * TQ,
                                                                h * D_CKV:(h + 1) * D_CKV]
            a_ref[sub, h * TQ:(h + 1) * TQ, D_CKV:] = eye
    m_ref[...] = jnp.full((2, n_rows, LANES), -jnp.inf, F32)
    acc_ref[...] = jnp.zeros((2, n_rows, 2 * D_CKV), F32)
    n_chunks = ATT_TK // LANES
    n_steps = (q0 + TQ + ATT_TK - 1) // ATT_TK

    def key_rows(sub, j):
        r0 = jnp.maximum(q0 + (sub + 1) * TQ + PAD_KEYS - (j + 1) * ATT_TK, 0)
        return pl.ds(pl.multiple_of(r0, TQ), ATT_TK)

    def logits(sub, j):
        rows = key_rows(sub, j)
        rhs = jnp.concatenate([vaug_ref[rows, :D_CKV], maskt_ref[sub, rows, :]], axis=1)
        s_ref[sub] = lax.dot_general(a_ref[sub], rhs, (((1,), (1,)), ((), ())),
                                     preferred_element_type=F32)

    def consume(sub, j):
        vk = vaug_ref[key_rows(sub, j), :]
        near = jnp.minimum(j, 1)
        ps = []
        for h in range(N_HEADS):
            rows = slice(h * TQ, (h + 1) * TQ)
            sh = s_ref[sub, rows, :]
            sh = jnp.concatenate([sh[:, :ATT_TK - BIAS_TK],
                                  sh[:, ATT_TK - BIAS_TK:] + bias_ref[near, h]], axis=1)
            m_old = m_ref[sub, rows, :]
            m_new = jnp.maximum(m_old, jnp.max(sh, axis=1, keepdims=True))
            alpha = jnp.exp2(m_old - m_new)
            m_ref[sub, rows, :] = m_new
            p = jnp.exp2(sh - jnp.concatenate([m_new] * n_chunks, axis=1))
            ps.append(p.astype(BF16))
            acc_ref[sub, rows, :] = acc_ref[sub, rows, :] * jnp.concatenate([alpha, alpha], axis=1)
        acc_ref[sub] += jnp.dot(jnp.concatenate(ps, axis=0), vk, preferred_element_type=F32)

    logits(0, 0)

    def att_step(j, carry):
        logits(1, j)
        consume(0, j)
        logits(0, j + 1)
        consume(1, j)
        return carry

    lax.fori_loop(0, n_steps, att_step, 0)

    for sub in range(2):
        for h in range(N_HEADS):
            acc = acc_ref[sub, h * TQ:(h + 1) * TQ, :]
            olat_ref[sub * TQ:(sub + 1) * TQ, h * D_CKV:(h + 1) * D_CKV] = (
                acc[:, :D_CKV] / acc[:, D_CKV:]).astype(BF16)


def _t5_bucket(dist):
    n = jnp.maximum(dist, 0)
    nf = jnp.maximum(n, 1).astype(F32)
    large = MAX_EXACT + (jnp.log(nf / MAX_EXACT) / math.log(MAX_DISTANCE / MAX_EXACT)
                         * (NUM_BUCKETS - MAX_EXACT)).astype(I32)
    large = jnp.minimum(large, NUM_BUCKETS - 1)
    return jnp.where(n < MAX_EXACT, n, large)


def _near_bias(rel_bias):
    assert BIAS_TK >= TQ + MAX_DISTANCE - 1
    period = TQ + BIAS_TK
    e = jnp.arange(period)
    e = jnp.where(e < BIAS_TK, e, e - period)
    tab = rel_bias[_t5_bucket(BIAS_TK - TQ - e)] - rel_bias[NUM_BUCKETS - 1]
    flat = jnp.tile(tab, (TQ, 1))[:TQ * (period - 1)]
    near = flat.reshape(TQ, period - 1, N_HEADS)[:, :BIAS_TK]
    return (jnp.transpose(near, (2, 0, 1)) * LOG2E).astype(F32)


def _attention(qidx, widx, qlat, kidx, ckv, rel_bias, batch, seq):
    assert seq % SCORE_TILE == 0 and SCORE_TILE % TQ == 0 and ATT_TK % (2 * TQ) == 0
    k_sel = min(TOPK_MAX, seq // 4)
    nq = seq // TQ
    r3 = lambda a: a.reshape(batch, seq, a.shape[-1])
    widxt = jnp.transpose(widx.reshape(batch, nq, TQ, IDX_HEADS), (0, 1, 3, 2))
    ckv3 = jnp.pad(r3(ckv), ((0, 0), (PAD_KEYS, 0), (0, 0)))
    vaug = jnp.concatenate([ckv3, jnp.ones_like(ckv3)], axis=-1)
    bias = _near_bias(rel_bias)
    bias = jnp.stack([bias, jnp.zeros_like(bias)])
    tri = jnp.tril(jnp.ones((PREFIX_ROWS, PREFIX_ROWS), BF16))

    qblk = lambda c: pl.BlockSpec((None, 2 * TQ, c), lambda b, i: (b, i, 0))
    qtile = lambda a: pl.BlockSpec((None, 2) + a.shape[2:], lambda b, i: (b, i, 0, 0))
    per_b = lambda a: pl.BlockSpec((None,) + a.shape[1:], lambda b, i: (b,) + (0,) * (a.ndim - 1),
                                   pipeline_mode=pl.Buffered(1))
    n_rows = N_HEADS * TQ
    kidx3 = r3(kidx)
    out = pl.pallas_call(
        functools.partial(_attn_kernel, k_sel=k_sel),
        grid=(batch, nq // 2),
        in_specs=[qblk(IDX_HEADS * IDX_DIM), qtile(widxt), qblk(N_HEADS * D_CKV), per_b(kidx3), per_b(vaug),
                  pl.BlockSpec(bias.shape, lambda b, i: (0, 0, 0, 0),
                               pipeline_mode=pl.Buffered(1)),
                  pl.BlockSpec(tri.shape, lambda b, i: (0, 0))],
        out_specs=qblk(N_HEADS * D_CKV),
        out_shape=jax.ShapeDtypeStruct((batch, seq, N_HEADS * D_CKV), BF16),
        scratch_shapes=[
            pltpu.VMEM((2, seq, TQ), I32),
            pltpu.VMEM((2, seq + PAD_KEYS, TQ), BF16),
            pltpu.VMEM((2, n_rows, 2 * D_CKV), BF16),
            pltpu.VMEM((2, n_rows, 2 * D_CKV), F32),
            pltpu.VMEM((2, n_rows, LANES), F32),
            pltpu.VMEM((2, n_rows, ATT_TK), F32),
            pltpu.VMEM((2, SCORE_TILE, IDX_HEADS * TQ), F32),
            pltpu.VMEM((IDX_HEADS * TQ, IDX_DIM), BF16),
        ],
        compiler_params=_cparams(2),
        name="dsa_attn",
    )(r3(qidx), widxt, r3(qlat), kidx3, vaug, bias, tri)
    return out.reshape(batch * seq, N_HEADS * D_CKV)


def _mlp_kernel(*refs, f_chunk, final_norm, with_attn_out):
    if with_attn_out:
        x_ref, olat_ref, wuv_ref, wo_ref, g_ref, wup_ref, wdn_ref, gf_ref, y_ref, acc_ref = refs
        pair = 2 * D_CKV
        o = [jnp.dot(olat_ref[:, p * pair:(p + 1) * pair], wuv_ref[p], preferred_element_type=F32)
             for p in range(N_HEADS // 2)]
        o = jnp.concatenate(o, axis=1).astype(BF16)
        x = x_ref[...] + jnp.dot(o, wo_ref[...], preferred_element_type=F32)
    else:
        x_ref, g_ref, wup_ref, wdn_ref, gf_ref, y_ref, acc_ref = refs
        x = x_ref[...]
    h = _rms(x, g_ref[...]).astype(BF16)
    d_ff = wup_ref.shape[1]
    for c in range(d_ff // f_chunk):
        u = jnp.dot(h, wup_ref[:, c * f_chunk:(c + 1) * f_chunk], preferred_element_type=F32)
        a = jnp.square(jnp.maximum(u, 0.0)).astype(BF16)
        d = jnp.dot(a, wdn_ref[c * f_chunk:(c + 1) * f_chunk, :], preferred_element_type=F32)
        if c == 0:
            acc_ref[...] = d
        else:
            acc_ref[...] += d
    y = x + acc_ref[...]
    if final_norm:
        y = _rms(y, gf_ref[...])
    y_ref[...] = y


def _mlp(x2, g, w_up, w_down, g_final, final_norm, attn_out=None, tm=512, f_chunk=512):
    n = x2.shape[0]
    const = lambda a: pl.BlockSpec(a.shape, lambda i: (0,) * a.ndim, pipeline_mode=pl.Buffered(1))
    rows = lambda c: pl.BlockSpec((tm, c), lambda i: (i, 0))
    args, specs = [x2], [rows(D_MODEL)]
    if attn_out is not None:
        olat, w_uv, w_o = attn_out
        wuv = jnp.transpose(w_uv, (1, 0, 2)).reshape(N_HEADS // 2, 2, D_CKV, D_V)
        eye = jnp.eye(2, dtype=w_uv.dtype)
        wbd = (wuv[:, :, :, None, :] * eye[None, :, None, :, None]).reshape(
            N_HEADS // 2, 2 * D_CKV, 2 * D_V)
        extra = [olat, wbd.astype(BF16), w_o.astype(BF16)]
        args += extra
        specs += [rows(N_HEADS * D_CKV), const(extra[1]), const(extra[2])]
    weights = [g.reshape(1, -1), w_up.astype(BF16), w_down.astype(BF16), g_final.reshape(1, -1)]
    args += weights
    specs += [const(a) for a in weights]
    return pl.pallas_call(
        functools.partial(_mlp_kernel, f_chunk=f_chunk, final_norm=final_norm,
                          with_attn_out=attn_out is not None),
        grid=(n // tm,),
        in_specs=specs,
        out_specs=rows(D_MODEL),
        out_shape=jax.ShapeDtypeStruct((n, D_MODEL), F32),
        scratch_shapes=[pltpu.VMEM((tm, D_MODEL), F32)],
        compiler_params=_cparams(1),
        name="mlp_final" if final_norm else "mlp",
    )(*args)


def _glu_kernel(x_ref, g_ref, w_ref, b_ref, u_ref):
    h = _rms(x_ref[...], g_ref[...]).astype(BF16)
    u = jnp.dot(h, w_ref[...], preferred_element_type=F32) + b_ref[...]
    d = u.shape[1] // 2
    u_ref[...] = u[:, :d] * jax.nn.sigmoid(u[:, d:])


def _glu(x2, g, w_pw1, b_pw1, tm=512):
    n = x2.shape[0]
    full = lambda a: pl.BlockSpec(a.shape, lambda i: (0,) * a.ndim)
    args = (x2, g.reshape(1, -1), w_pw1.astype(BF16), b_pw1.reshape(1, -1))
    return pl.pallas_call(
        _glu_kernel,
        grid=(n // tm,),
        in_specs=[pl.BlockSpec((tm, D_MODEL), lambda i: (i, 0))] + [full(a) for a in args[1:]],
        out_specs=pl.BlockSpec((tm, D_MODEL), lambda i: (i, 0)),
        out_shape=jax.ShapeDtypeStruct((n, D_MODEL), F32),
        compiler_params=_cparams(1),
        name="conv_glu",
    )(*args)


HALO = 32
CONV_ROWS = 128
SUBLANE_PAD = 8


def _dwconv_kernel(x_ref, u_ref, halo_ref, wdw_ref, bdw_ref, lng_ref, lnb_ref, w2_ref, b2_ref, y_ref,
                   ext_ref, cv_ref):
    i = pl.program_id(1)
    tm = u_ref.shape[0]
    ext_ref[:HALO, :] = jnp.where(i == 0, 0.0, halo_ref[...])
    ext_ref[HALO:HALO + tm, :] = u_ref[...]
    ext_ref[HALO + tm:, :] = jnp.zeros((SUBLANE_PAD, D_MODEL), F32)
    off = HALO - (CONV_WIDTH - 1)

    def rows_body(r, carry):
        r0 = pl.multiple_of(r * CONV_ROWS, CONV_ROWS)
        for lc in range(D_MODEL // LANES):
            lanes = slice(lc * LANES, (lc + 1) * LANES)
            acc = jnp.broadcast_to(bdw_ref[:, lanes], (CONV_ROWS, LANES))
            for s in range(SUBLANE_PAD):
                part = None
                for j in range(CONV_WIDTH):
                    if (off + j) % SUBLANE_PAD != s:
                        continue
                    rows = pl.ds(pl.multiple_of(r0 + (off + j - s), SUBLANE_PAD),
                                 CONV_ROWS + SUBLANE_PAD)
                    term = wdw_ref[j:j + 1, lanes] * ext_ref[rows, lanes]
                    part = term if part is None else part + term
                if part is not None:
                    acc = acc + part[s:s + CONV_ROWS]
            cv_ref[pl.ds(r0, CONV_ROWS), lanes] = acc
        return carry

    lax.fori_loop(0, tm // CONV_ROWS, rows_body, 0)
    v = cv_ref[...]
    mu = jnp.mean(v, axis=-1, keepdims=True)
    var = jnp.mean(jnp.square(v - mu), axis=-1, keepdims=True)
    v = (v - mu) * lax.rsqrt(var + EPS) * lng_ref[...] + lnb_ref[...]
    v = v * jax.nn.sigmoid(v)
    y_ref[...] = (x_ref[...] + jnp.dot(v.astype(BF16), w2_ref[...], preferred_element_type=F32)
                  + b2_ref[...])


def _dwconv(x2, u2, w_dw, b_dw, ln_g, ln_b, w_pw2, b_pw2, batch, seq, tm=512):
    x3 = x2.reshape(batch, seq, D_MODEL)
    u3 = u2.reshape(batch, seq, D_MODEL)
    hb = tm // HALO
    blk = pl.BlockSpec((None, tm, D_MODEL), lambda b, i: (b, i, 0))
    halo = pl.BlockSpec((None, HALO, D_MODEL), lambda b, i: (b, jnp.maximum(i * hb - 1, 0), 0))
    full = lambda a: pl.BlockSpec(a.shape, lambda b, i: (0,) * a.ndim)
    args = (x3, u3, u3, w_dw, b_dw.reshape(1, -1), ln_g.reshape(1, -1), ln_b.reshape(1, -1),
            w_pw2.astype(BF16), b_pw2.reshape(1, -1))
    out = pl.pallas_call(
        _dwconv_kernel,
        grid=(batch, seq // tm),
        in_specs=[blk, blk, halo] + [full(a) for a in args[3:]],
        out_specs=blk,
        out_shape=jax.ShapeDtypeStruct((batch, seq, D_MODEL), F32),
        scratch_shapes=[pltpu.VMEM((HALO + tm + SUBLANE_PAD, D_MODEL), F32),
                        pltpu.VMEM((tm, D_MODEL), F32)],
        compiler_params=_cparams(2),
        name="conv_dw",
    )(*args)
    return out.reshape(batch * seq, D_MODEL)


def kernel(x, norm_mix, norm_mlp, norm_final, rel_bias, attn_w_in, attn_q_norm, attn_kv_norm, attn_kidx_norm, attn_w_qidx, attn_w_uq, attn_w_uk, attn_w_uv, attn_w_o, conv_w_pw1, conv_b_pw1, conv_w_dw, conv_b_dw, conv_ln_g, conv_ln_b, conv_w_pw2, conv_b_pw2, mlp_w_up, mlp_w_down):
    batch, seq, d = x.shape
    depth = norm_mix.shape[0]
    x2 = x.reshape(batch * seq, d)
    for i in range(depth):
        j = i // 2
        if i % 2 == 0:
            qidx, widx, qlat, kidx, ckv = _proj(
                x2, norm_mix[i], attn_w_in[j], attn_q_norm[j], attn_kv_norm[j], attn_kidx_norm[j],
                attn_w_qidx[j], attn_w_uq[j], attn_w_uk[j])
            olat = _attention(qidx, widx, qlat, kidx, ckv, rel_bias, batch, seq)
            mixer_out = (olat, attn_w_uv[j], attn_w_o[j])
        else:
            mixer_out = None
            u = _glu(x2, norm_mix[i], conv_w_pw1[j], conv_b_pw1[j])
            x2 = _dwconv(x2, u, conv_w_dw[j], conv_b_dw[j], conv_ln_g[j], conv_ln_b[j],
                         conv_w_pw2[j], conv_b_pw2[j], batch, seq)
        last = i == depth - 1
        x2 = _mlp(x2, norm_mlp[i], mlp_w_up[i], mlp_w_down[i], norm_final, final_norm=last,
                  attn_out=mixer_out)
    if depth == 0:
        raise ValueError("depth must be positive")
    return x2.reshape(batch, seq, d)
```

```python
import functools
import math

import jax
import jax.numpy as jnp
from jax import lax
from jax.experimental import pallas as pl
from jax.experimental.pallas import tpu as pltpu

F32 = jnp.float32
BF16 = jnp.bfloat16
I32 = jnp.int32

D_MODEL = 1024
N_HEADS = 16
D_NOPE = 64
D_V = 64
D_CQ = 256
D_CKV = 128
IDX_HEADS = 8
IDX_DIM = 64
TOPK_MAX = 256
CONV_WIDTH = 31
NUM_BUCKETS = 32
MAX_EXACT = 16
MAX_DISTANCE = 128
EPS = 1e-6

LANES = 128
SUBLANES = 8
TQ = 128
SCORE_TILE = 512
ATT_TK = 512
BIAS_TK = 2 * TQ
PAD_KEYS = ATT_TK - TQ
NEG_MASK = -1e30
LOG2E = math.log2(math.e)
FIXED_BITS = 22
BITS_PER_TEST = 2
PREFIX_ROWS = 256
VMEM_LIMIT = 56 * 1024 * 1024


def _cparams(n_axes):
    return pltpu.CompilerParams(dimension_semantics=("arbitrary",) * n_axes,
                                vmem_limit_bytes=VMEM_LIMIT)


def _rms(x, g):
    return x * lax.rsqrt(jnp.mean(x * x, axis=-1, keepdims=True) + EPS) * g


def _proj_kernel(x_ref, g_ref, win_ref, qn_ref, kvn_ref, kin_ref, wqidx_ref, wuq_ref, wukt_ref,
                 qidx_ref, widx_ref, qlat_ref, kidx_ref, ckv_ref):
    h = _rms(x_ref[...], g_ref[...])
    proj = jnp.dot(h.astype(BF16), win_ref[...], preferred_element_type=F32)
    o1, o2, o3 = D_CQ, D_CQ + D_CKV, D_CQ + D_CKV + IDX_DIM
    cq = _rms(proj[:, :o1], qn_ref[...])
    ckv = _rms(proj[:, o1:o2], kvn_ref[...])
    kid = _rms(proj[:, o2:o3], kin_ref[...])
    widx_ref[...] = proj[:, o3:o3 + IDX_HEADS] * (IDX_HEADS ** -0.5)
    ckv_ref[...] = ckv.astype(BF16)
    kidx_ref[...] = kid.astype(BF16)
    cqb = cq.astype(BF16)
    qidx = jnp.dot(cqb, wqidx_ref[...], preferred_element_type=F32) * (IDX_DIM ** -0.5)
    qidx_ref[...] = qidx.astype(BF16)
    qh = jnp.dot(cqb, wuq_ref[...], preferred_element_type=F32).astype(BF16)
    for hp in range(N_HEADS // 2):
        ql = jnp.dot(qh[:, hp * 2 * D_NOPE:(hp + 1) * 2 * D_NOPE], wukt_ref[hp],
                     preferred_element_type=F32) * (D_NOPE ** -0.5 * LOG2E)
        qlat_ref[:, hp * 2 * D_CKV:(hp + 1) * 2 * D_CKV] = ql.astype(BF16)


def _proj(x2, g, w_in, qn, kvn, kin, w_qidx, w_uq, w_uk, tm=1024):
    n = x2.shape[0]
    ncol = w_in.shape[1]
    npad = -ncol % LANES
    win = jnp.pad(w_in, ((0, 0), (0, npad))).astype(BF16)
    wukt = jnp.transpose(w_uk, (1, 2, 0)).reshape(N_HEADS // 2, 2, D_NOPE, D_CKV)
    eye = jnp.eye(2, dtype=w_uk.dtype)
    wukt = (wukt[:, :, :, None, :] * eye[None, :, None, :, None]).reshape(
        N_HEADS // 2, 2 * D_NOPE, 2 * D_CKV).astype(BF16)
    full = lambda a: pl.BlockSpec(a.shape, lambda i: (0,) * a.ndim)
    row = lambda c: pl.BlockSpec((tm, c), lambda i: (i, 0))
    args = (x2, g.reshape(1, -1), win, qn.reshape(1, -1), kvn.reshape(1, -1), kin.reshape(1, -1),
            w_qidx.astype(BF16), w_uq.astype(BF16), wukt)
    return pl.pallas_call(
        _proj_kernel,
        grid=(n // tm,),
        in_specs=[row(D_MODEL)] + [full(a) for a in args[1:]],
        out_specs=[row(IDX_HEADS * IDX_DIM), row(IDX_HEADS), row(N_HEADS * D_CKV), row(IDX_DIM),
                   row(D_CKV)],
        out_shape=[jax.ShapeDtypeStruct((n, IDX_HEADS * IDX_DIM), BF16),
                   jax.ShapeDtypeStruct((n, IDX_HEADS), F32),
                   jax.ShapeDtypeStruct((n, N_HEADS * D_CKV), BF16),
                   jax.ShapeDtypeStruct((n, IDX_DIM), BF16),
                   jax.ShapeDtypeStruct((n, D_CKV), BF16)],
        compiler_params=_cparams(1),
        name="dsa_proj",
    )(*args)


def _attn_kernel(qidx_ref, widxt_ref, qlat_ref, kidx_ref, vaug_ref, bias_ref, tri_ref, olat_ref,
                 keys_ref, maskt_ref, a_ref, acc_ref, m_ref, s_ref, qk_ref, qs_ref, *, k_sel):
    n_rows = N_HEADS * TQ
    q0 = 2 * pl.program_id(1) * TQ

    n_tiles = q0 // SCORE_TILE + 1
    krow = lax.broadcasted_iota(I32, (SCORE_TILE, TQ), 0)
    qpos = [q0 + sub * TQ + lax.broadcasted_iota(I32, (1, TQ), 1) for sub in range(2)]
    trivial = [qp < k_sel for qp in qpos]

    def tile_rows(tt):
        return pl.ds(pl.multiple_of(tt * SCORE_TILE, SCORE_TILE), SCORE_TILE)

    def score_pass(sub):
        for h in range(IDX_HEADS):
            qs_ref[h * TQ:(h + 1) * TQ, :] = qidx_ref[sub * TQ:(sub + 1) * TQ,
                                                      h * IDX_DIM:(h + 1) * IDX_DIM]

        def score_dot(tt, slot):
            rows = tile_rows(jnp.minimum(tt, keys_ref.shape[1] // SCORE_TILE - 1))
            qk_ref[slot] = lax.dot_general(kidx_ref[rows, :], qs_ref[...], (((1,), (1,)), ((), ())),
                                           preferred_element_type=F32)

        def score_keys(tt, slot):
            sc = jnp.zeros((SCORE_TILE, TQ), F32)
            for h in range(IDX_HEADS):
                sc = sc + widxt_ref[sub, h:h + 1, :] * jnp.maximum(
                    qk_ref[slot, :, h * TQ:(h + 1) * TQ], 0.0)
            sc = jnp.where(tt * SCORE_TILE + krow <= qpos[sub], sc, -jnp.inf)
            bits = pltpu.bitcast(sc, I32)
            keys_ref[sub, tile_rows(tt), :] = bits ^ ((bits >> 31) & 0x7FFFFFFF)

        score_dot(0, 0)

        def score_pair(pp, carry):
            tt = 2 * pp
            score_dot(tt + 1, 1)
            score_keys(tt, 0)
            score_dot(tt + 2, 0)
            score_keys(tt + 1, 1)
            return carry

        lax.fori_loop(0, n_tiles // 2, score_pair, 0)

        @pl.when(n_tiles % 2 == 1)
        def _():
            score_keys(n_tiles - 1, 0)

    for sub in range(2):
        score_pass(sub)

    def tile_reduce(x, op):
        return op(x.reshape(SCORE_TILE // SUBLANES, SUBLANES, TQ), axis=0)

    def count_ge(thr):
        def body(tt, acc):
            return tuple(acc[sub] + tile_reduce(
                jnp.where(keys_ref[sub, tile_rows(tt), :] >= thr[sub], 1, 0), jnp.sum)
                for sub in range(2))
        zero = jnp.zeros((SUBLANES, TQ), I32)
        acc = lax.fori_loop(0, n_tiles, body, (zero, zero))
        return tuple(jnp.sum(a, axis=0, keepdims=True) for a in acc)

    def bit_step(step, st):
        trial = tuple(s[0] ^ (jnp.int32(1) << (31 - step)) for s in st)
        cnt = count_ge(trial)
        out = []
        for sub in range(2):
            cand, cntc, cnt_ub = st[sub]
            ok = cnt[sub] >= k_sel
            out.append((jnp.where(ok, trial[sub], cand), jnp.where(ok, cnt[sub], cntc),
                        jnp.where(ok, cnt_ub, cnt[sub])))
        return tuple(out)

    def resolve(step, st, tval, res):
        top = tuple(s[0] | ((jnp.int32(1) << (32 - step)) - 1) for s in st)

        def body(tt, c):
            out = []
            for sub in range(2):
                k = keys_ref[sub, tile_rows(tt), :]
                inb = jnp.logical_and(k >= st[sub][0], k <= top[sub])
                out.append((jnp.minimum(c[sub][0], tile_reduce(jnp.where(inb, k, 2 ** 31 - 1), jnp.min)),
                            jnp.maximum(c[sub][1], tile_reduce(jnp.where(inb, k, -2 ** 31), jnp.max))))
            return tuple(out)

        init = (jnp.full((SUBLANES, TQ), 2 ** 31 - 1, I32), jnp.full((SUBLANES, TQ), -2 ** 31, I32))
        mnmx = lax.fori_loop(0, n_tiles, body, (init, init))
        tvals, ress, n_open = [], [], 0
        for sub in range(2):
            _, cntc, cnt_ub = st[sub]
            mn = jnp.min(mnmx[sub][0], axis=0, keepdims=True)
            mx = jnp.max(mnmx[sub][1], axis=0, keepdims=True)
            known = jnp.logical_or(mn == mx, cnt_ub == k_sel - 1)
            tvals.append(jnp.where(known, mx, tval[sub]))
            ress.append(jnp.where(known, 1, res[sub]))
            settled = jnp.logical_or(jnp.logical_or(ress[sub] > 0, cntc == k_sel), trivial[sub])
            n_open = n_open + jnp.sum(jnp.where(settled, 0, 1))
        return tuple(tvals), tuple(ress), n_open

    st0 = (jnp.full((1, TQ), -2 ** 31, I32), jnp.full((1, TQ), 2 ** 30, I32), jnp.zeros((1, TQ), I32))
    st = lax.fori_loop(0, FIXED_BITS, bit_step, (st0, st0))
    zeros = jnp.zeros((1, TQ), I32)
    tval, res, n_open = resolve(FIXED_BITS, st, (st[0][0], st[1][0]), (zeros, zeros))

    def search_body(c):
        step, st, tval, res, _ = c
        for b in range(BITS_PER_TEST):
            st = bit_step(step + b, st)
        tval, res, n_open = resolve(step + BITS_PER_TEST, st, tval, res)
        return step + BITS_PER_TEST, st, tval, res, n_open

    _, st, tval, res, _ = lax.while_loop(
        lambda c: jnp.logical_and(c[0] < 32, c[4] > 0), search_body,
        (jnp.int32(FIXED_BITS), st, tval, res, n_open))

    thr, tie_take = [], []
    for sub in range(2):
        cand, cntc, cnt_ub = st[sub]
        exact = jnp.logical_or(cntc == k_sel, trivial[sub])
        t = jnp.where(exact, cand, jnp.where(res[sub] > 0, tval[sub], cand))
        thr.append(jnp.where(trivial[sub], -2 ** 31, t))
        tie_take.append(jnp.where(exact, 2.0 ** 30, (k_sel - cnt_ub).astype(F32)))

    def mask_tile(sub, tt, seen):
        k = keys_ref[sub, tile_rows(tt), :]
        idx = tt * SCORE_TILE + krow
        tie = k == thr[sub]
        tie_bf = jnp.where(tie, 1.0, 0.0).astype(BF16)
        wanted = []
        for c in range(SCORE_TILE // PREFIX_ROWS):
            rank = jnp.dot(tri_ref[...], tie_bf[c * PREFIX_ROWS:(c + 1) * PREFIX_ROWS],
                           preferred_element_type=F32)
            wanted.append(rank <= tie_take[sub] - seen)
            seen = seen + rank[PREFIX_ROWS - 1:PREFIX_ROWS, :]
        sel = jnp.logical_or(k > thr[sub], jnp.logical_and(tie, jnp.concatenate(wanted, axis=0)))
        sel = jnp.logical_and(sel, idx <= qpos[sub])
        rows = pl.ds(pl.multiple_of(PAD_KEYS + tt * SCORE_TILE, TQ), SCORE_TILE)
        maskt_ref[sub, rows, :] = jnp.where(sel, 0.0, NEG_MASK).astype(BF16)
        return seen

    def mask_trip(pp, seen):
        return tuple(mask_tile(sub, 2 * pp + 1, mask_tile(sub, 2 * pp, seen[sub])) for sub in range(2))

    zero_f = jnp.zeros((1, TQ), F32)
    seen = lax.fori_loop(0, n_tiles // 2, mask_trip, (zero_f, zero_f))

    @pl.when(n_tiles % 2 == 1)
    def _():
        for sub in range(2):
            mask_tile(sub, n_tiles - 1, seen[sub])

    for sub in range(2):
        maskt_ref[sub, :PAD_KEYS, :] = jnp.full((PAD_KEYS, TQ), NEG_MASK, BF16)

    eye = (lax.broadcasted_iota(I32, (TQ, TQ), 0) == lax.broadcasted_iota(I32, (TQ, TQ), 1))
    eye = jnp.where(eye, 1.0, 0.0).astype(BF16)
    for sub in range(2):
        for h in range(N_HEADS):
            a_ref[sub, h * TQ:(h + 1) * TQ, :D_CKV] = qlat_ref[sub * TQ:(sub + 1) * TQ,
                                                                h * D_CKV:(h + 1) * D_CKV]
            a_ref[sub, h * TQ:(h + 1) * TQ, D_CKV:] = eye
    m_ref[...] = jnp.full((2, n_rows, LANES), -jnp.inf, F32)
    acc_ref[...] = jnp.zeros((2, n_rows, 2 * D_CKV), F32)
    n_chunks = ATT_TK // LANES
    n_steps = (q0 + TQ + ATT_TK - 1) // ATT_TK

    def key_rows(sub, j):
        r0 = jnp.maximum(q0 + (sub + 1) * TQ + PAD_KEYS - (j + 1) * ATT_TK, 0)
        return pl.ds(pl.multiple_of(r0, TQ), ATT_TK)

    def logits(sub, j):
        rows = key_rows(sub, j)
        rhs = jnp.concatenate([vaug_ref[rows, :D_CKV], maskt_ref[sub, rows, :]], axis=1)
        s_ref[sub] = lax.dot_general(a_ref[sub], rhs, (((1,), (1,)), ((), ())),
                                     preferred_element_type=F32)

    def consume(sub, j):
        vk = vaug_ref[key_rows(sub, j), :]
        near = jnp.minimum(j, 1)
        ps = []
        for h in range(N_HEADS):
            rows = slice(h * TQ, (h + 1) * TQ)
            sh = s_ref[sub, rows, :]
            sh = jnp.concatenate([sh[:, :ATT_TK - BIAS_TK],
                                  sh[:, ATT_TK - BIAS_TK:] + bias_ref[near, h]], axis=1)
            m_old = m_ref[sub, rows, :]
            m_new = jnp.maximum(m_old, jnp.max(sh, axis=1, keepdims=True))
            alpha = jnp.exp2(m_old - m_new)
            m_ref[sub, rows, :] = m_new
            p = jnp.exp2(sh - jnp.concatenate([m_new] * n_chunks, axis=1))
            ps.append(p.astype(BF16))
            acc_ref[sub, rows, :] = acc_ref[sub, rows, :] * jnp.concatenate([alpha, alpha], axis=1)
        acc_ref[sub] += jnp.dot(jnp.concatenate(ps, axis=0), vk, preferred_element_type=F32)

    logits(0, 0)

    def att_step(j, carry):
        logits(1, j)
        consume(0, j)
        logits(0, j + 1)
        consume(1, j)
        return carry

    lax.fori_loop(0, n_steps, att_step, 0)

    for sub in range(2):
        for h in range(N_HEADS):
            acc = acc_ref[sub, h * TQ:(h + 1) * TQ, :]
            olat_ref[sub * TQ:(sub + 1) * TQ, h * D_CKV:(h + 1) * D_CKV] = (
                acc[:, :D_CKV] / acc[:, D_CKV:]).astype(BF16)


def _t5_bucket(dist):
    n = jnp.maximum(dist, 0)
    nf = jnp.maximum(n, 1).astype(F32)
    large = MAX_EXACT + (jnp.log(nf / MAX_EXACT) / math.log(MAX_DISTANCE / MAX_EXACT)
                         * (NUM_BUCKETS - MAX_EXACT)).astype(I32)
    large = jnp.minimum(large, NUM_BUCKETS - 1)
    return jnp.where(n < MAX_EXACT, n, large)


def _near_bias(rel_bias):
    assert BIAS_TK >= TQ + MAX_DISTANCE - 1
    period = TQ + BIAS_TK
    e = jnp.arange(period)
    e = jnp.where(e < BIAS_TK, e, e - period)
    tab = rel_bias[_t5_bucket(BIAS_TK - TQ - e)] - rel_bias[NUM_BUCKETS - 1]
    flat = jnp.tile(tab, (TQ, 1))[:TQ * (period - 1)]
    near = flat.reshape(TQ, period - 1, N_HEADS)[:, :BIAS_TK]
    return (jnp.transpose(near, (2, 0, 1)) * LOG2E).astype(F32)


def _attention(qidx, widx, qlat, kidx, ckv, rel_bias, batch, seq):
    assert seq % SCORE_TILE == 0 and SCORE_TILE % TQ == 0 and ATT_TK % (2 * TQ) == 0
    k_sel = min(TOPK_MAX, seq // 4)
    nq = seq // TQ
    r3 = lambda a: a.reshape(batch, seq, a.shape[-1])
    widxt = jnp.transpose(widx.reshape(batch, nq, TQ, IDX_HEADS), (0, 1, 3, 2))
    ckv3 = jnp.pad(r3(ckv), ((0, 0), (PAD_KEYS, 0), (0, 0)))
    vaug = jnp.concatenate([ckv3, jnp.ones_like(ckv3)], axis=-1)
    bias = _near_bias(rel_bias)
    bias = jnp.stack([bias, jnp.zeros_like(bias)])
    tri = jnp.tril(jnp.ones((PREFIX_ROWS, PREFIX_ROWS), BF16))

    qblk = lambda c: pl.BlockSpec((None, 2 * TQ, c), lambda b, i: (b, i, 0))
    qtile = lambda a: pl.BlockSpec((None, 2) + a.shape[2:], lambda b, i: (b, i, 0, 0))
    per_b = lambda a: pl.BlockSpec((None,) + a.shape[1:], lambda b, i: (b,) + (0,) * (a.ndim - 1),
                                   pipeline_mode=pl.Buffered(1))
    n_rows = N_HEADS * TQ
    kidx3 = r3(kidx)
    out = pl.pallas_call(
        functools.partial(_attn_kernel, k_sel=k_sel),
        grid=(batch, nq // 2),
        in_specs=[qblk(IDX_HEADS * IDX_DIM), qtile(widxt), qblk(N_HEADS * D_CKV), per_b(kidx3), per_b(vaug),
                  pl.BlockSpec(bias.shape, lambda b, i: (0, 0, 0, 0),
                               pipeline_mode=pl.Buffered(1)),
                  pl.BlockSpec(tri.shape, lambda b, i: (0, 0))],
        out_specs=qblk(N_HEADS * D_CKV),
        out_shape=jax.ShapeDtypeStruct((batch, seq, N_HEADS * D_CKV), BF16),
        scratch_shapes=[
            pltpu.VMEM((2, seq, TQ), I32),
            pltpu.VMEM((2, seq + PAD_KEYS, TQ), BF16),
            pltpu.VMEM((2, n_rows, 2 * D_CKV), BF16),
            pltpu.VMEM((2, n_rows, 2 * D_CKV), F32),
            pltpu.VMEM((2, n_rows, LANES), F32),
            pltpu.VMEM((2, n_rows, ATT_TK), F32),
            pltpu.VMEM((2, SCORE_TILE, IDX_HEADS * TQ), F32),
            pltpu.VMEM((IDX_HEADS * TQ, IDX_DIM), BF16),
        ],
        compiler_params=_cparams(2),
        name="dsa_attn",
    )(r3(qidx), widxt, r3(qlat), kidx3, vaug, bias, tri)
    return out.reshape(batch * seq, N_HEADS * D_CKV)


def _mlp_kernel(*refs, f_chunk, final_norm, with_attn_out):
    if with_attn_out:
        x_ref, olat_ref, wuv_ref, wo_ref, g_ref, wup_ref, wdn_ref, gf_ref, y_ref, acc_ref = refs
        pair = 2 * D_CKV
        o = [jnp.dot(olat_ref[:, p * pair:(p + 1) * pair], wuv_ref[p], preferred_element_type=F32)
             for p in range(N_HEADS // 2)]
        o = jnp.concatenate(o, axis=1).astype(BF16)
        x = x_ref[...] + jnp.dot(o, wo_ref[...], preferred_element_type=F32)
    else:
        x_ref, g_ref, wup_ref, wdn_ref, gf_ref, y_ref, acc_ref = refs
        x = x_ref[...]
    h = _rms(x, g_ref[...]).astype(BF16)
    d_ff = wup_ref.shape[1]
    for c in range(d_ff // f_chunk):
        u = jnp.dot(h, wup_ref[:, c * f_chunk:(c + 1) * f_chunk], preferred_element_type=F32)
        a = jnp.square(jnp.maximum(u, 0.0)).astype(BF16)
        d = jnp.dot(a, wdn_ref[c * f_chunk:(c + 1) * f_chunk, :], preferred_element_type=F32)
        if c == 0:
            acc_ref[...] = d
        else:
            acc_ref[...] += d
    y = x + acc_ref[...]
    if final_norm:
        y = _rms(y, gf_ref[...])
    y_ref[...] = y


def _mlp(x2, g, w_up, w_down, g_final, final_norm, attn_out=None, tm=512, f_chunk=512):
    n = x2.shape[0]
    const = lambda a: pl.BlockSpec(a.shape, lambda i: (0,) * a.ndim, pipeline_mode=pl.Buffered(1))
    rows = lambda c: pl.BlockSpec((tm, c), lambda i: (i, 0))
    args, specs = [x2], [rows(D_MODEL)]
    if attn_out is not None:
        olat, w_uv, w_o = attn_out
        wuv = jnp.transpose(w_uv, (1, 0, 2)).reshape(N_HEADS // 2, 2, D_CKV, D_V)
        eye = jnp.eye(2, dtype=w_uv.dtype)
        wbd = (wuv[:, :, :, None, :] * eye[None, :, None, :, None]).reshape(
            N_HEADS // 2, 2 * D_CKV, 2 * D_V)
        extra = [olat, wbd.astype(BF16), w_o.astype(BF16)]
        args += extra
        specs += [rows(N_HEADS * D_CKV), const(extra[1]), const(extra[2])]
    weights = [g.reshape(1, -1), w_up.astype(BF16), w_down.astype(BF16), g_final.reshape(1, -1)]
    args += weights
    specs += [const(a) for a in weights]
    return pl.pallas_call(
        functools.partial(_mlp_kernel, f_chunk=f_chunk, final_norm=final_norm,
                          with_attn_out=attn_out is not None),
        grid=(n // tm,),
        in_specs=specs,
        out_specs=rows(D_MODEL),
        out_shape=jax.ShapeDtypeStruct((n, D_MODEL), F32),
        scratch_shapes=[pltpu.VMEM((tm, D_MODEL), F32)],
        compiler_params=_cparams(1),
        name="mlp_final" if final_norm else "mlp",
    )(*args)


def _glu_kernel(x_ref, g_ref, w_ref, b_ref, u_ref):
    h = _rms(x_ref[...], g_ref[...]).astype(BF16)
    u = jnp.dot(h, w_ref[...], preferred_element_type=F32) + b_ref[...]
    d = u.shape[1] // 2
    u_ref[...] = u[:, :d] * jax.nn.sigmoid(u[:, d:])


def _glu(x2, g, w_pw1, b_pw1, tm=1024):
    n = x2.shape[0]
    full = lambda a: pl.BlockSpec(a.shape, lambda i: (0,) * a.ndim)
    args = (x2, g.reshape(1, -1), w_pw1.astype(BF16), b_pw1.reshape(1, -1))
    return pl.pallas_call(
        _glu_kernel,
        grid=(n // tm,),
        in_specs=[pl.BlockSpec((tm, D_MODEL), lambda i: (i, 0))] + [full(a) for a in args[1:]],
        out_specs=pl.BlockSpec((tm, D_MODEL), lambda i: (i, 0)),
        out_shape=jax.ShapeDtypeStruct((n, D_MODEL), F32),
        compiler_params=_cparams(1),
        name="conv_glu",
    )(*args)


HALO = 32
CONV_ROWS = 128
SUBLANE_PAD = 8


def _dwconv_kernel(x_ref, u_ref, halo_ref, wdw_ref, bdw_ref, lng_ref, lnb_ref, w2_ref, b2_ref, y_ref,
                   ext_ref, cv_ref):
    i = pl.program_id(1)
    tm = u_ref.shape[0]
    ext_ref[:HALO, :] = jnp.where(i == 0, 0.0, halo_ref[...])
    ext_ref[HALO:HALO + tm, :] = u_ref[...]
    ext_ref[HALO + tm:, :] = jnp.zeros((SUBLANE_PAD, D_MODEL), F32)
    off = HALO - (CONV_WIDTH - 1)

    def rows_body(r, carry):
        r0 = pl.multiple_of(r * CONV_ROWS, CONV_ROWS)
        for lc in range(D_MODEL // LANES):
            lanes = slice(lc * LANES, (lc + 1) * LANES)
            acc = jnp.broadcast_to(bdw_ref[:, lanes], (CONV_ROWS, LANES))
            for s in range(SUBLANE_PAD):
                part = None
                for j in range(CONV_WIDTH):
                    if (off + j) % SUBLANE_PAD != s:
                        continue
                    rows = pl.ds(pl.multiple_of(r0 + (off + j - s), SUBLANE_PAD),
                                 CONV_ROWS + SUBLANE_PAD)
                    term = wdw_ref[j:j + 1, lanes] * ext_ref[rows, lanes]
                    part = term if part is None else part + term
                if part is not None:
                    acc = acc + part[s:s + CONV_ROWS]
            cv_ref[pl.ds(r0, CONV_ROWS), lanes] = acc
        return carry

    lax.fori_loop(0, tm // CONV_ROWS, rows_body, 0)
    v = cv_ref[...]
    mu = jnp.mean(v, axis=-1, keepdims=True)
    var = jnp.mean(jnp.square(v - mu), axis=-1, keepdims=True)
    v = (v - mu) * lax.rsqrt(var + EPS) * lng_ref[...] + lnb_ref[...]
    v = v * jax.nn.sigmoid(v)
    y_ref[...] = (x_ref[...] + jnp.dot(v.astype(BF16), w2_ref[...], preferred_element_type=F32)
                  + b2_ref[...])


def _dwconv(x2, u2, w_dw, b_dw, ln_g, ln_b, w_pw2, b_pw2, batch, seq, tm=512):
    x3 = x2.reshape(batch, seq, D_MODEL)
    u3 = u2.reshape(batch, seq, D_MODEL)
    hb = tm // HALO
    blk = pl.BlockSpec((None, tm, D_MODEL), lambda b, i: (b, i, 0))
    halo = pl.BlockSpec((None, HALO, D_MODEL), lambda b, i: (b, jnp.maximum(i * hb - 1, 0), 0))
    full = lambda a: pl.BlockSpec(a.shape, lambda b, i: (0,) * a.ndim)
    args = (x3, u3, u3, w_dw, b_dw.reshape(1, -1), ln_g.reshape(1, -1), ln_b.reshape(1, -1),
            w_pw2.astype(BF16), b_pw2.reshape(1, -1))
    out = pl.pallas_call(
        _dwconv_kernel,
        grid=(batch, seq // tm),
        in_specs=[blk, blk, halo] + [full(a) for a in args[3:]],
        out_specs=blk,
        out_shape=jax.ShapeDtypeStruct((batch, seq, D_MODEL), F32),
        scratch_shapes=[pltpu.VMEM((HALO + tm + SUBLANE_PAD, D_MODEL), F32),
                        pltpu.VMEM((tm, D_MODEL), F32)],
        compiler_params=_cparams(2),
        name="conv_dw",
    )(*args)
    return out.reshape(batch * seq, D_MODEL)


def kernel(x, norm_mix, norm_mlp, norm_final, rel_bias, attn_w_in, attn_q_norm, attn_kv_norm, attn_kidx_norm, attn_w_qidx, attn_w_uq, attn_w_uk, attn_w_uv, attn_w_o, conv_w_pw1, conv_b_pw1, conv_w_dw, conv_b_dw, conv_ln_g, conv_ln_b, conv_w_pw2, conv_b_pw2, mlp_w_up, mlp_w_down):
    batch, seq, d = x.shape
    depth = norm_mix.shape[0]
    x2 = x.reshape(batch * seq, d)
    for i in range(depth):
        j = i // 2
        if i % 2 == 0:
            qidx, widx, qlat, kidx, ckv = _proj(
                x2, norm_mix[i], attn_w_in[j], attn_q_norm[j], attn_kv_norm[j], attn_kidx_norm[j],
                attn_w_qidx[j], attn_w_uq[j], attn_w_uk[j])
            olat = _attention(qidx, widx, qlat, kidx, ckv, rel_bias, batch, seq)
            mixer_out = (olat, attn_w_uv[j], attn_w_o[j])
        else:
            mixer_out = None
            u = _glu(x2, norm_mix[i], conv_w_pw1[j], conv_b_pw1[j])
            x2 = _dwconv(x2, u, conv_w_dw[j], conv_b_dw[j], conv_ln_g[j], conv_ln_b[j],
                         conv_w_pw2[j], conv_b_pw2[j], batch, seq)
        last = i == depth - 1
        x2 = _mlp(x2, norm_mlp[i], mlp_w_up[i], mlp_w_down[i], norm_final, final_norm=last,
                  attn_out=mixer_out)
    if depth == 0:
        raise ValueError("depth must be positive")
    return x2.reshape(batch, seq, d)
```

```python
import functools
import math

import jax
import jax.numpy as jnp
from jax import lax
from jax.experimental import pallas as pl
from jax.experimental.pallas import tpu as pltpu

F32 = jnp.float32
BF16 = jnp.bfloat16
I32 = jnp.int32

D_MODEL = 1024
N_HEADS = 16
D_NOPE = 64
D_V = 64
D_CQ = 256
D_CKV = 128
IDX_HEADS = 8
IDX_DIM = 64
TOPK_MAX = 256
CONV_WIDTH = 31
NUM_BUCKETS = 32
MAX_EXACT = 16
MAX_DISTANCE = 128
EPS = 1e-6

LANES = 128
SUBLANES = 8
TQ = 128
SCORE_TILE = 512
ATT_TK = 512
BIAS_TK = 2 * TQ
PAD_KEYS = ATT_TK - TQ
NEG_MASK = -1e30
LOG2E = math.log2(math.e)
FIXED_BITS = 22
BITS_PER_TEST = 2
PREFIX_ROWS = 256
VMEM_LIMIT = 56 * 1024 * 1024


def _cparams(n_axes):
    return pltpu.CompilerParams(dimension_semantics=("arbitrary",) * n_axes,
                                vmem_limit_bytes=VMEM_LIMIT)


def _rms(x, g):
    return x * lax.rsqrt(jnp.mean(x * x, axis=-1, keepdims=True) + EPS) * g


def _proj_kernel(x_ref, g_ref, win_ref, qn_ref, kvn_ref, kin_ref, wqidx_ref, wuq_ref, wukt_ref,
                 qidx_ref, widx_ref, qlat_ref, kidx_ref, ckv_ref):
    h = _rms(x_ref[...], g_ref[...])
    proj = jnp.dot(h.astype(BF16), win_ref[...], preferred_element_type=F32)
    o1, o2, o3 = D_CQ, D_CQ + D_CKV, D_CQ + D_CKV + IDX_DIM
    cq = _rms(proj[:, :o1], qn_ref[...])
    ckv = _rms(proj[:, o1:o2], kvn_ref[...])
    kid = _rms(proj[:, o2:o3], kin_ref[...])
    widx_ref[...] = proj[:, o3:o3 + IDX_HEADS] * (IDX_HEADS ** -0.5)
    ckv_ref[...] = ckv.astype(BF16)
    kidx_ref[...] = kid.astype(BF16)
    cqb = cq.astype(BF16)
    qidx = jnp.dot(cqb, wqidx_ref[...], preferred_element_type=F32) * (IDX_DIM ** -0.5)
    qidx_ref[...] = qidx.astype(BF16)
    qh = jnp.dot(cqb, wuq_ref[...], preferred_element_type=F32).astype(BF16)
    for hp in range(N_HEADS // 2):
        ql = jnp.dot(qh[:, hp * 2 * D_NOPE:(hp + 1) * 2 * D_NOPE], wukt_ref[hp],
                     preferred_element_type=F32) * (D_NOPE ** -0.5 * LOG2E)
        qlat_ref[:, hp * 2 * D_CKV:(hp + 1) * 2 * D_CKV] = ql.astype(BF16)


def _proj(x2, g, w_in, qn, kvn, kin, w_qidx, w_uq, w_uk, tm=1024):
    n = x2.shape[0]
    ncol = w_in.shape[1]
    npad = -ncol % LANES
    win = jnp.pad(w_in, ((0, 0), (0, npad))).astype(BF16)
    wukt = jnp.transpose(w_uk, (1, 2, 0)).reshape(N_HEADS // 2, 2, D_NOPE, D_CKV)
    eye = jnp.eye(2, dtype=w_uk.dtype)
    wukt = (wukt[:, :, :, None, :] * eye[None, :, None, :, None]).reshape(
        N_HEADS // 2, 2 * D_NOPE, 2 * D_CKV).astype(BF16)
    full = lambda a: pl.BlockSpec(a.shape, lambda i: (0,) * a.ndim)
    row = lambda c: pl.BlockSpec((tm, c), lambda i: (i, 0))
    args = (x2, g.reshape(1, -1), win, qn.reshape(1, -1), kvn.reshape(1, -1), kin.reshape(1, -1),
            w_qidx.astype(BF16), w_uq.astype(BF16), wukt)
    return pl.pallas_call(
        _proj_kernel,
        grid=(n // tm,),
        in_specs=[row(D_MODEL)] + [full(a) for a in args[1:]],
        out_specs=[row(IDX_HEADS * IDX_DIM), row(IDX_HEADS), row(N_HEADS * D_CKV), row(IDX_DIM),
                   row(D_CKV)],
        out_shape=[jax.ShapeDtypeStruct((n, IDX_HEADS * IDX_DIM), BF16),
                   jax.ShapeDtypeStruct((n, IDX_HEADS), F32),
                   jax.ShapeDtypeStruct((n, N_HEADS * D_CKV), BF16),
                   jax.ShapeDtypeStruct((n, IDX_DIM), BF16),
                   jax.ShapeDtypeStruct((n, D_CKV), BF16)],
        compiler_params=_cparams(1),
        name="dsa_proj",
    )(*args)


def _attn_kernel(qidx_ref, widxt_ref, qlat_ref, kidx_ref, vaug_ref, bias_ref, tri_ref, olat_ref,
                 keys_ref, maskt_ref, a_ref, acc_ref, m_ref, s_ref, qk_ref, qs_ref, *, k_sel):
    n_rows = N_HEADS * TQ
    q0 = 2 * pl.program_id(1) * TQ

    n_tiles = q0 // SCORE_TILE + 1
    krow = lax.broadcasted_iota(I32, (SCORE_TILE, TQ), 0)
    qpos = [q0 + sub * TQ + lax.broadcasted_iota(I32, (1, TQ), 1) for sub in range(2)]
    trivial = [qp < k_sel for qp in qpos]

    def tile_rows(tt):
        return pl.ds(pl.multiple_of(tt * SCORE_TILE, SCORE_TILE), SCORE_TILE)

    def score_pass(sub):
        for h in range(IDX_HEADS):
            qs_ref[h * TQ:(h + 1) * TQ, :] = qidx_ref[sub * TQ:(sub + 1) * TQ,
                                                      h * IDX_DIM:(h + 1) * IDX_DIM]

        def score_dot(tt, slot):
            rows = tile_rows(jnp.minimum(tt, keys_ref.shape[1] // SCORE_TILE - 1))
            qk_ref[slot] = lax.dot_general(kidx_ref[rows, :], qs_ref[...], (((1,), (1,)), ((), ())),
                                           preferred_element_type=F32)

        def score_keys(tt, slot):
            sc = jnp.zeros((SCORE_TILE, TQ), F32)
            for h in range(IDX_HEADS):
                sc = sc + widxt_ref[sub, h:h + 1, :] * jnp.maximum(
                    qk_ref[slot, :, h * TQ:(h + 1) * TQ], 0.0)
            sc = jnp.where(tt * SCORE_TILE + krow <= qpos[sub], sc, -jnp.inf)
            bits = pltpu.bitcast(sc, I32)
            keys_ref[sub, tile_rows(tt), :] = bits ^ ((bits >> 31) & 0x7FFFFFFF)

        score_dot(0, 0)

        def score_pair(pp, carry):
            tt = 2 * pp
            score_dot(tt + 1, 1)
            score_keys(tt, 0)
            score_dot(tt + 2, 0)
            score_keys(tt + 1, 1)
            return carry

        lax.fori_loop(0, n_tiles // 2, score_pair, 0)

        @pl.when(n_tiles % 2 == 1)
        def _():
            score_keys(n_tiles - 1, 0)

    for sub in range(2):
        score_pass(sub)

    def tile_reduce(x, op):
        return op(x.reshape(SCORE_TILE // SUBLANES, SUBLANES, TQ), axis=0)

    def count_ge(thr):
        def body(tt, acc):
            return tuple(acc[sub] + tile_reduce(
                jnp.where(keys_ref[sub, tile_rows(tt), :] >= thr[sub], 1, 0), jnp.sum)
                for sub in range(2))
        zero = jnp.zeros((SUBLANES, TQ), I32)
        acc = lax.fori_loop(0, n_tiles, body, (zero, zero))
        return tuple(jnp.sum(a, axis=0, keepdims=True) for a in acc)

    def bit_step(step, st):
        trial = tuple(s[0] ^ (jnp.int32(1) << (31 - step)) for s in st)
        cnt = count_ge(trial)
        out = []
        for sub in range(2):
            cand, cntc, cnt_ub = st[sub]
            ok = cnt[sub] >= k_sel
            out.append((jnp.where(ok, trial[sub], cand), jnp.where(ok, cnt[sub], cntc),
                        jnp.where(ok, cnt_ub, cnt[sub])))
        return tuple(out)

    def resolve(step, st, tval, res):
        top = tuple(s[0] | ((jnp.int32(1) << (32 - step)) - 1) for s in st)

        def body(tt, c):
            out = []
            for sub in range(2):
                k = keys_ref[sub, tile_rows(tt), :]
                inb = jnp.logical_and(k >= st[sub][0], k <= top[sub])
                out.append((jnp.minimum(c[sub][0], tile_reduce(jnp.where(inb, k, 2 ** 31 - 1), jnp.min)),
                            jnp.maximum(c[sub][1], tile_reduce(jnp.where(inb, k, -2 ** 31), jnp.max))))
            return tuple(out)

        init = (jnp.full((SUBLANES, TQ), 2 ** 31 - 1, I32), jnp.full((SUBLANES, TQ), -2 ** 31, I32))
        mnmx = lax.fori_loop(0, n_tiles, body, (init, init))
        tvals, ress, n_open = [], [], 0
        for sub in range(2):
            _, cntc, cnt_ub = st[sub]
            mn = jnp.min(mnmx[sub][0], axis=0, keepdims=True)
            mx = jnp.max(mnmx[sub][1], axis=0, keepdims=True)
            known = jnp.logical_or(mn == mx, cnt_ub == k_sel - 1)
            tvals.append(jnp.where(known, mx, tval[sub]))
            ress.append(jnp.where(known, 1, res[sub]))
            settled = jnp.logical_or(jnp.logical_or(ress[sub] > 0, cntc == k_sel), trivial[sub])
            n_open = n_open + jnp.sum(jnp.where(settled, 0, 1))
        return tuple(tvals), tuple(ress), n_open

    st0 = (jnp.full((1, TQ), -2 ** 31, I32), jnp.full((1, TQ), 2 ** 30, I32), jnp.zeros((1, TQ), I32))
    st = lax.fori_loop(0, FIXED_BITS, bit_step, (st0, st0))
    zeros = jnp.zeros((1, TQ), I32)
    tval, res, n_open = resolve(FIXED_BITS, st, (st[0][0], st[1][0]), (zeros, zeros))

    def search_body(c):
        step, st, tval, res, _ = c
        for b in range(BITS_PER_TEST):
            st = bit_step(step + b, st)
        tval, res, n_open = resolve(step + BITS_PER_TEST, st, tval, res)
        return step + BITS_PER_TEST, st, tval, res, n_open

    _, st, tval, res, _ = lax.while_loop(
        lambda c: jnp.logical_and(c[0] < 32, c[4] > 0), search_body,
        (jnp.int32(FIXED_BITS), st, tval, res, n_open))

    thr, tie_take = [], []
    for sub in range(2):
        cand, cntc, cnt_ub = st[sub]
        exact = jnp.logical_or(cntc == k_sel, trivial[sub])
        t = jnp.where(exact, cand, jnp.where(res[sub] > 0, tval[sub], cand))
        thr.append(jnp.where(trivial[sub], -2 ** 31, t))
        tie_take.append(jnp.where(exact, 2.0 ** 30, (k_sel - cnt_ub).astype(F32)))

    def mask_tile(sub, tt, seen):
        k = keys_ref[sub, tile_rows(tt), :]
        idx = tt * SCORE_TILE + krow
        tie = k == thr[sub]
        tie_bf = jnp.where(tie, 1.0, 0.0).astype(BF16)
        wanted = []
        for c in range(SCORE_TILE // PREFIX_ROWS):
            rank = jnp.dot(tri_ref[...], tie_bf[c * PREFIX_ROWS:(c + 1) * PREFIX_ROWS],
                           preferred_element_type=F32)
            wanted.append(rank <= tie_take[sub] - seen)
            seen = seen + rank[PREFIX_ROWS - 1:PREFIX_ROWS, :]
        sel = jnp.logical_or(k > thr[sub], jnp.logical_and(tie, jnp.concatenate(wanted, axis=0)))
        sel = jnp.logical_and(sel, idx <= qpos[sub])
        rows = pl.ds(pl.multiple_of(PAD_KEYS + tt * SCORE_TILE, TQ), SCORE_TILE)
        maskt_ref[sub, rows, :] = jnp.where(sel, 0.0, NEG_MASK).astype(BF16)
        return seen

    def mask_trip(pp, seen):
        return tuple(mask_tile(sub, 2 * pp + 1, mask_tile(sub, 2 * pp, seen[sub])) for sub in range(2))

    zero_f = jnp.zeros((1, TQ), F32)
    seen = lax.fori_loop(0, n_tiles // 2, mask_trip, (zero_f, zero_f))

    @pl.when(n_tiles % 2 == 1)
    def _():
        for sub in range(2):
            mask_tile(sub, n_tiles - 1, seen[sub])

    for sub in range(2):
        maskt_ref[sub, :PAD_KEYS, :] = jnp.full((PAD_KEYS, TQ), NEG_MASK, BF16)

    eye = (lax.broadcasted_iota(I32, (TQ, TQ), 0) == lax.broadcasted_iota(I32, (TQ, TQ), 1))
    eye = jnp.where(eye, 1.0, 0.0).astype(BF16)
    for sub in range(2):
        for h in range(N_HEADS):
            a_ref[sub, h * TQ:(h + 1) * TQ, :D_CKV] = qlat_ref[sub * TQ:(sub + 1) * TQ,
                                                                h * D_CKV:(h + 1) * D_CKV]
            a_ref[sub, h * TQ:(h + 1) * TQ, D_CKV:] = eye
    m_ref[...] = jnp.full((2, n_rows, LANES), -jnp.inf, F32)
    acc_ref[...] = jnp.zeros((2, n_rows, 2 * D_CKV), F32)
    n_chunks = ATT_TK // LANES
    n_steps = (q0 + TQ + ATT_TK - 1) // ATT_TK

    def key_rows(sub, j):
        r0 = jnp.maximum(q0 + (sub + 1) * TQ + PAD_KEYS - (j + 1) * ATT_TK, 0)
        return pl.ds(pl.multiple_of(r0, TQ), ATT_TK)

    def logits(sub, j):
        rows = key_rows(sub, j)
        rhs = jnp.concatenate([vaug_ref[rows, :D_CKV], maskt_ref[sub, rows, :]], axis=1)
        s_ref[sub] = lax.dot_general(a_ref[sub], rhs, (((1,), (1,)), ((), ())),
                                     preferred_element_type=F32)

    def consume(sub, j):
        vk = vaug_ref[key_rows(sub, j), :]
        near = jnp.minimum(j, 1)
        ps = []
        for h in range(N_HEADS):
            rows = slice(h * TQ, (h + 1) * TQ)
            sh = s_ref[sub, rows, :]
            sh = jnp.concatenate([sh[:, :ATT_TK - BIAS_TK],
                                  sh[:, ATT_TK - BIAS_TK:] + bias_ref[near, h]], axis=1)
            m_old = m_ref[sub, rows, :]
            m_new = jnp.maximum(m_old, jnp.max(sh, axis=1, keepdims=True))
            alpha = jnp.exp2(m_old - m_new)
            m_ref[sub, rows, :] = m_new
            p = jnp.exp2(sh - jnp.concatenate([m_new] * n_chunks, axis=1))
            ps.append(p.astype(BF16))
            acc_ref[sub, rows, :] = acc_ref[sub, rows, :] * jnp.concatenate([alpha, alpha], axis=1)
        acc_ref[sub] += jnp.dot(jnp.concatenate(ps, axis=0), vk, preferred_element_type=F32)

    logits(0, 0)

    def att_step(j, carry):
        logits(1, j)
        consume(0, j)
        logits(0, j + 1)
        consume(1, j)
        return carry

    lax.fori_loop(0, n_steps, att_step, 0)

    for sub in range(2):
        for h in range(N_HEADS):
            acc = acc_ref[sub, h * TQ:(h + 1) * TQ, :]
            olat_ref[sub * TQ:(sub + 1) * TQ, h * D_CKV:(h + 1) * D_CKV] = (
                acc[:, :D_CKV] / acc[:, D_CKV:]).astype(BF16)


def _t5_bucket(dist):
    n = jnp.maximum(dist, 0)
    nf = jnp.maximum(n, 1).astype(F32)
    large = MAX_EXACT + (jnp.log(nf / MAX_EXACT) / math.log(MAX_DISTANCE / MAX_EXACT)
                         * (NUM_BUCKETS - MAX_EXACT)).astype(I32)
    large = jnp.minimum(large, NUM_BUCKETS - 1)
    return jnp.where(n < MAX_EXACT, n, large)


def _near_bias(rel_bias):
    assert BIAS_TK >= TQ + MAX_DISTANCE - 1
    period = TQ + BIAS_TK
    e = jnp.arange(period)
    e = jnp.where(e < BIAS_TK, e, e - period)
    tab = rel_bias[_t5_bucket(BIAS_TK - TQ - e)] - rel_bias[NUM_BUCKETS - 1]
    flat = jnp.tile(tab, (TQ, 1))[:TQ * (period - 1)]
    near = flat.reshape(TQ, period - 1, N_HEADS)[:, :BIAS_TK]
    return (jnp.transpose(near, (2, 0, 1)) * LOG2E).astype(F32)


def _attention(qidx, widx, qlat, kidx, ckv, rel_bias, batch, seq):
    assert seq % SCORE_TILE == 0 and SCORE_TILE % TQ == 0 and ATT_TK % (2 * TQ) == 0
    k_sel = min(TOPK_MAX, seq // 4)
    nq = seq // TQ
    r3 = lambda a: a.reshape(batch, seq, a.shape[-1])
    widxt = jnp.transpose(widx.reshape(batch, nq, TQ, IDX_HEADS), (0, 1, 3, 2))
    ckv3 = jnp.pad(r3(ckv), ((0, 0), (PAD_KEYS, 0), (0, 0)))
    vaug = jnp.concatenate([ckv3, jnp.ones_like(ckv3)], axis=-1)
    bias = _near_bias(rel_bias)
    bias = jnp.stack([bias, jnp.zeros_like(bias)])
    tri = jnp.tril(jnp.ones((PREFIX_ROWS, PREFIX_ROWS), BF16))

    qblk = lambda c: pl.BlockSpec((None, 2 * TQ, c), lambda b, i: (b, i, 0))
    qtile = lambda a: pl.BlockSpec((None, 2) + a.shape[2:], lambda b, i: (b, i, 0, 0))
    per_b = lambda a: pl.BlockSpec((None,) + a.shape[1:], lambda b, i: (b,) + (0,) * (a.ndim - 1),
                                   pipeline_mode=pl.Buffered(1))
    n_rows = N_HEADS * TQ
    kidx3 = r3(kidx)
    out = pl.pallas_call(
        functools.partial(_attn_kernel, k_sel=k_sel),
        grid=(batch, nq // 2),
        in_specs=[qblk(IDX_HEADS * IDX_DIM), qtile(widxt), qblk(N_HEADS * D_CKV), per_b(kidx3), per_b(vaug),
                  pl.BlockSpec(bias.shape, lambda b, i: (0, 0, 0, 0),
                               pipeline_mode=pl.Buffered(1)),
                  pl.BlockSpec(tri.shape, lambda b, i: (0, 0))],
        out_specs=qblk(N_HEADS * D_CKV),
        out_shape=jax.ShapeDtypeStruct((batch, seq, N_HEADS * D_CKV), BF16),
        scratch_shapes=[
            pltpu.VMEM((2, seq, TQ), I32),
            pltpu.VMEM((2, seq + PAD_KEYS, TQ), BF16),
            pltpu.VMEM((2, n_rows, 2 * D_CKV), BF16),
            pltpu.VMEM((2, n_rows, 2 * D_CKV), F32),
            pltpu.VMEM((2, n_rows, LANES), F32),
            pltpu.VMEM((2, n_rows, ATT_TK), F32),
            pltpu.VMEM((2, SCORE_TILE, IDX_HEADS * TQ), F32),
            pltpu.VMEM((IDX_HEADS * TQ, IDX_DIM), BF16),
        ],
        compiler_params=_cparams(2),
        name="dsa_attn",
    )(r3(qidx), widxt, r3(qlat), kidx3, vaug, bias, tri)
    return out.reshape(batch * seq, N_HEADS * D_CKV)


def _mlp_kernel(*refs, f_chunk, final_norm, with_attn_out):
    if with_attn_out:
        x_ref, olat_ref, wuv_ref, wo_ref, g_ref, wup_ref, wdn_ref, gf_ref, y_ref, acc_ref = refs
        pair = 2 * D_CKV
        o = [jnp.dot(olat_ref[:, p * pair:(p + 1) * pair], wuv_ref[p], preferred_element_type=F32)
             for p in range(N_HEADS // 2)]
        o = jnp.concatenate(o, axis=1).astype(BF16)
        x = x_ref[...] + jnp.dot(o, wo_ref[...], preferred_element_type=F32)
    else:
        x_ref, g_ref, wup_ref, wdn_ref, gf_ref, y_ref, acc_ref = refs
        x = x_ref[...]
    h = _rms(x, g_ref[...]).astype(BF16)
    d_ff = wup_ref.shape[1]
    for c in range(d_ff // f_chunk):
        u = jnp.dot(h, wup_ref[:, c * f_chunk:(c + 1) * f_chunk], preferred_element_type=F32)
        a = jnp.square(jnp.maximum(u, 0.0)).astype(BF16)
        d = jnp.dot(a, wdn_ref[c * f_chunk:(c + 1) * f_chunk, :], preferred_element_type=F32)
        if c == 0:
            acc_ref[...] = d
        else:
            acc_ref[...] += d
    y = x + acc_ref[...]
    if final_norm:
        y = _rms(y, gf_ref[...])
    y_ref[...] = y


def _mlp(x2, g, w_up, w_down, g_final, final_norm, attn_out=None, tm=512, f_chunk=512):
    n = x2.shape[0]
    const = lambda a: pl.BlockSpec(a.shape, lambda i: (0,) * a.ndim, pipeline_mode=pl.Buffered(1))
    rows = lambda c: pl.BlockSpec((tm, c), lambda i: (i, 0))
    args, specs = [x2], [rows(D_MODEL)]
    if attn_out is not None:
        olat, w_uv, w_o = attn_out
        wuv = jnp.transpose(w_uv, (1, 0, 2)).reshape(N_HEADS // 2, 2, D_CKV, D_V)
        eye = jnp.eye(2, dtype=w_uv.dtype)
        wbd = (wuv[:, :, :, None, :] * eye[None, :, None, :, None]).reshape(
            N_HEADS // 2, 2 * D_CKV, 2 * D_V)
        extra = [olat, wbd.astype(BF16), w_o.astype(BF16)]
        args += extra
        specs += [rows(N_HEADS * D_CKV), const(extra[1]), const(extra[2])]
    weights = [g.reshape(1, -1), w_up.astype(BF16), w_down.astype(BF16), g_final.reshape(1, -1)]
    args += weights
    specs += [const(a) for a in weights]
    return pl.pallas_call(
        functools.partial(_mlp_kernel, f_chunk=f_chunk, final_norm=final_norm,
                          with_attn_out=attn_out is not None),
        grid=(n // tm,),
        in_specs=specs,
        out_specs=rows(D_MODEL),
        out_shape=jax.ShapeDtypeStruct((n, D_MODEL), F32),
        scratch_shapes=[pltpu.VMEM((tm, D_MODEL), F32)],
        compiler_params=_cparams(1),
        name="mlp_final" if final_norm else "mlp",
    )(*args)


def _glu_kernel(x_ref, g_ref, w_ref, b_ref, u_ref):
    h = _rms(x_ref[...], g_ref[...]).astype(BF16)
    u = jnp.dot(h, w_ref[...], preferred_element_type=F32) + b_ref[...]
    d = u.shape[1] // 2
    u_ref[...] = u[:, :d] * jax.nn.sigmoid(u[:, d:])


def _glu(x2, g, w_pw1, b_pw1, tm=1024):
    n = x2.shape[0]
    full = lambda a: pl.BlockSpec(a.shape, lambda i: (0,) * a.ndim)
    args = (x2, g.reshape(1, -1), w_pw1.astype(BF16), b_pw1.reshape(1, -1))
    return pl.pallas_call(
        _glu_kernel,
        grid=(n // tm,),
        in_specs=[pl.BlockSpec((tm, D_MODEL), lambda i: (i, 0))] + [full(a) for a in args[1:]],
        out_specs=pl.BlockSpec((tm, D_MODEL), lambda i: (i, 0)),
        out_shape=jax.ShapeDtypeStruct((n, D_MODEL), F32),
        compiler_params=_cparams(1),
        name="conv_glu",
    )(*args)


HALO = 32
CONV_ROWS = 128
SUBLANE_PAD = 8


def _dwconv_kernel(x_ref, u_ref, halo_ref, wdw_ref, bdw_ref, lng_ref, lnb_ref, w2_ref, b2_ref, y_ref,
                   ext_ref, cv_ref):
    i = pl.program_id(1)
    tm = u_ref.shape[0]
    ext_ref[:HALO, :] = jnp.where(i == 0, 0.0, halo_ref[...])
    ext_ref[HALO:HALO + tm, :] = u_ref[...]
    ext_ref[HALO + tm:, :] = jnp.zeros((SUBLANE_PAD, D_MODEL), F32)
    off = HALO - (CONV_WIDTH - 1)

    def rows_body(r, carry):
        r0 = pl.multiple_of(r * CONV_ROWS, CONV_ROWS)
        for lc in range(D_MODEL // LANES):
            lanes = slice(lc * LANES, (lc + 1) * LANES)
            acc = jnp.broadcast_to(bdw_ref[:, lanes], (CONV_ROWS, LANES))
            for s in range(SUBLANE_PAD):
                part = None
                for j in range(CONV_WIDTH):
                    if (off + j) % SUBLANE_PAD != s:
                        continue
                    rows = pl.ds(pl.multiple_of(r0 + (off + j - s), SUBLANE_PAD),
                                 CONV_ROWS + SUBLANE_PAD)
                    term = wdw_ref[j:j + 1, lanes] * ext_ref[rows, lanes]
                    part = term if part is None else part + term
                if part is not None:
                    acc = acc + part[s:s + CONV_ROWS]
            cv_ref[pl.ds(r0, CONV_ROWS), lanes] = acc
        return carry

    lax.fori_loop(0, tm // CONV_ROWS, rows_body, 0)
    v = cv_ref[...]
    mu = jnp.mean(v, axis=-1, keepdims=True)
    var = jnp.mean(jnp.square(v - mu), axis=-1, keepdims=True)
    v = (v - mu) * lax.rsqrt(var + EPS) * lng_ref[...] + lnb_ref[...]
    v = v * jax.nn.sigmoid(v)
    y_ref[...] = (x_ref[...] + jnp.dot(v.astype(BF16), w2_ref[...], preferred_element_type=F32)
                  + b2_ref[...])


def _dwconv(x2, u2, w_dw, b_dw, ln_g, ln_b, w_pw2, b_pw2, batch, seq, tm=1024):
    x3 = x2.reshape(batch, seq, D_MODEL)
    u3 = u2.reshape(batch, seq, D_MODEL)
    hb = tm // HALO
    blk = pl.BlockSpec((None, tm, D_MODEL), lambda b, i: (b, i, 0))
    halo = pl.BlockSpec((None, HALO, D_MODEL), lambda b, i: (b, jnp.maximum(i * hb - 1, 0), 0))
    full = lambda a: pl.BlockSpec(a.shape, lambda b, i: (0,) * a.ndim)
    args = (x3, u3, u3, w_dw, b_dw.reshape(1, -1), ln_g.reshape(1, -1), ln_b.reshape(1, -1),
            w_pw2.astype(BF16), b_pw2.reshape(1, -1))
    out = pl.pallas_call(
        _dwconv_kernel,
        grid=(batch, seq // tm),
        in_specs=[blk, blk, halo] + [full(a) for a in args[3:]],
        out_specs=blk,
        out_shape=jax.ShapeDtypeStruct((batch, seq, D_MODEL), F32),
        scratch_shapes=[pltpu.VMEM((HALO + tm + SUBLANE_PAD, D_MODEL), F32),
                        pltpu.VMEM((tm, D_MODEL), F32)],
        compiler_params=_cparams(2),
        name="conv_dw",
    )(*args)
    return out.reshape(batch * seq, D_MODEL)


def kernel(x, norm_mix, norm_mlp, norm_final, rel_bias, attn_w_in, attn_q_norm, attn_kv_norm, attn_kidx_norm, attn_w_qidx, attn_w_uq, attn_w_uk, attn_w_uv, attn_w_o, conv_w_pw1, conv_b_pw1, conv_w_dw, conv_b_dw, conv_ln_g, conv_ln_b, conv_w_pw2, conv_b_pw2, mlp_w_up, mlp_w_down):
    batch, seq, d = x.shape
    depth = norm_mix.shape[0]
    x2 = x.reshape(batch * seq, d)
    for i in range(depth):
        j = i // 2
        if i % 2 == 0:
            qidx, widx, qlat, kidx, ckv = _proj(
                x2, norm_mix[i], attn_w_in[j], attn_q_norm[j], attn_kv_norm[j], attn_kidx_norm[j],
                attn_w_qidx[j], attn_w_uq[j], attn_w_uk[j])
            olat = _attention(qidx, widx, qlat, kidx, ckv, rel_bias, batch, seq)
            mixer_out = (olat, attn_w_uv[j], attn_w_o[j])
        else:
            mixer_out = None
            u = _glu(x2, norm_mix[i], conv_w_pw1[j], conv_b_pw1[j])
            x2 = _dwconv(x2, u, conv_w_dw[j], conv_b_dw[j], conv_ln_g[j], conv_ln_b[j],
                         conv_w_pw2[j], conv_b_pw2[j], batch, seq)
        last = i == depth - 1
        x2 = _mlp(x2, norm_mlp[i], mlp_w_up[i], mlp_w_down[i], norm_final, final_norm=last,
                  attn_out=mixer_out)
    if depth == 0:
        raise ValueError("depth must be positive")
    return x2.reshape(batch, seq, d)
```

```python
import functools
import math

import jax
import jax.numpy as jnp
from jax import lax
from jax.experimental import pallas as pl
from jax.experimental.pallas import tpu as pltpu

F32 = jnp.float32
BF16 = jnp.bfloat16
I32 = jnp.int32

D_MODEL = 1024
N_HEADS = 16
D_NOPE = 64
D_V = 64
D_CQ = 256
D_CKV = 128
IDX_HEADS = 8
IDX_DIM = 64
TOPK_MAX = 256
CONV_WIDTH = 31
NUM_BUCKETS = 32
MAX_EXACT = 16
MAX_DISTANCE = 128
EPS = 1e-6

LANES = 128
SUBLANES = 8
TQ = 128
SCORE_TILE = 512
ATT_TK = 512
BIAS_TK = 2 * TQ
PAD_KEYS = ATT_TK - TQ
NEG_MASK = -1e30
LOG2E = math.log2(math.e)
FIXED_BITS = 20
BITS_PER_TEST = 2
PREFIX_ROWS = 256
VMEM_LIMIT = 56 * 1024 * 1024


def _cparams(n_axes):
    return pltpu.CompilerParams(dimension_semantics=("arbitrary",) * n_axes,
                                vmem_limit_bytes=VMEM_LIMIT)


def _rms(x, g):
    return x * lax.rsqrt(jnp.mean(x * x, axis=-1, keepdims=True) + EPS) * g


def _proj_kernel(x_ref, g_ref, win_ref, qn_ref, kvn_ref, kin_ref, wqidx_ref, wuq_ref, wukt_ref,
                 qidx_ref, widx_ref, qlat_ref, kidx_ref, ckv_ref):
    h = _rms(x_ref[...], g_ref[...])
    proj = jnp.dot(h.astype(BF16), win_ref[...], preferred_element_type=F32)
    o1, o2, o3 = D_CQ, D_CQ + D_CKV, D_CQ + D_CKV + IDX_DIM
    cq = _rms(proj[:, :o1], qn_ref[...])
    ckv = _rms(proj[:, o1:o2], kvn_ref[...])
    kid = _rms(proj[:, o2:o3], kin_ref[...])
    widx_ref[...] = proj[:, o3:o3 + IDX_HEADS] * (IDX_HEADS ** -0.5)
    ckv_ref[...] = ckv.astype(BF16)
    kidx_ref[...] = kid.astype(BF16)
    cqb = cq.astype(BF16)
    qidx = jnp.dot(cqb, wqidx_ref[...], preferred_element_type=F32) * (IDX_DIM ** -0.5)
    qidx_ref[...] = qidx.astype(BF16)
    qh = jnp.dot(cqb, wuq_ref[...], preferred_element_type=F32).astype(BF16)
    for hp in range(N_HEADS // 2):
        ql = jnp.dot(qh[:, hp * 2 * D_NOPE:(hp + 1) * 2 * D_NOPE], wukt_ref[hp],
                     preferred_element_type=F32) * (D_NOPE ** -0.5 * LOG2E)
        qlat_ref[:, hp * 2 * D_CKV:(hp + 1) * 2 * D_CKV] = ql.astype(BF16)


def _proj(x2, g, w_in, qn, kvn, kin, w_qidx, w_uq, w_uk, tm=1024):
    n = x2.shape[0]
    ncol = w_in.shape[1]
    npad = -ncol % LANES
    win = jnp.pad(w_in, ((0, 0), (0, npad))).astype(BF16)
    wukt = jnp.transpose(w_uk, (1, 2, 0)).reshape(N_HEADS // 2, 2, D_NOPE, D_CKV)
    eye = jnp.eye(2, dtype=w_uk.dtype)
    wukt = (wukt[:, :, :, None, :] * eye[None, :, None, :, None]).reshape(
        N_HEADS // 2, 2 * D_NOPE, 2 * D_CKV).astype(BF16)
    full = lambda a: pl.BlockSpec(a.shape, lambda i: (0,) * a.ndim)
    row = lambda c: pl.BlockSpec((tm, c), lambda i: (i, 0))
    args = (x2, g.reshape(1, -1), win, qn.reshape(1, -1), kvn.reshape(1, -1), kin.reshape(1, -1),
            w_qidx.astype(BF16), w_uq.astype(BF16), wukt)
    return pl.pallas_call(
        _proj_kernel,
        grid=(n // tm,),
        in_specs=[row(D_MODEL)] + [full(a) for a in args[1:]],
        out_specs=[row(IDX_HEADS * IDX_DIM), row(IDX_HEADS), row(N_HEADS * D_CKV), row(IDX_DIM),
                   row(D_CKV)],
        out_shape=[jax.ShapeDtypeStruct((n, IDX_HEADS * IDX_DIM), BF16),
                   jax.ShapeDtypeStruct((n, IDX_HEADS), F32),
                   jax.ShapeDtypeStruct((n, N_HEADS * D_CKV), BF16),
                   jax.ShapeDtypeStruct((n, IDX_DIM), BF16),
                   jax.ShapeDtypeStruct((n, D_CKV), BF16)],
        compiler_params=_cparams(1),
        name="dsa_proj",
    )(*args)


def _attn_kernel(qidx_ref, widxt_ref, qlat_ref, kidx_ref, vaug_ref, bias_ref, tri_ref, olat_ref,
                 keys_ref, maskt_ref, a_ref, acc_ref, m_ref, s_ref, qk_ref, qs_ref, *, k_sel):
    n_rows = N_HEADS * TQ
    q0 = 2 * pl.program_id(1) * TQ

    n_tiles = q0 // SCORE_TILE + 1
    krow = lax.broadcasted_iota(I32, (SCORE_TILE, TQ), 0)
    qpos = [q0 + sub * TQ + lax.broadcasted_iota(I32, (1, TQ), 1) for sub in range(2)]
    trivial = [qp < k_sel for qp in qpos]

    def tile_rows(tt):
        return pl.ds(pl.multiple_of(tt * SCORE_TILE, SCORE_TILE), SCORE_TILE)

    def score_pass(sub):
        for h in range(IDX_HEADS):
            qs_ref[h * TQ:(h + 1) * TQ, :] = qidx_ref[sub * TQ:(sub + 1) * TQ,
                                                      h * IDX_DIM:(h + 1) * IDX_DIM]

        def score_dot(tt, slot):
            rows = tile_rows(jnp.minimum(tt, keys_ref.shape[1] // SCORE_TILE - 1))
            qk_ref[slot] = lax.dot_general(kidx_ref[rows, :], qs_ref[...], (((1,), (1,)), ((), ())),
                                           preferred_element_type=F32)

        def score_keys(tt, slot):
            sc = jnp.zeros((SCORE_TILE, TQ), F32)
            for h in range(IDX_HEADS):
                sc = sc + widxt_ref[sub, h:h + 1, :] * jnp.maximum(
                    qk_ref[slot, :, h * TQ:(h + 1) * TQ], 0.0)
            sc = jnp.where(tt * SCORE_TILE + krow <= qpos[sub], sc, -jnp.inf)
            bits = pltpu.bitcast(sc, I32)
            keys_ref[sub, tile_rows(tt), :] = bits ^ ((bits >> 31) & 0x7FFFFFFF)

        score_dot(0, 0)

        def score_pair(pp, carry):
            tt = 2 * pp
            score_dot(tt + 1, 1)
            score_keys(tt, 0)
            score_dot(tt + 2, 0)
            score_keys(tt + 1, 1)
            return carry

        lax.fori_loop(0, n_tiles // 2, score_pair, 0)

        @pl.when(n_tiles % 2 == 1)
        def _():
            score_keys(n_tiles - 1, 0)

    for sub in range(2):
        score_pass(sub)

    def tile_reduce(x, op):
        return op(x.reshape(SCORE_TILE // SUBLANES, SUBLANES, TQ), axis=0)

    def count_ge(thr):
        def body(tt, acc):
            return tuple(acc[sub] + tile_reduce(
                jnp.where(keys_ref[sub, tile_rows(tt), :] >= thr[sub], 1, 0), jnp.sum)
                for sub in range(2))
        zero = jnp.zeros((SUBLANES, TQ), I32)
        acc = lax.fori_loop(0, n_tiles, body, (zero, zero))
        return tuple(jnp.sum(a, axis=0, keepdims=True) for a in acc)

    def bit_step(step, st):
        trial = tuple(s[0] ^ (jnp.int32(1) << (31 - step)) for s in st)
        cnt = count_ge(trial)
        out = []
        for sub in range(2):
            cand, cntc, cnt_ub = st[sub]
            ok = cnt[sub] >= k_sel
            out.append((jnp.where(ok, trial[sub], cand), jnp.where(ok, cnt[sub], cntc),
                        jnp.where(ok, cnt_ub, cnt[sub])))
        return tuple(out)

    def resolve(step, st, tval, res):
        top = tuple(s[0] | ((jnp.int32(1) << (32 - step)) - 1) for s in st)

        def body(tt, c):
            out = []
            for sub in range(2):
                k = keys_ref[sub, tile_rows(tt), :]
                inb = jnp.logical_and(k >= st[sub][0], k <= top[sub])
                out.append((jnp.minimum(c[sub][0], tile_reduce(jnp.where(inb, k, 2 ** 31 - 1), jnp.min)),
                            jnp.maximum(c[sub][1], tile_reduce(jnp.where(inb, k, -2 ** 31), jnp.max))))
            return tuple(out)

        init = (jnp.full((SUBLANES, TQ), 2 ** 31 - 1, I32), jnp.full((SUBLANES, TQ), -2 ** 31, I32))
        mnmx = lax.fori_loop(0, n_tiles, body, (init, init))
        tvals, ress, n_open = [], [], 0
        for sub in range(2):
            _, cntc, cnt_ub = st[sub]
            mn = jnp.min(mnmx[sub][0], axis=0, keepdims=True)
            mx = jnp.max(mnmx[sub][1], axis=0, keepdims=True)
            known = jnp.logical_or(mn == mx, cnt_ub == k_sel - 1)
            tvals.append(jnp.where(known, mx, tval[sub]))
            ress.append(jnp.where(known, 1, res[sub]))
            settled = jnp.logical_or(jnp.logical_or(ress[sub] > 0, cntc == k_sel), trivial[sub])
            n_open = n_open + jnp.sum(jnp.where(settled, 0, 1))
        return tuple(tvals), tuple(ress), n_open

    st0 = (jnp.full((1, TQ), -2 ** 31, I32), jnp.full((1, TQ), 2 ** 30, I32), jnp.zeros((1, TQ), I32))
    st = lax.fori_loop(0, FIXED_BITS, bit_step, (st0, st0))
    zeros = jnp.zeros((1, TQ), I32)
    tval, res, n_open = resolve(FIXED_BITS, st, (st[0][0], st[1][0]), (zeros, zeros))

    def search_body(c):
        step, st, tval, res, _ = c
        for b in range(BITS_PER_TEST):
            st = bit_step(step + b, st)
        tval, res, n_open = resolve(step + BITS_PER_TEST, st, tval, res)
        return step + BITS_PER_TEST, st, tval, res, n_open

    _, st, tval, res, _ = lax.while_loop(
        lambda c: jnp.logical_and(c[0] < 32, c[4] > 0), search_body,
        (jnp.int32(FIXED_BITS), st, tval, res, n_open))

    thr, tie_take = [], []
    for sub in range(2):
        cand, cntc, cnt_ub = st[sub]
        exact = jnp.logical_or(cntc == k_sel, trivial[sub])
        t = jnp.where(exact, cand, jnp.where(res[sub] > 0, tval[sub], cand))
        thr.append(jnp.where(trivial[sub], -2 ** 31, t))
        tie_take.append(jnp.where(exact, 2.0 ** 30, (k_sel - cnt_ub).astype(F32)))

    def mask_tile(sub, tt, seen):
        k = keys_ref[sub, tile_rows(tt), :]
        idx = tt * SCORE_TILE + krow
        tie = k == thr[sub]
        tie_bf = jnp.where(tie, 1.0, 0.0).astype(BF16)
        wanted = []
        for c in range(SCORE_TILE // PREFIX_ROWS):
            rank = jnp.dot(tri_ref[...], tie_bf[c * PREFIX_ROWS:(c + 1) * PREFIX_ROWS],
                           preferred_element_type=F32)
            wanted.append(rank <= tie_take[sub] - seen)
            seen = seen + rank[PREFIX_ROWS - 1:PREFIX_ROWS, :]
        sel = jnp.logical_or(k > thr[sub], jnp.logical_and(tie, jnp.concatenate(wanted, axis=0)))
        sel = jnp.logical_and(sel, idx <= qpos[sub])
        rows = pl.ds(pl.multiple_of(PAD_KEYS + tt * SCORE_TILE, TQ), SCORE_TILE)
        maskt_ref[sub, rows, :] = jnp.where(sel, 0.0, NEG_MASK).astype(BF16)
        return seen

    def mask_trip(pp, seen):
        return tuple(mask_tile(sub, 2 * pp + 1, mask_tile(sub, 2 * pp, seen[sub])) for sub in range(2))

    zero_f = jnp.zeros((1, TQ), F32)
    seen = lax.fori_loop(0, n_tiles // 2, mask_trip, (zero_f, zero_f))

    @pl.when(n_tiles % 2 == 1)
    def _():
        for sub in range(2):
            mask_tile(sub, n_tiles - 1, seen[sub])

    for sub in range(2):
        maskt_ref[sub, :PAD_KEYS, :] = jnp.full((PAD_KEYS, TQ), NEG_MASK, BF16)

    eye = (lax.broadcasted_iota(I32, (TQ, TQ), 0) == lax.broadcasted_iota(I32, (TQ, TQ), 1))
    eye = jnp.where(eye, 1.0, 0.0).astype(BF16)
    for sub in range(2):
        for h in range(N_HEADS):
            a_ref[sub, h * TQ:(h + 1) * TQ, :D_CKV] = qlat_ref[sub * TQ:(sub + 1) * TQ,
                                                                h * D_CKV:(h + 1) * D_CKV]
            a_ref[sub, h * TQ:(h + 1) * TQ, D_CKV:] = eye
    m_ref[...] = jnp.full((2, n_rows, LANES), -jnp.inf, F32)
    acc_ref[...] = jnp.zeros((2, n_rows, 2 * D_CKV), F32)
    n_chunks = ATT_TK // LANES
    n_steps = (q0 + TQ + ATT_TK - 1) // ATT_TK

    def key_rows(sub, j):
        r0 = jnp.maximum(q0 + (sub + 1) * TQ + PAD_KEYS - (j + 1) * ATT_TK, 0)
        return pl.ds(pl.multiple_of(r0, TQ), ATT_TK)

    def logits(sub, j):
        rows = key_rows(sub, j)
        rhs = jnp.concatenate([vaug_ref[rows, :D_CKV], maskt_ref[sub, rows, :]], axis=1)
        s_ref[sub] = lax.dot_general(a_ref[sub], rhs, (((1,), (1,)), ((), ())),
                                     preferred_element_type=F32)

    def consume(sub, j):
        vk = vaug_ref[key_rows(sub, j), :]
        near = jnp.minimum(j, 1)
        ps = []
        for h in range(N_HEADS):
            rows = slice(h * TQ, (h + 1) * TQ)
            sh = s_ref[sub, rows, :]
            sh = jnp.concatenate([sh[:, :ATT_TK - BIAS_TK],
                                  sh[:, ATT_TK - BIAS_TK:] + bias_ref[near, h]], axis=1)
            m_old = m_ref[sub, rows, :]
            m_new = jnp.maximum(m_old, jnp.max(sh, axis=1, keepdims=True))
            alpha = jnp.exp2(m_old - m_new)
            m_ref[sub, rows, :] = m_new
            p = jnp.exp2(sh - jnp.concatenate([m_new] * n_chunks, axis=1))
            ps.append(p.astype(BF16))
            acc_ref[sub, rows, :] = acc_ref[sub, rows, :] * jnp.concatenate([alpha, alpha], axis=1)
        acc_ref[sub] += jnp.dot(jnp.concatenate(ps, axis=0), vk, preferred_element_type=F32)

    logits(0, 0)

    def att_step(j, carry):
        logits(1, j)
        consume(0, j)
        logits(0, j + 1)
        consume(1, j)
        return carry

    lax.fori_loop(0, n_steps, att_step, 0)

    for sub in range(2):
        for h in range(N_HEADS):
            acc = acc_ref[sub, h * TQ:(h + 1) * TQ, :]
            olat_ref[sub * TQ:(sub + 1) * TQ, h * D_CKV:(h + 1) * D_CKV] = (
                acc[:, :D_CKV] / acc[:, D_CKV:]).astype(BF16)


def _t5_bucket(dist):
    n = jnp.maximum(dist, 0)
    nf = jnp.maximum(n, 1).astype(F32)
    large = MAX_EXACT + (jnp.log(nf / MAX_EXACT) / math.log(MAX_DISTANCE / MAX_EXACT)
                         * (NUM_BUCKETS - MAX_EXACT)).astype(I32)
    large = jnp.minimum(large, NUM_BUCKETS - 1)
    return jnp.where(n < MAX_EXACT, n, large)


def _near_bias(rel_bias):
    assert BIAS_TK >= TQ + MAX_DISTANCE - 1
    period = TQ + BIAS_TK
    e = jnp.arange(period)
    e = jnp.where(e < BIAS_TK, e, e - period)
    tab = rel_bias[_t5_bucket(BIAS_TK - TQ - e)] - rel_bias[NUM_BUCKETS - 1]
    flat = jnp.tile(tab, (TQ, 1))[:TQ * (period - 1)]
    near = flat.reshape(TQ, period - 1, N_HEADS)[:, :BIAS_TK]
    return (jnp.transpose(near, (2, 0, 1)) * LOG2E).astype(F32)


def _attention(qidx, widx, qlat, kidx, ckv, rel_bias, batch, seq):
    assert seq % SCORE_TILE == 0 and SCORE_TILE % TQ == 0 and ATT_TK % (2 * TQ) == 0
    k_sel = min(TOPK_MAX, seq // 4)
    nq = seq // TQ
    r3 = lambda a: a.reshape(batch, seq, a.shape[-1])
    widxt = jnp.transpose(widx.reshape(batch, nq, TQ, IDX_HEADS), (0, 1, 3, 2))
    ckv3 = jnp.pad(r3(ckv), ((0, 0), (PAD_KEYS, 0), (0, 0)))
    vaug = jnp.concatenate([ckv3, jnp.ones_like(ckv3)], axis=-1)
    bias = _near_bias(rel_bias)
    bias = jnp.stack([bias, jnp.zeros_like(bias)])
    tri = jnp.tril(jnp.ones((PREFIX_ROWS, PREFIX_ROWS), BF16))

    qblk = lambda c: pl.BlockSpec((None, 2 * TQ, c), lambda b, i: (b, i, 0))
    qtile = lambda a: pl.BlockSpec((None, 2) + a.shape[2:], lambda b, i: (b, i, 0, 0))
    per_b = lambda a: pl.BlockSpec((None,) + a.shape[1:], lambda b, i: (b,) + (0,) * (a.ndim - 1),
                                   pipeline_mode=pl.Buffered(1))
    n_rows = N_HEADS * TQ
    kidx3 = r3(kidx)
    out = pl.pallas_call(
        functools.partial(_attn_kernel, k_sel=k_sel),
        grid=(batch, nq // 2),
        in_specs=[qblk(IDX_HEADS * IDX_DIM), qtile(widxt), qblk(N_HEADS * D_CKV), per_b(kidx3), per_b(vaug),
                  pl.BlockSpec(bias.shape, lambda b, i: (0, 0, 0, 0),
                               pipeline_mode=pl.Buffered(1)),
                  pl.BlockSpec(tri.shape, lambda b, i: (0, 0))],
        out_specs=qblk(N_HEADS * D_CKV),
        out_shape=jax.ShapeDtypeStruct((batch, seq, N_HEADS * D_CKV), BF16),
        scratch_shapes=[
            pltpu.VMEM((2, seq, TQ), I32),
            pltpu.VMEM((2, seq + PAD_KEYS, TQ), BF16),
            pltpu.VMEM((2, n_rows, 2 * D_CKV), BF16),
            pltpu.VMEM((2, n_rows, 2 * D_CKV), F32),
            pltpu.VMEM((2, n_rows, LANES), F32),
            pltpu.VMEM((2, n_rows, ATT_TK), F32),
            pltpu.VMEM((2, SCORE_TILE, IDX_HEADS * TQ), F32),
            pltpu.VMEM((IDX_HEADS * TQ, IDX_DIM), BF16),
        ],
        compiler_params=_cparams(2),
        name="dsa_attn",
    )(r3(qidx), widxt, r3(qlat), kidx3, vaug, bias, tri)
    return out.reshape(batch * seq, N_HEADS * D_CKV)


def _mlp_kernel(*refs, f_chunk, final_norm, with_attn_out):
    if with_attn_out:
        x_ref, olat_ref, wuv_ref, wo_ref, g_ref, wup_ref, wdn_ref, gf_ref, y_ref, acc_ref = refs
        pair = 2 * D_CKV
        o = [jnp.dot(olat_ref[:, p * pair:(p + 1) * pair], wuv_ref[p], preferred_element_type=F32)
             for p in range(N_HEADS // 2)]
        o = jnp.concatenate(o, axis=1).astype(BF16)
        x = x_ref[...] + jnp.dot(o, wo_ref[...], preferred_element_type=F32)
    else:
        x_ref, g_ref, wup_ref, wdn_ref, gf_ref, y_ref, acc_ref = refs
        x = x_ref[...]
    h = _rms(x, g_ref[...]).astype(BF16)
    d_ff = wup_ref.shape[1]
    for c in range(d_ff // f_chunk):
        u = jnp.dot(h, wup_ref[:, c * f_chunk:(c + 1) * f_chunk], preferred_element_type=F32)
        a = jnp.square(jnp.maximum(u, 0.0)).astype(BF16)
        d = jnp.dot(a, wdn_ref[c * f_chunk:(c + 1) * f_chunk, :], preferred_element_type=F32)
        if c == 0:
            acc_ref[...] = d
        else:
            acc_ref[...] += d
    y = x + acc_ref[...]
    if final_norm:
        y = _rms(y, gf_ref[...])
    y_ref[...] = y


def _mlp(x2, g, w_up, w_down, g_final, final_norm, attn_out=None, tm=512, f_chunk=512):
    n = x2.shape[0]
    const = lambda a: pl.BlockSpec(a.shape, lambda i: (0,) * a.ndim, pipeline_mode=pl.Buffered(1))
    rows = lambda c: pl.BlockSpec((tm, c), lambda i: (i, 0))
    args, specs = [x2], [rows(D_MODEL)]
    if attn_out is not None:
        olat, w_uv, w_o = attn_out
        wuv = jnp.transpose(w_uv, (1, 0, 2)).reshape(N_HEADS // 2, 2, D_CKV, D_V)
        eye = jnp.eye(2, dtype=w_uv.dtype)
        wbd = (wuv[:, :, :, None, :] * eye[None, :, None, :, None]).reshape(
            N_HEADS // 2, 2 * D_CKV, 2 * D_V)
        extra = [olat, wbd.astype(BF16), w_o.astype(BF16)]
        args += extra
        specs += [rows(N_HEADS * D_CKV), const(extra[1]), const(extra[2])]
    weights = [g.reshape(1, -1), w_up.astype(BF16), w_down.astype(BF16), g_final.reshape(1, -1)]
    args += weights
    specs += [const(a) for a in weights]
    return pl.pallas_call(
        functools.partial(_mlp_kernel, f_chunk=f_chunk, final_norm=final_norm,
                          with_attn_out=attn_out is not None),
        grid=(n // tm,),
        in_specs=specs,
        out_specs=rows(D_MODEL),
        out_shape=jax.ShapeDtypeStruct((n, D_MODEL), F32),
        scratch_shapes=[pltpu.VMEM((tm, D_MODEL), F32)],
        compiler_params=_cparams(1),
        name="mlp_final" if final_norm else "mlp",
    )(*args)


def _glu_kernel(x_ref, g_ref, w_ref, b_ref, u_ref):
    h = _rms(x_ref[...], g_ref[...]).astype(BF16)
    u = jnp.dot(h, w_ref[...], preferred_element_type=F32) + b_ref[...]
    d = u.shape[1] // 2
    u_ref[...] = u[:, :d] * jax.nn.sigmoid(u[:, d:])


def _glu(x2, g, w_pw1, b_pw1, tm=1024):
    n = x2.shape[0]
    full = lambda a: pl.BlockSpec(a.shape, lambda i: (0,) * a.ndim)
    args = (x2, g.reshape(1, -1), w_pw1.astype(BF16), b_pw1.reshape(1, -1))
    return pl.pallas_call(
        _glu_kernel,
        grid=(n // tm,),
        in_specs=[pl.BlockSpec((tm, D_MODEL), lambda i: (i, 0))] + [full(a) for a in args[1:]],
        out_specs=pl.BlockSpec((tm, D_MODEL), lambda i: (i, 0)),
        out_shape=jax.ShapeDtypeStruct((n, D_MODEL), F32),
        compiler_params=_cparams(1),
        name="conv_glu",
    )(*args)


HALO = 32
CONV_ROWS = 128
SUBLANE_PAD = 8


def _dwconv_kernel(x_ref, u_ref, halo_ref, wdw_ref, bdw_ref, lng_ref, lnb_ref, w2_ref, b2_ref, y_ref,
                   ext_ref, cv_ref):
    i = pl.program_id(1)
    tm = u_ref.shape[0]
    ext_ref[:HALO, :] = jnp.where(i == 0, 0.0, halo_ref[...])
    ext_ref[HALO:HALO + tm, :] = u_ref[...]
    ext_ref[HALO + tm:, :] = jnp.zeros((SUBLANE_PAD, D_MODEL), F32)
    off = HALO - (CONV_WIDTH - 1)

    def rows_body(r, carry):
        r0 = pl.multiple_of(r * CONV_ROWS, CONV_ROWS)
        for lc in range(D_MODEL // LANES):
            lanes = slice(lc * LANES, (lc + 1) * LANES)
            acc = jnp.broadcast_to(bdw_ref[:, lanes], (CONV_ROWS, LANES))
            for s in range(SUBLANE_PAD):
                part = None
                for j in range(CONV_WIDTH):
                    if (off + j) % SUBLANE_PAD != s:
                        continue
                    rows = pl.ds(pl.multiple_of(r0 + (off + j - s), SUBLANE_PAD),
                                 CONV_ROWS + SUBLANE_PAD)
                    term = wdw_ref[j:j + 1, lanes] * ext_ref[rows, lanes]
                    part = term if part is None else part + term
                if part is not None:
                    acc = acc + part[s:s + CONV_ROWS]
            cv_ref[pl.ds(r0, CONV_ROWS), lanes] = acc
        return carry

    lax.fori_loop(0, tm // CONV_ROWS, rows_body, 0)
    v = cv_ref[...]
    mu = jnp.mean(v, axis=-1, keepdims=True)
    var = jnp.mean(jnp.square(v - mu), axis=-1, keepdims=True)
    v = (v - mu) * lax.rsqrt(var + EPS) * lng_ref[...] + lnb_ref[...]
    v = v * jax.nn.sigmoid(v)
    y_ref[...] = (x_ref[...] + jnp.dot(v.astype(BF16), w2_ref[...], preferred_element_type=F32)
                  + b2_ref[...])


def _dwconv(x2, u2, w_dw, b_dw, ln_g, ln_b, w_pw2, b_pw2, batch, seq, tm=1024):
    x3 = x2.reshape(batch, seq, D_MODEL)
    u3 = u2.reshape(batch, seq, D_MODEL)
    hb = tm // HALO
    blk = pl.BlockSpec((None, tm, D_MODEL), lambda b, i: (b, i, 0))
    halo = pl.BlockSpec((None, HALO, D_MODEL), lambda b, i: (b, jnp.maximum(i * hb - 1, 0), 0))
    full = lambda a: pl.BlockSpec(a.shape, lambda b, i: (0,) * a.ndim)
    args = (x3, u3, u3, w_dw, b_dw.reshape(1, -1), ln_g.reshape(1, -1), ln_b.reshape(1, -1),
            w_pw2.astype(BF16), b_pw2.reshape(1, -1))
    out = pl.pallas_call(
        _dwconv_kernel,
        grid=(batch, seq // tm),
        in_specs=[blk, blk, halo] + [full(a) for a in args[3:]],
        out_specs=blk,
        out_shape=jax.ShapeDtypeStruct((batch, seq, D_MODEL), F32),
        scratch_shapes=[pltpu.VMEM((HALO + tm + SUBLANE_PAD, D_MODEL), F32),
                        pltpu.VMEM((tm, D_MODEL), F32)],
        compiler_params=_cparams(2),
        name="conv_dw",
    )(*args)
    return out.reshape(batch * seq, D_MODEL)


def kernel(x, norm_mix, norm_mlp, norm_final, rel_bias, attn_w_in, attn_q_norm, attn_kv_norm, attn_kidx_norm, attn_w_qidx, attn_w_uq, attn_w_uk, attn_w_uv, attn_w_o, conv_w_pw1, conv_b_pw1, conv_w_dw, conv_b_dw, conv_ln_g, conv_ln_b, conv_w_pw2, conv_b_pw2, mlp_w_up, mlp_w_down):
    batch, seq, d = x.shape
    depth = norm_mix.shape[0]
    x2 = x.reshape(batch * seq, d)
    for i in range(depth):
        j = i // 2
        if i % 2 == 0:
            qidx, widx, qlat, kidx, ckv = _proj(
                x2, norm_mix[i], attn_w_in[j], attn_q_norm[j], attn_kv_norm[j], attn_kidx_norm[j],
                attn_w_qidx[j], attn_w_uq[j], attn_w_uk[j])
            olat = _attention(qidx, widx, qlat, kidx, ckv, rel_bias, batch, seq)
            mixer_out = (olat, attn_w_uv[j], attn_w_o[j])
        else:
            mixer_out = None
            u = _glu(x2, norm_mix[i], conv_w_pw1[j], conv_b_pw1[j])
            x2 = _dwconv(x2, u, conv_w_dw[j], conv_b_dw[j], conv_ln_g[j], conv_ln_b[j],
                         conv_w_pw2[j], conv_b_pw2[j], batch, seq)
        last = i == depth - 1
        x2 = _mlp(x2, norm_mlp[i], mlp_w_up[i], mlp_w_down[i], norm_final, final_norm=last,
                  attn_out=mixer_out)
    if depth == 0:
        raise ValueError("depth must be positive")
    return x2.reshape(batch, seq, d)
```

```python
import functools
import math

import jax
import jax.numpy as jnp
from jax import lax
from jax.experimental import pallas as pl
from jax.experimental.pallas import tpu as pltpu

F32 = jnp.float32
BF16 = jnp.bfloat16
I32 = jnp.int32

D_MODEL = 1024
N_HEADS = 16
D_NOPE = 64
D_V = 64
D_CQ = 256
D_CKV = 128
IDX_HEADS = 8
IDX_DIM = 64
TOPK_MAX = 256
CONV_WIDTH = 31
NUM_BUCKETS = 32
MAX_EXACT = 16
MAX_DISTANCE = 128
EPS = 1e-6

LANES = 128
SUBLANES = 8
TQ = 128
SCORE_TILE = 512
ATT_TK = 512
BIAS_TK = 2 * TQ
PAD_KEYS = ATT_TK - TQ
NEG_MASK = -1e30
LOG2E = math.log2(math.e)
FIXED_BITS = 22
BITS_PER_TEST = 2
PREFIX_ROWS = 256
VMEM_LIMIT = 56 * 1024 * 1024


def _cparams(n_axes):
    return pltpu.CompilerParams(dimension_semantics=("arbitrary",) * n_axes,
                                vmem_limit_bytes=VMEM_LIMIT)


def _rms(x, g):
    return x * lax.rsqrt(jnp.mean(x * x, axis=-1, keepdims=True) + EPS) * g


def _proj_kernel(x_ref, g_ref, win_ref, qn_ref, kvn_ref, kin_ref, wqidx_ref, wuq_ref, wukt_ref,
                 qidx_ref, widx_ref, qlat_ref, kidx_ref, ckv_ref):
    h = _rms(x_ref[...], g_ref[...])
    proj = jnp.dot(h.astype(BF16), win_ref[...], preferred_element_type=F32)
    o1, o2, o3 = D_CQ, D_CQ + D_CKV, D_CQ + D_CKV + IDX_DIM
    cq = _rms(proj[:, :o1], qn_ref[...])
    ckv = _rms(proj[:, o1:o2], kvn_ref[...])
    kid = _rms(proj[:, o2:o3], kin_ref[...])
    widx_ref[...] = proj[:, o3:o3 + IDX_HEADS] * (IDX_HEADS ** -0.5)
    ckv_ref[...] = ckv.astype(BF16)
    kidx_ref[...] = kid.astype(BF16)
    cqb = cq.astype(BF16)
    qidx = jnp.dot(cqb, wqidx_ref[...], preferred_element_type=F32) * (IDX_DIM ** -0.5)
    qidx_ref[...] = qidx.astype(BF16)
    qh = jnp.dot(cqb, wuq_ref[...], preferred_element_type=F32).astype(BF16)
    for hp in range(N_HEADS // 2):
        ql = jnp.dot(qh[:, hp * 2 * D_NOPE:(hp + 1) * 2 * D_NOPE], wukt_ref[hp],
                     preferred_element_type=F32) * (D_NOPE ** -0.5 * LOG2E)
        qlat_ref[:, hp * 2 * D_CKV:(hp + 1) * 2 * D_CKV] = ql.astype(BF16)


def _proj(x2, g, w_in, qn, kvn, kin, w_qidx, w_uq, w_uk, tm=1024):
    n = x2.shape[0]
    ncol = w_in.shape[1]
    npad = -ncol % LANES
    win = jnp.pad(w_in, ((0, 0), (0, npad))).astype(BF16)
    wukt = jnp.transpose(w_uk, (1, 2, 0)).reshape(N_HEADS // 2, 2, D_NOPE, D_CKV)
    eye = jnp.eye(2, dtype=w_uk.dtype)
    wukt = (wukt[:, :, :, None, :] * eye[None, :, None, :, None]).reshape(
        N_HEADS // 2, 2 * D_NOPE, 2 * D_CKV).astype(BF16)
    full = lambda a: pl.BlockSpec(a.shape, lambda i: (0,) * a.ndim)
    row = lambda c: pl.BlockSpec((tm, c), lambda i: (i, 0))
    args = (x2, g.reshape(1, -1), win, qn.reshape(1, -1), kvn.reshape(1, -1), kin.reshape(1, -1),
            w_qidx.astype(BF16), w_uq.astype(BF16), wukt)
    return pl.pallas_call(
        _proj_kernel,
        grid=(n // tm,),
        in_specs=[row(D_MODEL)] + [full(a) for a in args[1:]],
        out_specs=[row(IDX_HEADS * IDX_DIM), row(IDX_HEADS), row(N_HEADS * D_CKV), row(IDX_DIM),
                   row(D_CKV)],
        out_shape=[jax.ShapeDtypeStruct((n, IDX_HEADS * IDX_DIM), BF16),
                   jax.ShapeDtypeStruct((n, IDX_HEADS), F32),
                   jax.ShapeDtypeStruct((n, N_HEADS * D_CKV), BF16),
                   jax.ShapeDtypeStruct((n, IDX_DIM), BF16),
                   jax.ShapeDtypeStruct((n, D_CKV), BF16)],
        compiler_params=_cparams(1),
        name="dsa_proj",
    )(*args)


def _attn_kernel(qidx_ref, widxt_ref, qlat_ref, kidx_ref, vaug_ref, bias_ref, tri_ref, olat_ref,
                 keys_ref, maskt_ref, a_ref, acc_ref, m_ref, s_ref, qk_ref, qs_ref, *, k_sel):
    n_rows = N_HEADS * TQ
    q0 = 2 * pl.program_id(1) * TQ

    n_tiles = q0 // SCORE_TILE + 1
    krow = lax.broadcasted_iota(I32, (SCORE_TILE, TQ), 0)
    qpos = [q0 + sub * TQ + lax.broadcasted_iota(I32, (1, TQ), 1) for sub in range(2)]
    trivial = [qp < k_sel for qp in qpos]

    def tile_rows(tt):
        return pl.ds(pl.multiple_of(tt * SCORE_TILE, SCORE_TILE), SCORE_TILE)

    def score_pass(sub):
        for h in range(IDX_HEADS):
            qs_ref[h * TQ:(h + 1) * TQ, :] = qidx_ref[sub * TQ:(sub + 1) * TQ,
                                                      h * IDX_DIM:(h + 1) * IDX_DIM]

        def score_dot(tt, slot):
            rows = tile_rows(jnp.minimum(tt, keys_ref.shape[1] // SCORE_TILE - 1))
            qk_ref[slot] = lax.dot_general(kidx_ref[rows, :], qs_ref[...], (((1,), (1,)), ((), ())),
                                           preferred_element_type=F32)

        def score_keys(tt, slot):
            sc = jnp.zeros((SCORE_TILE, TQ), F32)
            for h in range(IDX_HEADS):
                sc = sc + widxt_ref[sub, h:h + 1, :] * jnp.maximum(
                    qk_ref[slot, :, h * TQ:(h + 1) * TQ], 0.0)
            sc = jnp.where(tt * SCORE_TILE + krow <= qpos[sub], sc, -jnp.inf)
            bits = pltpu.bitcast(sc, I32)
            keys_ref[sub, tile_rows(tt), :] = bits ^ ((bits >> 31) & 0x7FFFFFFF)

        score_dot(0, 0)

        def score_pair(pp, carry):
            tt = 2 * pp
            score_dot(tt + 1, 1)
            score_keys(tt, 0)
            score_dot(tt + 2, 0)
            score_keys(tt + 1, 1)
            return carry

        lax.fori_loop(0, n_tiles // 2, score_pair, 0)

        @pl.when(n_tiles % 2 == 1)
        def _():
            score_keys(n_tiles - 1, 0)

    for sub in range(2):
        score_pass(sub)

    def tile_reduce(x, op):
        return op(x.reshape(SCORE_TILE // SUBLANES, SUBLANES, TQ), axis=0)

    def count_ge(thr):
        def body(tt, acc):
            return tuple(acc[sub] + tile_reduce(
                jnp.where(keys_ref[sub, tile_rows(tt), :] >= thr[sub], 1, 0), jnp.sum)
                for sub in range(2))
        zero = jnp.zeros((SUBLANES, TQ), I32)
        acc = lax.fori_loop(0, n_tiles, body, (zero, zero))
        return tuple(jnp.sum(a, axis=0, keepdims=True) for a in acc)

    def bit_step(step, st):
        trial = tuple(s[0] ^ (jnp.int32(1) << (31 - step)) for s in st)
        cnt = count_ge(trial)
        out = []
        for sub in range(2):
            cand, cntc, cnt_ub = st[sub]
            ok = cnt[sub] >= k_sel
            out.append((jnp.where(ok, trial[sub], cand), jnp.where(ok, cnt[sub], cntc),
                        jnp.where(ok, cnt_ub, cnt[sub])))
        return tuple(out)

    def resolve(step, st, tval, res):
        top = tuple(s[0] | ((jnp.int32(1) << (32 - step)) - 1) for s in st)

        def body(tt, c):
            out = []
            for sub in range(2):
                k = keys_ref[sub, tile_rows(tt), :]
                inb = jnp.logical_and(k >= st[sub][0], k <= top[sub])
                out.append((jnp.minimum(c[sub][0], tile_reduce(jnp.where(inb, k, 2 ** 31 - 1), jnp.min)),
                            jnp.maximum(c[sub][1], tile_reduce(jnp.where(inb, k, -2 ** 31), jnp.max))))
            return tuple(out)

        init = (jnp.full((SUBLANES, TQ), 2 ** 31 - 1, I32), jnp.full((SUBLANES, TQ), -2 ** 31, I32))
        mnmx = lax.fori_loop(0, n_tiles, body, (init, init))
        tvals, ress, n_open = [], [], 0
        for sub in range(2):
            _, cntc, cnt_ub = st[sub]
            mn = jnp.min(mnmx[sub][0], axis=0, keepdims=True)
            mx = jnp.max(mnmx[sub][1], axis=0, keepdims=True)
            known = jnp.logical_or(mn == mx, cnt_ub == k_sel - 1)
            tvals.append(jnp.where(known, mx, tval[sub]))
            ress.append(jnp.where(known, 1, res[sub]))
            settled = jnp.logical_or(jnp.logical_or(ress[sub] > 0, cntc == k_sel), trivial[sub])
            n_open = n_open + jnp.sum(jnp.where(settled, 0, 1))
        return tuple(tvals), tuple(ress), n_open

    st0 = (jnp.full((1, TQ), -2 ** 31, I32), jnp.full((1, TQ), 2 ** 30, I32), jnp.zeros((1, TQ), I32))
    st = lax.fori_loop(0, FIXED_BITS, bit_step, (st0, st0))
    zeros = jnp.zeros((1, TQ), I32)
    tval, res, n_open = resolve(FIXED_BITS, st, (st[0][0], st[1][0]), (zeros, zeros))

    def search_body(c):
        step, st, tval, res, _ = c
        for b in range(BITS_PER_TEST):
            st = bit_step(step + b, st)
        tval, res, n_open = resolve(step + BITS_PER_TEST, st, tval, res)
        return step + BITS_PER_TEST, st, tval, res, n_open

    _, st, tval, res, _ = lax.while_loop(
        lambda c: jnp.logical_and(c[0] < 32, c[4] > 0), search_body,
        (jnp.int32(FIXED_BITS), st, tval, res, n_open))

    thr, tie_take = [], []
    for sub in range(2):
        cand, cntc, cnt_ub = st[sub]
        exact = jnp.logical_or(cntc == k_sel, trivial[sub])
        t = jnp.where(exact, cand, jnp.where(res[sub] > 0, tval[sub], cand))
        thr.append(jnp.where(trivial[sub], -2 ** 31, t))
        tie_take.append(jnp.where(exact, 2.0 ** 30, (k_sel - cnt_ub).astype(F32)))

    def mask_tile(sub, tt, seen):
        k = keys_ref[sub, tile_rows(tt), :]
        idx = tt * SCORE_TILE + krow
        tie = k == thr[sub]
        tie_bf = jnp.where(tie, 1.0, 0.0).astype(BF16)
        wanted = []
        for c in range(SCORE_TILE // PREFIX_ROWS):
            rank = jnp.dot(tri_ref[...], tie_bf[c * PREFIX_ROWS:(c + 1) * PREFIX_ROWS],
                           preferred_element_type=F32)
            wanted.append(rank <= tie_take[sub] - seen)
            seen = seen + rank[PREFIX_ROWS - 1:PREFIX_ROWS, :]
        sel = jnp.logical_or(k > thr[sub], jnp.logical_and(tie, jnp.concatenate(wanted, axis=0)))
        sel = jnp.logical_and(sel, idx <= qpos[sub])
        rows = pl.ds(pl.multiple_of(PAD_KEYS + tt * SCORE_TILE, TQ), SCORE_TILE)
        maskt_ref[sub, rows, :] = jnp.where(sel, 0.0, NEG_MASK).astype(BF16)
        return seen

    def mask_trip(pp, seen):
        return tuple(mask_tile(sub, 2 * pp + 1, mask_tile(sub, 2 * pp, seen[sub])) for sub in range(2))

    zero_f = jnp.zeros((1, TQ), F32)
    seen = lax.fori_loop(0, n_tiles // 2, mask_trip, (zero_f, zero_f))

    @pl.when(n_tiles % 2 == 1)
    def _():
        for sub in range(2):
            mask_tile(sub, n_tiles - 1, seen[sub])

    for sub in range(2):
        maskt_ref[sub, :PAD_KEYS, :] = jnp.full((PAD_KEYS, TQ), NEG_MASK, BF16)

    eye = (lax.broadcasted_iota(I32, (TQ, TQ), 0) == lax.broadcasted_iota(I32, (TQ, TQ), 1))
    eye = jnp.where(eye, 1.0, 0.0).astype(BF16)
    for sub in range(2):
        for h in range(N_HEADS):
            a_ref[sub, h * TQ:(h + 1) * TQ, :D_CKV] = qlat_ref[sub * TQ:(sub + 1) * TQ,
                                                                h * D_CKV:(h + 1) * D_CKV]
            a_ref[sub, h * TQ:(h + 1) * TQ, D_CKV:] = eye
    n_chunks = ATT_TK // LANES
    n_steps = (q0 + TQ + ATT_TK - 1) // ATT_TK

    def key_rows(sub, j):
        r0 = jnp.maximum(q0 + (sub + 1) * TQ + PAD_KEYS - (j + 1) * ATT_TK, 0)
        return pl.ds(pl.multiple_of(r0, TQ), ATT_TK)

    def logits(sub, j):
        rows = key_rows(sub, j)
        rhs = jnp.concatenate([vaug_ref[rows, :D_CKV], maskt_ref[sub, rows, :]], axis=1)
        s_ref[sub] = lax.dot_general(a_ref[sub], rhs, (((1,), (1,)), ((), ())),
                                     preferred_element_type=F32)

    def consume(sub, j, newest=False):
        vk = vaug_ref[key_rows(sub, j), :]
        ps = []
        for h in range(N_HEADS):
            rows = slice(h * TQ, (h + 1) * TQ)
            sh = s_ref[sub, rows, :]
            if newest:
                sh = jnp.concatenate([sh[:, :ATT_TK - BIAS_TK],
                                      sh[:, ATT_TK - BIAS_TK:] + bias_ref[h]], axis=1)
                m_new = jnp.broadcast_to(jnp.max(sh, axis=1, keepdims=True), (TQ, LANES))
            else:
                m_old = m_ref[sub, rows, :]
                m_new = jnp.maximum(m_old, jnp.max(sh, axis=1, keepdims=True))
                alpha = jnp.exp2(m_old - m_new)
                acc_ref[sub, rows, :] = acc_ref[sub, rows, :] * jnp.concatenate([alpha, alpha], axis=1)
            m_ref[sub, rows, :] = m_new
            p = jnp.exp2(sh - jnp.concatenate([m_new] * n_chunks, axis=1))
            ps.append(p.astype(BF16))
        pv = jnp.dot(jnp.concatenate(ps, axis=0), vk, preferred_element_type=F32)
        if newest:
            acc_ref[sub] = pv
        else:
            acc_ref[sub] += pv

    logits(0, 0)

    logits(1, 0)
    consume(0, 0, newest=True)
    logits(0, 1)
    consume(1, 0, newest=True)

    def att_step(j, carry):
        logits(1, j)
        consume(0, j)
        logits(0, j + 1)
        consume(1, j)
        return carry

    lax.fori_loop(1, n_steps, att_step, 0)

    for sub in range(2):
        for h in range(N_HEADS):
            acc = acc_ref[sub, h * TQ:(h + 1) * TQ, :]
            olat_ref[sub * TQ:(sub + 1) * TQ, h * D_CKV:(h + 1) * D_CKV] = (
                acc[:, :D_CKV] / acc[:, D_CKV:]).astype(BF16)


def _t5_bucket(dist):
    n = jnp.maximum(dist, 0)
    nf = jnp.maximum(n, 1).astype(F32)
    large = MAX_EXACT + (jnp.log(nf / MAX_EXACT) / math.log(MAX_DISTANCE / MAX_EXACT)
                         * (NUM_BUCKETS - MAX_EXACT)).astype(I32)
    large = jnp.minimum(large, NUM_BUCKETS - 1)
    return jnp.where(n < MAX_EXACT, n, large)


def _near_bias(rel_bias):
    assert BIAS_TK >= TQ + MAX_DISTANCE - 1
    period = TQ + BIAS_TK
    e = jnp.arange(period)
    e = jnp.where(e < BIAS_TK, e, e - period)
    tab = rel_bias[_t5_bucket(BIAS_TK - TQ - e)] - rel_bias[NUM_BUCKETS - 1]
    flat = jnp.tile(tab, (TQ, 1))[:TQ * (period - 1)]
    near = flat.reshape(TQ, period - 1, N_HEADS)[:, :BIAS_TK]
    return (jnp.transpose(near, (2, 0, 1)) * LOG2E).astype(F32)


def _attention(qidx, widx, qlat, kidx, ckv, rel_bias, batch, seq):
    assert seq % SCORE_TILE == 0 and SCORE_TILE % TQ == 0 and ATT_TK % (2 * TQ) == 0
    k_sel = min(TOPK_MAX, seq // 4)
    nq = seq // TQ
    r3 = lambda a: a.reshape(batch, seq, a.shape[-1])
    widxt = jnp.transpose(widx.reshape(batch, nq, TQ, IDX_HEADS), (0, 1, 3, 2))
    ckv3 = jnp.pad(r3(ckv), ((0, 0), (PAD_KEYS, 0), (0, 0)))
    vaug = jnp.concatenate([ckv3, jnp.ones_like(ckv3)], axis=-1)
    bias = _near_bias(rel_bias)
    tri = jnp.tril(jnp.ones((PREFIX_ROWS, PREFIX_ROWS), BF16))

    qblk = lambda c: pl.BlockSpec((None, 2 * TQ, c), lambda b, i: (b, i, 0))
    qtile = lambda a: pl.BlockSpec((None, 2) + a.shape[2:], lambda b, i: (b, i, 0, 0))
    per_b = lambda a: pl.BlockSpec((None,) + a.shape[1:], lambda b, i: (b,) + (0,) * (a.ndim - 1),
                                   pipeline_mode=pl.Buffered(1))
    n_rows = N_HEADS * TQ
    kidx3 = r3(kidx)
    out = pl.pallas_call(
        functools.partial(_attn_kernel, k_sel=k_sel),
        grid=(batch, nq // 2),
        in_specs=[qblk(IDX_HEADS * IDX_DIM), qtile(widxt), qblk(N_HEADS * D_CKV), per_b(kidx3), per_b(vaug),
                  pl.BlockSpec(bias.shape, lambda b, i: (0, 0, 0),
                               pipeline_mode=pl.Buffered(1)),
                  pl.BlockSpec(tri.shape, lambda b, i: (0, 0))],
        out_specs=qblk(N_HEADS * D_CKV),
        out_shape=jax.ShapeDtypeStruct((batch, seq, N_HEADS * D_CKV), BF16),
        scratch_shapes=[
            pltpu.VMEM((2, seq, TQ), I32),
            pltpu.VMEM((2, seq + PAD_KEYS, TQ), BF16),
            pltpu.VMEM((2, n_rows, 2 * D_CKV), BF16),
            pltpu.VMEM((2, n_rows, 2 * D_CKV), F32),
            pltpu.VMEM((2, n_rows, LANES), F32),
            pltpu.VMEM((2, n_rows, ATT_TK), F32),
            pltpu.VMEM((2, SCORE_TILE, IDX_HEADS * TQ), F32),
            pltpu.VMEM((IDX_HEADS * TQ, IDX_DIM), BF16),
        ],
        compiler_params=_cparams(2),
        name="dsa_attn",
    )(r3(qidx), widxt, r3(qlat), kidx3, vaug, bias, tri)
    return out.reshape(batch * seq, N_HEADS * D_CKV)


def _mlp_kernel(*refs, f_chunk, final_norm, with_attn_out):
    if with_attn_out:
        x_ref, olat_ref, wuv_ref, wo_ref, g_ref, wup_ref, wdn_ref, gf_ref, y_ref, acc_ref = refs
        pair = 2 * D_CKV
        o = [jnp.dot(olat_ref[:, p * pair:(p + 1) * pair], wuv_ref[p], preferred_element_type=F32)
             for p in range(N_HEADS // 2)]
        o = jnp.concatenate(o, axis=1).astype(BF16)
        x = x_ref[...] + jnp.dot(o, wo_ref[...], preferred_element_type=F32)
    else:
        x_ref, g_ref, wup_ref, wdn_ref, gf_ref, y_ref, acc_ref = refs
        x = x_ref[...]
    h = _rms(x, g_ref[...]).astype(BF16)
    d_ff = wup_ref.shape[1]
    for c in range(d_ff // f_chunk):
        u = jnp.dot(h, wup_ref[:, c * f_chunk:(c + 1) * f_chunk], preferred_element_type=F32)
        a = jnp.square(jnp.maximum(u, 0.0)).astype(BF16)
        d = jnp.dot(a, wdn_ref[c * f_chunk:(c + 1) * f_chunk, :], preferred_element_type=F32)
        if c == 0:
            acc_ref[...] = d
        else:
            acc_ref[...] += d
    y = x + acc_ref[...]
    if final_norm:
        y = _rms(y, gf_ref[...])
    y_ref[...] = y


def _mlp(x2, g, w_up, w_down, g_final, final_norm, attn_out=None, tm=512, f_chunk=512):
    n = x2.shape[0]
    const = lambda a: pl.BlockSpec(a.shape, lambda i: (0,) * a.ndim, pipeline_mode=pl.Buffered(1))
    rows = lambda c: pl.BlockSpec((tm, c), lambda i: (i, 0))
    args, specs = [x2], [rows(D_MODEL)]
    if attn_out is not None:
        olat, w_uv, w_o = attn_out
        wuv = jnp.transpose(w_uv, (1, 0, 2)).reshape(N_HEADS // 2, 2, D_CKV, D_V)
        eye = jnp.eye(2, dtype=w_uv.dtype)
        wbd = (wuv[:, :, :, None, :] * eye[None, :, None, :, None]).reshape(
            N_HEADS // 2, 2 * D_CKV, 2 * D_V)
        extra = [olat, wbd.astype(BF16), w_o.astype(BF16)]
        args += extra
        specs += [rows(N_HEADS * D_CKV), const(extra[1]), const(extra[2])]
    weights = [g.reshape(1, -1), w_up.astype(BF16), w_down.astype(BF16), g_final.reshape(1, -1)]
    args += weights
    specs += [const(a) for a in weights]
    return pl.pallas_call(
        functools.partial(_mlp_kernel, f_chunk=f_chunk, final_norm=final_norm,
                          with_attn_out=attn_out is not None),
        grid=(n // tm,),
        in_specs=specs,
        out_specs=rows(D_MODEL),
        out_shape=jax.ShapeDtypeStruct((n, D_MODEL), F32),
        scratch_shapes=[pltpu.VMEM((tm, D_MODEL), F32)],
        compiler_params=_cparams(1),
        name="mlp_final" if final_norm else "mlp",
    )(*args)


def _glu_kernel(x_ref, g_ref, w_ref, b_ref, u_ref):
    h = _rms(x_ref[...], g_ref[...]).astype(BF16)
    u = jnp.dot(h, w_ref[...], preferred_element_type=F32) + b_ref[...]
    d = u.shape[1] // 2
    u_ref[...] = u[:, :d] * jax.nn.sigmoid(u[:, d:])


def _glu(x2, g, w_pw1, b_pw1, tm=1024):
    n = x2.shape[0]
    full = lambda a: pl.BlockSpec(a.shape, lambda i: (0,) * a.ndim)
    args = (x2, g.reshape(1, -1), w_pw1.astype(BF16), b_pw1.reshape(1, -1))
    return pl.pallas_call(
        _glu_kernel,
        grid=(n // tm,),
        in_specs=[pl.BlockSpec((tm, D_MODEL), lambda i: (i, 0))] + [full(a) for a in args[1:]],
        out_specs=pl.BlockSpec((tm, D_MODEL), lambda i: (i, 0)),
        out_shape=jax.ShapeDtypeStruct((n, D_MODEL), F32),
        compiler_params=_cparams(1),
        name="conv_glu",
    )(*args)


HALO = 32
CONV_ROWS = 128
SUBLANE_PAD = 8


def _dwconv_kernel(x_ref, u_ref, halo_ref, wdw_ref, bdw_ref, lng_ref, lnb_ref, w2_ref, b2_ref, y_ref,
                   ext_ref, cv_ref):
    i = pl.program_id(1)
    tm = u_ref.shape[0]
    ext_ref[:HALO, :] = jnp.where(i == 0, 0.0, halo_ref[...])
    ext_ref[HALO:HALO + tm, :] = u_ref[...]
    ext_ref[HALO + tm:, :] = jnp.zeros((SUBLANE_PAD, D_MODEL), F32)
    off = HALO - (CONV_WIDTH - 1)

    def rows_body(r, carry):
        r0 = pl.multiple_of(r * CONV_ROWS, CONV_ROWS)
        for lc in range(D_MODEL // LANES):
            lanes = slice(lc * LANES, (lc + 1) * LANES)
            acc = jnp.broadcast_to(bdw_ref[:, lanes], (CONV_ROWS, LANES))
            for s in range(SUBLANE_PAD):
                part = None
                for j in range(CONV_WIDTH):
                    if (off + j) % SUBLANE_PAD != s:
                        continue
                    rows = pl.ds(pl.multiple_of(r0 + (off + j - s), SUBLANE_PAD),
                                 CONV_ROWS + SUBLANE_PAD)
                    term = wdw_ref[j:j + 1, lanes] * ext_ref[rows, lanes]
                    part = term if part is None else part + term
                if part is not None:
                    acc = acc + part[s:s + CONV_ROWS]
            cv_ref[pl.ds(r0, CONV_ROWS), lanes] = acc
        return carry

    lax.fori_loop(0, tm // CONV_ROWS, rows_body, 0)
    v = cv_ref[...]
    mu = jnp.mean(v, axis=-1, keepdims=True)
    var = jnp.mean(jnp.square(v - mu), axis=-1, keepdims=True)
    v = (v - mu) * lax.rsqrt(var + EPS) * lng_ref[...] + lnb_ref[...]
    v = v * jax.nn.sigmoid(v)
    y_ref[...] = (x_ref[...] + jnp.dot(v.astype(BF16), w2_ref[...], preferred_element_type=F32)
                  + b2_ref[...])


def _dwconv(x2, u2, w_dw, b_dw, ln_g, ln_b, w_pw2, b_pw2, batch, seq, tm=1024):
    x3 = x2.reshape(batch, seq, D_MODEL)
    u3 = u2.reshape(batch, seq, D_MODEL)
    hb = tm // HALO
    blk = pl.BlockSpec((None, tm, D_MODEL), lambda b, i: (b, i, 0))
    halo = pl.BlockSpec((None, HALO, D_MODEL), lambda b, i: (b, jnp.maximum(i * hb - 1, 0), 0))
    full = lambda a: pl.BlockSpec(a.shape, lambda b, i: (0,) * a.ndim)
    args = (x3, u3, u3, w_dw, b_dw.reshape(1, -1), ln_g.reshape(1, -1), ln_b.reshape(1, -1),
            w_pw2.astype(BF16), b_pw2.reshape(1, -1))
    out = pl.pallas_call(
        _dwconv_kernel,
        grid=(batch, seq // tm),
        in_specs=[blk, blk, halo] + [full(a) for a in args[3:]],
        out_specs=blk,
        out_shape=jax.ShapeDtypeStruct((batch, seq, D_MODEL), F32),
        scratch_shapes=[pltpu.VMEM((HALO + tm + SUBLANE_PAD, D_MODEL), F32),
                        pltpu.VMEM((tm, D_MODEL), F32)],
        compiler_params=_cparams(2),
        name="conv_dw",
    )(*args)
    return out.reshape(batch * seq, D_MODEL)


def kernel(x, norm_mix, norm_mlp, norm_final, rel_bias, attn_w_in, attn_q_norm, attn_kv_norm, attn_kidx_norm, attn_w_qidx, attn_w_uq, attn_w_uk, attn_w_uv, attn_w_o, conv_w_pw1, conv_b_pw1, conv_w_dw, conv_b_dw, conv_ln_g, conv_ln_b, conv_w_pw2, conv_b_pw2, mlp_w_up, mlp_w_down):
    batch, seq, d = x.shape
    depth = norm_mix.shape[0]
    x2 = x.reshape(batch * seq, d)
    for i in range(depth):
        j = i // 2
        if i % 2 == 0:
            qidx, widx, qlat, kidx, ckv = _proj(
                x2, norm_mix[i], attn_w_in[j], attn_q_norm[j], attn_kv_norm[j], attn_kidx_norm[j],
                attn_w_qidx[j], attn_w_uq[j], attn_w_uk[j])
            olat = _attention(qidx, widx, qlat, kidx, ckv, rel_bias, batch, seq)
            mixer_out = (olat, attn_w_uv[j], attn_w_o[j])
        else:
            mixer_out = None
            u = _glu(x2, norm_mix[i], conv_w_pw1[j], conv_b_pw1[j])
            x2 = _dwconv(x2, u, conv_w_dw[j], conv_b_dw[j], conv_ln_g[j], conv_ln_b[j],
                         conv_w_pw2[j], conv_b_pw2[j], batch, seq)
        last = i == depth - 1
        x2 = _mlp(x2, norm_mlp[i], mlp_w_up[i], mlp_w_down[i], norm_final, final_norm=last,
                  attn_out=mixer_out)
    if depth == 0:
        raise ValueError("depth must be positive")
    return x2.reshape(batch, seq, d)
```

```python
import functools
import math

import jax
import jax.numpy as jnp
from jax import lax
from jax.experimental import pallas as pl
from jax.experimental.pallas import tpu as pltpu

F32 = jnp.float32
BF16 = jnp.bfloat16
I32 = jnp.int32

D_MODEL = 1024
N_HEADS = 16
D_NOPE = 64
D_V = 64
D_CQ = 256
D_CKV = 128
IDX_HEADS = 8
IDX_DIM = 64
TOPK_MAX = 256
CONV_WIDTH = 31
NUM_BUCKETS = 32
MAX_EXACT = 16
MAX_DISTANCE = 128
EPS = 1e-6

LANES = 128
SUBLANES = 8
TQ = 128
SCORE_TILE = 512
ATT_TK = 512
BIAS_TK = 2 * TQ
PAD_KEYS = ATT_TK - TQ
NEG_MASK = -1e30
LOG2E = math.log2(math.e)
FIXED_BITS = 22
BITS_PER_TEST = 1
PREFIX_ROWS = 256
VMEM_LIMIT = 56 * 1024 * 1024


def _cparams(n_axes):
    return pltpu.CompilerParams(dimension_semantics=("arbitrary",) * n_axes,
                                vmem_limit_bytes=VMEM_LIMIT)


def _rms(x, g):
    return x * lax.rsqrt(jnp.mean(x * x, axis=-1, keepdims=True) + EPS) * g


def _proj_kernel(x_ref, g_ref, win_ref, qn_ref, kvn_ref, kin_ref, wqidx_ref, wuq_ref, wukt_ref,
                 qidx_ref, widx_ref, qlat_ref, kidx_ref, ckv_ref):
    h = _rms(x_ref[...], g_ref[...])
    proj = jnp.dot(h.astype(BF16), win_ref[...], preferred_element_type=F32)
    o1, o2, o3 = D_CQ, D_CQ + D_CKV, D_CQ + D_CKV + IDX_DIM
    cq = _rms(proj[:, :o1], qn_ref[...])
    ckv = _rms(proj[:, o1:o2], kvn_ref[...])
    kid = _rms(proj[:, o2:o3], kin_ref[...])
    widx_ref[...] = proj[:, o3:o3 + IDX_HEADS] * (IDX_HEADS ** -0.5)
    ckv_ref[...] = ckv.astype(BF16)
    kidx_ref[...] = kid.astype(BF16)
    cqb = cq.astype(BF16)
    qidx = jnp.dot(cqb, wqidx_ref[...], preferred_element_type=F32) * (IDX_DIM ** -0.5)
    qidx_ref[...] = qidx.astype(BF16)
    qh = jnp.dot(cqb, wuq_ref[...], preferred_element_type=F32).astype(BF16)
    for hp in range(N_HEADS // 2):
        ql = jnp.dot(qh[:, hp * 2 * D_NOPE:(hp + 1) * 2 * D_NOPE], wukt_ref[hp],
                     preferred_element_type=F32) * (D_NOPE ** -0.5 * LOG2E)
        qlat_ref[:, hp * 2 * D_CKV:(hp + 1) * 2 * D_CKV] = ql.astype(BF16)


def _proj(x2, g, w_in, qn, kvn, kin, w_qidx, w_uq, w_uk, tm=1024):
    n = x2.shape[0]
    ncol = w_in.shape[1]
    npad = -ncol % LANES
    win = jnp.pad(w_in, ((0, 0), (0, npad))).astype(BF16)
    wukt = jnp.transpose(w_uk, (1, 2, 0)).reshape(N_HEADS // 2, 2, D_NOPE, D_CKV)
    eye = jnp.eye(2, dtype=w_uk.dtype)
    wukt = (wukt[:, :, :, None, :] * eye[None, :, None, :, None]).reshape(
        N_HEADS // 2, 2 * D_NOPE, 2 * D_CKV).astype(BF16)
    full = lambda a: pl.BlockSpec(a.shape, lambda i: (0,) * a.ndim)
    row = lambda c: pl.BlockSpec((tm, c), lambda i: (i, 0))
    args = (x2, g.reshape(1, -1), win, qn.reshape(1, -1), kvn.reshape(1, -1), kin.reshape(1, -1),
            w_qidx.astype(BF16), w_uq.astype(BF16), wukt)
    return pl.pallas_call(
        _proj_kernel,
        grid=(n // tm,),
        in_specs=[row(D_MODEL)] + [full(a) for a in args[1:]],
        out_specs=[row(IDX_HEADS * IDX_DIM), row(IDX_HEADS), row(N_HEADS * D_CKV), row(IDX_DIM),
                   row(D_CKV)],
        out_shape=[jax.ShapeDtypeStruct((n, IDX_HEADS * IDX_DIM), BF16),
                   jax.ShapeDtypeStruct((n, IDX_HEADS), F32),
                   jax.ShapeDtypeStruct((n, N_HEADS * D_CKV), BF16),
                   jax.ShapeDtypeStruct((n, IDX_DIM), BF16),
                   jax.ShapeDtypeStruct((n, D_CKV), BF16)],
        compiler_params=_cparams(1),
        name="dsa_proj",
    )(*args)


def _attn_kernel(qidx_ref, widxt_ref, qlat_ref, kidx_ref, vaug_ref, bias_ref, tri_ref, olat_ref,
                 keys_ref, maskt_ref, a_ref, acc_ref, m_ref, s_ref, qk_ref, qs_ref, *, k_sel):
    n_rows = N_HEADS * TQ
    q0 = 2 * pl.program_id(1) * TQ

    n_tiles = q0 // SCORE_TILE + 1
    krow = lax.broadcasted_iota(I32, (SCORE_TILE, TQ), 0)
    qpos = [q0 + sub * TQ + lax.broadcasted_iota(I32, (1, TQ), 1) for sub in range(2)]
    trivial = [qp < k_sel for qp in qpos]

    def tile_rows(tt):
        return pl.ds(pl.multiple_of(tt * SCORE_TILE, SCORE_TILE), SCORE_TILE)

    def score_pass(sub):
        for h in range(IDX_HEADS):
            qs_ref[h * TQ:(h + 1) * TQ, :] = qidx_ref[sub * TQ:(sub + 1) * TQ,
                                                      h * IDX_DIM:(h + 1) * IDX_DIM]

        def score_dot(tt, slot):
            rows = tile_rows(jnp.minimum(tt, keys_ref.shape[1] // SCORE_TILE - 1))
            qk_ref[slot] = lax.dot_general(kidx_ref[rows, :], qs_ref[...], (((1,), (1,)), ((), ())),
                                           preferred_element_type=F32)

        def score_keys(tt, slot):
            sc = jnp.zeros((SCORE_TILE, TQ), F32)
            for h in range(IDX_HEADS):
                sc = sc + widxt_ref[sub, h:h + 1, :] * jnp.maximum(
                    qk_ref[slot, :, h * TQ:(h + 1) * TQ], 0.0)
            sc = jnp.where(tt * SCORE_TILE + krow <= qpos[sub], sc, -jnp.inf)
            bits = pltpu.bitcast(sc, I32)
            keys_ref[sub, tile_rows(tt), :] = bits ^ ((bits >> 31) & 0x7FFFFFFF)

        score_dot(0, 0)

        def score_pair(pp, carry):
            tt = 2 * pp
            score_dot(tt + 1, 1)
            score_keys(tt, 0)
            score_dot(tt + 2, 0)
            score_keys(tt + 1, 1)
            return carry

        lax.fori_loop(0, n_tiles // 2, score_pair, 0)

        @pl.when(n_tiles % 2 == 1)
        def _():
            score_keys(n_tiles - 1, 0)

    for sub in range(2):
        score_pass(sub)

    def tile_reduce(x, op):
        return op(x.reshape(SCORE_TILE // SUBLANES, SUBLANES, TQ), axis=0)

    def count_ge(thr):
        def body(tt, acc):
            return tuple(acc[sub] + tile_reduce(
                jnp.where(keys_ref[sub, tile_rows(tt), :] >= thr[sub], 1, 0), jnp.sum)
                for sub in range(2))
        zero = jnp.zeros((SUBLANES, TQ), I32)
        acc = lax.fori_loop(0, n_tiles, body, (zero, zero))
        return tuple(jnp.sum(a, axis=0, keepdims=True) for a in acc)

    def bit_step(step, st):
        trial = tuple(s[0] ^ (jnp.int32(1) << (31 - step)) for s in st)
        cnt = count_ge(trial)
        out = []
        for sub in range(2):
            cand, cntc, cnt_ub = st[sub]
            ok = cnt[sub] >= k_sel
            out.append((jnp.where(ok, trial[sub], cand), jnp.where(ok, cnt[sub], cntc),
                        jnp.where(ok, cnt_ub, cnt[sub])))
        return tuple(out)

    def resolve(step, st, tval, res):
        top = tuple(s[0] | ((jnp.int32(1) << (32 - step)) - 1) for s in st)

        def body(tt, c):
            out = []
            for sub in range(2):
                k = keys_ref[sub, tile_rows(tt), :]
                inb = jnp.logical_and(k >= st[sub][0], k <= top[sub])
                out.append((jnp.minimum(c[sub][0], tile_reduce(jnp.where(inb, k, 2 ** 31 - 1), jnp.min)),
                            jnp.maximum(c[sub][1], tile_reduce(jnp.where(inb, k, -2 ** 31), jnp.max))))
            return tuple(out)

        init = (jnp.full((SUBLANES, TQ), 2 ** 31 - 1, I32), jnp.full((SUBLANES, TQ), -2 ** 31, I32))
        mnmx = lax.fori_loop(0, n_tiles, body, (init, init))
        tvals, ress, n_open = [], [], 0
        for sub in range(2):
            _, cntc, cnt_ub = st[sub]
            mn = jnp.min(mnmx[sub][0], axis=0, keepdims=True)
            mx = jnp.max(mnmx[sub][1], axis=0, keepdims=True)
            known = jnp.logical_or(mn == mx, cnt_ub == k_sel - 1)
            tvals.append(jnp.where(known, mx, tval[sub]))
            ress.append(jnp.where(known, 1, res[sub]))
            settled = jnp.logical_or(jnp.logical_or(ress[sub] > 0, cntc == k_sel), trivial[sub])
            n_open = n_open + jnp.sum(jnp.where(settled, 0, 1))
        return tuple(tvals), tuple(ress), n_open

    st0 = (jnp.full((1, TQ), -2 ** 31, I32), jnp.full((1, TQ), 2 ** 30, I32), jnp.zeros((1, TQ), I32))
    st = lax.fori_loop(0, FIXED_BITS, bit_step, (st0, st0))
    zeros = jnp.zeros((1, TQ), I32)
    tval, res, n_open = resolve(FIXED_BITS, st, (st[0][0], st[1][0]), (zeros, zeros))

    def search_body(c):
        step, st, tval, res, _ = c
        for b in range(BITS_PER_TEST):
            st = bit_step(step + b, st)
        tval, res, n_open = resolve(step + BITS_PER_TEST, st, tval, res)
        return step + BITS_PER_TEST, st, tval, res, n_open

    _, st, tval, res, _ = lax.while_loop(
        lambda c: jnp.logical_and(c[0] < 32, c[4] > 0), search_body,
        (jnp.int32(FIXED_BITS), st, tval, res, n_open))

    thr, tie_take = [], []
    for sub in range(2):
        cand, cntc, cnt_ub = st[sub]
        exact = jnp.logical_or(cntc == k_sel, trivial[sub])
        t = jnp.where(exact, cand, jnp.where(res[sub] > 0, tval[sub], cand))
        thr.append(jnp.where(trivial[sub], -2 ** 31, t))
        tie_take.append(jnp.where(exact, 2.0 ** 30, (k_sel - cnt_ub).astype(F32)))

    def mask_tile(sub, tt, seen):
        k = keys_ref[sub, tile_rows(tt), :]
        idx = tt * SCORE_TILE + krow
        tie = k == thr[sub]
        tie_bf = jnp.where(tie, 1.0, 0.0).astype(BF16)
        wanted = []
        for c in range(SCORE_TILE // PREFIX_ROWS):
            rank = jnp.dot(tri_ref[...], tie_bf[c * PREFIX_ROWS:(c + 1) * PREFIX_ROWS],
                           preferred_element_type=F32)
            wanted.append(rank <= tie_take[sub] - seen)
            seen = seen + rank[PREFIX_ROWS - 1:PREFIX_ROWS, :]
        sel = jnp.logical_or(k > thr[sub], jnp.logical_and(tie, jnp.concatenate(wanted, axis=0)))
        sel = jnp.logical_and(sel, idx <= qpos[sub])
        rows = pl.ds(pl.multiple_of(PAD_KEYS + tt * SCORE_TILE, TQ), SCORE_TILE)
        maskt_ref[sub, rows, :] = jnp.where(sel, 0.0, NEG_MASK).astype(BF16)
        return seen

    def mask_trip(pp, seen):
        return tuple(mask_tile(sub, 2 * pp + 1, mask_tile(sub, 2 * pp, seen[sub])) for sub in range(2))

    zero_f = jnp.zeros((1, TQ), F32)
    seen = lax.fori_loop(0, n_tiles // 2, mask_trip, (zero_f, zero_f))

    @pl.when(n_tiles % 2 == 1)
    def _():
        for sub in range(2):
            mask_tile(sub, n_tiles - 1, seen[sub])

    for sub in range(2):
        maskt_ref[sub, :PAD_KEYS, :] = jnp.full((PAD_KEYS, TQ), NEG_MASK, BF16)

    eye = (lax.broadcasted_iota(I32, (TQ, TQ), 0) == lax.broadcasted_iota(I32, (TQ, TQ), 1))
    eye = jnp.where(eye, 1.0, 0.0).astype(BF16)
    for sub in range(2):
        for h in range(N_HEADS):
            a_ref[sub, h * TQ:(h + 1) * TQ, :D_CKV] = qlat_ref[sub * TQ:(sub + 1) * TQ,
                                                                h * D_CKV:(h + 1) * D_CKV]
            a_ref[sub, h * TQ:(h + 1) * TQ, D_CKV:] = eye
    n_chunks = ATT_TK // LANES
    n_steps = (q0 + TQ + ATT_TK - 1) // ATT_TK

    def key_rows(sub, j):
        r0 = jnp.maximum(q0 + (sub + 1) * TQ + PAD_KEYS - (j + 1) * ATT_TK, 0)
        return pl.ds(pl.multiple_of(r0, TQ), ATT_TK)

    def logits(sub, j):
        rows = key_rows(sub, j)
        rhs = jnp.concatenate([vaug_ref[rows, :D_CKV], maskt_ref[sub, rows, :]], axis=1)
        s_ref[sub] = lax.dot_general(a_ref[sub], rhs, (((1,), (1,)), ((), ())),
                                     preferred_element_type=F32)

    def consume(sub, j, newest=False):
        vk = vaug_ref[key_rows(sub, j), :]
        ps = []
        for h in range(N_HEADS):
            rows = slice(h * TQ, (h + 1) * TQ)
            sh = s_ref[sub, rows, :]
            if newest:
                sh = jnp.concatenate([sh[:, :ATT_TK - BIAS_TK],
                                      sh[:, ATT_TK - BIAS_TK:] + bias_ref[h]], axis=1)
                m_new = jnp.broadcast_to(jnp.max(sh, axis=1, keepdims=True), (TQ, LANES))
            else:
                m_old = m_ref[sub, rows, :]
                m_new = jnp.maximum(m_old, jnp.max(sh, axis=1, keepdims=True))
                alpha = jnp.exp2(m_old - m_new)
                acc_ref[sub, rows, :] = acc_ref[sub, rows, :] * jnp.concatenate([alpha, alpha], axis=1)
            m_ref[sub, rows, :] = m_new
            p = jnp.exp2(sh - jnp.concatenate([m_new] * n_chunks, axis=1))
            ps.append(p.astype(BF16))
        pv = jnp.dot(jnp.concatenate(ps, axis=0), vk, preferred_element_type=F32)
        if newest:
            acc_ref[sub] = pv
        else:
            acc_ref[sub] += pv

    logits(0, 0)

    logits(1, 0)
    consume(0, 0, newest=True)
    logits(0, 1)
    consume(1, 0, newest=True)

    def att_step(j, carry):
        logits(1, j)
        consume(0, j)
        logits(0, j + 1)
        consume(1, j)
        return carry

    lax.fori_loop(1, n_steps, att_step, 0)

    for sub in range(2):
        for h in range(N_HEADS):
            acc = acc_ref[sub, h * TQ:(h + 1) * TQ, :]
            olat_ref[sub * TQ:(sub + 1) * TQ, h * D_CKV:(h + 1) * D_CKV] = (
                acc[:, :D_CKV] / acc[:, D_CKV:]).astype(BF16)


def _t5_bucket(dist):
    n = jnp.maximum(dist, 0)
    nf = jnp.maximum(n, 1).astype(F32)
    large = MAX_EXACT + (jnp.log(nf / MAX_EXACT) / math.log(MAX_DISTANCE / MAX_EXACT)
                         * (NUM_BUCKETS - MAX_EXACT)).astype(I32)
    large = jnp.minimum(large, NUM_BUCKETS - 1)
    return jnp.where(n < MAX_EXACT, n, large)


def _near_bias(rel_bias):
    assert BIAS_TK >= TQ + MAX_DISTANCE - 1
    period = TQ + BIAS_TK
    e = jnp.arange(period)
    e = jnp.where(e < BIAS_TK, e, e - period)
    tab = rel_bias[_t5_bucket(BIAS_TK - TQ - e)] - rel_bias[NUM_BUCKETS - 1]
    flat = jnp.tile(tab, (TQ, 1))[:TQ * (period - 1)]
    near = flat.reshape(TQ, period - 1, N_HEADS)[:, :BIAS_TK]
    return (jnp.transpose(near, (2, 0, 1)) * LOG2E).astype(F32)


def _attention(qidx, widx, qlat, kidx, ckv, rel_bias, batch, seq):
    assert seq % SCORE_TILE == 0 and SCORE_TILE % TQ == 0 and ATT_TK % (2 * TQ) == 0
    k_sel = min(TOPK_MAX, seq // 4)
    nq = seq // TQ
    r3 = lambda a: a.reshape(batch, seq, a.shape[-1])
    widxt = jnp.transpose(widx.reshape(batch, nq, TQ, IDX_HEADS), (0, 1, 3, 2))
    ckv3 = jnp.pad(r3(ckv), ((0, 0), (PAD_KEYS, 0), (0, 0)))
    vaug = jnp.concatenate([ckv3, jnp.ones_like(ckv3)], axis=-1)
    bias = _near_bias(rel_bias)
    tri = jnp.tril(jnp.ones((PREFIX_ROWS, PREFIX_ROWS), BF16))

    qblk = lambda c: pl.BlockSpec((None, 2 * TQ, c), lambda b, i: (b, i, 0))
    qtile = lambda a: pl.BlockSpec((None, 2) + a.shape[2:], lambda b, i: (b, i, 0, 0))
    per_b = lambda a: pl.BlockSpec((None,) + a.shape[1:], lambda b, i: (b,) + (0,) * (a.ndim - 1),
                                   pipeline_mode=pl.Buffered(1))
    n_rows = N_HEADS * TQ
    kidx3 = r3(kidx)
    out = pl.pallas_call(
        functools.partial(_attn_kernel, k_sel=k_sel),
        grid=(batch, nq // 2),
        in_specs=[qblk(IDX_HEADS * IDX_DIM), qtile(widxt), qblk(N_HEADS * D_CKV), per_b(kidx3), per_b(vaug),
                  pl.BlockSpec(bias.shape, lambda b, i: (0, 0, 0),
                               pipeline_mode=pl.Buffered(1)),
                  pl.BlockSpec(tri.shape, lambda b, i: (0, 0))],
        out_specs=qblk(N_HEADS * D_CKV),
        out_shape=jax.ShapeDtypeStruct((batch, seq, N_HEADS * D_CKV), BF16),
        scratch_shapes=[
            pltpu.VMEM((2, seq, TQ), I32),
            pltpu.VMEM((2, seq + PAD_KEYS, TQ), BF16),
            pltpu.VMEM((2, n_rows, 2 * D_CKV), BF16),
            pltpu.VMEM((2, n_rows, 2 * D_CKV), F32),
            pltpu.VMEM((2, n_rows, LANES), F32),
            pltpu.VMEM((2, n_rows, ATT_TK), F32),
            pltpu.VMEM((2, SCORE_TILE, IDX_HEADS * TQ), F32),
            pltpu.VMEM((IDX_HEADS * TQ, IDX_DIM), BF16),
        ],
        compiler_params=_cparams(2),
        name="dsa_attn",
    )(r3(qidx), widxt, r3(qlat), kidx3, vaug, bias, tri)
    return out.reshape(batch * seq, N_HEADS * D_CKV)


def _mlp_kernel(*refs, f_chunk, final_norm, with_attn_out):
    if with_attn_out:
        x_ref, olat_ref, wuv_ref, wo_ref, g_ref, wup_ref, wdn_ref, gf_ref, y_ref, acc_ref = refs
        pair = 2 * D_CKV
        o = [jnp.dot(olat_ref[:, p * pair:(p + 1) * pair], wuv_ref[p], preferred_element_type=F32)
             for p in range(N_HEADS // 2)]
        o = jnp.concatenate(o, axis=1).astype(BF16)
        x = x_ref[...] + jnp.dot(o, wo_ref[...], preferred_element_type=F32)
    else:
        x_ref, g_ref, wup_ref, wdn_ref, gf_ref, y_ref, acc_ref = refs
        x = x_ref[...]
    h = _rms(x, g_ref[...]).astype(BF16)
    d_ff = wup_ref.shape[1]
    for c in range(d_ff // f_chunk):
        u = jnp.dot(h, wup_ref[:, c * f_chunk:(c + 1) * f_chunk], preferred_element_type=F32)
        a = jnp.square(jnp.maximum(u, 0.0)).astype(BF16)
        d = jnp.dot(a, wdn_ref[c * f_chunk:(c + 1) * f_chunk, :], preferred_element_type=F32)
        if c == 0:
            acc_ref[...] = d
        else:
            acc_ref[...] += d
    y = x + acc_ref[...]
    if final_norm:
        y = _rms(y, gf_ref[...])
    y_ref[...] = y


def _mlp(x2, g, w_up, w_down, g_final, final_norm, attn_out=None, tm=512, f_chunk=512):
    n = x2.shape[0]
    const = lambda a: pl.BlockSpec(a.shape, lambda i: (0,) * a.ndim, pipeline_mode=pl.Buffered(1))
    rows = lambda c: pl.BlockSpec((tm, c), lambda i: (i, 0))
    args, specs = [x2], [rows(D_MODEL)]
    if attn_out is not None:
        olat, w_uv, w_o = attn_out
        wuv = jnp.transpose(w_uv, (1, 0, 2)).reshape(N_HEADS // 2, 2, D_CKV, D_V)
        eye = jnp.eye(2, dtype=w_uv.dtype)
        wbd = (wuv[:, :, :, None, :] * eye[None, :, None, :, None]).reshape(
            N_HEADS // 2, 2 * D_CKV, 2 * D_V)
        extra = [olat, wbd.astype(BF16), w_o.astype(BF16)]
        args += extra
        specs += [rows(N_HEADS * D_CKV), const(extra[1]), const(extra[2])]
    weights = [g.reshape(1, -1), w_up.astype(BF16), w_down.astype(BF16), g_final.reshape(1, -1)]
    args += weights
    specs += [const(a) for a in weights]
    return pl.pallas_call(
        functools.partial(_mlp_kernel, f_chunk=f_chunk, final_norm=final_norm,
                          with_attn_out=attn_out is not None),
        grid=(n // tm,),
        in_specs=specs,
        out_specs=rows(D_MODEL),
        out_shape=jax.ShapeDtypeStruct((n, D_MODEL), F32),
        scratch_shapes=[pltpu.VMEM((tm, D_MODEL), F32)],
        compiler_params=_cparams(1),
        name="mlp_final" if final_norm else "mlp",
    )(*args)


def _glu_kernel(x_ref, g_ref, w_ref, b_ref, u_ref):
    h = _rms(x_ref[...], g_ref[...]).astype(BF16)
    u = jnp.dot(h, w_ref[...], preferred_element_type=F32) + b_ref[...]
    d = u.shape[1] // 2
    u_ref[...] = u[:, :d] * jax.nn.sigmoid(u[:, d:])


def _glu(x2, g, w_pw1, b_pw1, tm=1024):
    n = x2.shape[0]
    full = lambda a: pl.BlockSpec(a.shape, lambda i: (0,) * a.ndim)
    args = (x2, g.reshape(1, -1), w_pw1.astype(BF16), b_pw1.reshape(1, -1))
    return pl.pallas_call(
        _glu_kernel,
        grid=(n // tm,),
        in_specs=[pl.BlockSpec((tm, D_MODEL), lambda i: (i, 0))] + [full(a) for a in args[1:]],
        out_specs=pl.BlockSpec((tm, D_MODEL), lambda i: (i, 0)),
        out_shape=jax.ShapeDtypeStruct((n, D_MODEL), F32),
        compiler_params=_cparams(1),
        name="conv_glu",
    )(*args)


HALO = 32
CONV_ROWS = 128
SUBLANE_PAD = 8


def _dwconv_kernel(x_ref, u_ref, halo_ref, wdw_ref, bdw_ref, lng_ref, lnb_ref, w2_ref, b2_ref, y_ref,
                   ext_ref, cv_ref):
    i = pl.program_id(1)
    tm = u_ref.shape[0]
    ext_ref[:HALO, :] = jnp.where(i == 0, 0.0, halo_ref[...])
    ext_ref[HALO:HALO + tm, :] = u_ref[...]
    ext_ref[HALO + tm:, :] = jnp.zeros((SUBLANE_PAD, D_MODEL), F32)
    off = HALO - (CONV_WIDTH - 1)

    def rows_body(r, carry):
        r0 = pl.multiple_of(r * CONV_ROWS, CONV_ROWS)
        for lc in range(D_MODEL // LANES):
            lanes = slice(lc * LANES, (lc + 1) * LANES)
            acc = jnp.broadcast_to(bdw_ref[:, lanes], (CONV_ROWS, LANES))
            for s in range(SUBLANE_PAD):
                part = None
                for j in range(CONV_WIDTH):
                    if (off + j) % SUBLANE_PAD != s:
                        continue
                    rows = pl.ds(pl.multiple_of(r0 + (off + j - s), SUBLANE_PAD),
                                 CONV_ROWS + SUBLANE_PAD)
                    term = wdw_ref[j:j + 1, lanes] * ext_ref[rows, lanes]
                    part = term if part is None else part + term
                if part is not None:
                    acc = acc + part[s:s + CONV_ROWS]
            cv_ref[pl.ds(r0, CONV_ROWS), lanes] = acc
        return carry

    lax.fori_loop(0, tm // CONV_ROWS, rows_body, 0)
    v = cv_ref[...]
    mu = jnp.mean(v, axis=-1, keepdims=True)
    var = jnp.mean(jnp.square(v - mu), axis=-1, keepdims=True)
    v = (v - mu) * lax.rsqrt(var + EPS) * lng_ref[...] + lnb_ref[...]
    v = v * jax.nn.sigmoid(v)
    y_ref[...] = (x_ref[...] + jnp.dot(v.astype(BF16), w2_ref[...], preferred_element_type=F32)
                  + b2_ref[...])


def _dwconv(x2, u2, w_dw, b_dw, ln_g, ln_b, w_pw2, b_pw2, batch, seq, tm=1024):
    x3 = x2.reshape(batch, seq, D_MODEL)
    u3 = u2.reshape(batch, seq, D_MODEL)
    hb = tm // HALO
    blk = pl.BlockSpec((None, tm, D_MODEL), lambda b, i: (b, i, 0))
    halo = pl.BlockSpec((None, HALO, D_MODEL), lambda b, i: (b, jnp.maximum(i * hb - 1, 0), 0))
    full = lambda a: pl.BlockSpec(a.shape, lambda b, i: (0,) * a.ndim)
    args = (x3, u3, u3, w_dw, b_dw.reshape(1, -1), ln_g.reshape(1, -1), ln_b.reshape(1, -1),
            w_pw2.astype(BF16), b_pw2.reshape(1, -1))
    out = pl.pallas_call(
        _dwconv_kernel,
        grid=(batch, seq // tm),
        in_specs=[blk, blk, halo] + [full(a) for a in args[3:]],
        out_specs=blk,
        out_shape=jax.ShapeDtypeStruct((batch, seq, D_MODEL), F32),
        scratch_shapes=[pltpu.VMEM((HALO + tm + SUBLANE_PAD, D_MODEL), F32),
                        pltpu.VMEM((tm, D_MODEL), F32)],
        compiler_params=_cparams(2),
        name="conv_dw",
    )(*args)
    return out.reshape(batch * seq, D_MODEL)


def kernel(x, norm_mix, norm_mlp, norm_final, rel_bias, attn_w_in, attn_q_norm, attn_kv_norm, attn_kidx_norm, attn_w_qidx, attn_w_uq, attn_w_uk, attn_w_uv, attn_w_o, conv_w_pw1, conv_b_pw1, conv_w_dw, conv_b_dw, conv_ln_g, conv_ln_b, conv_w_pw2, conv_b_pw2, mlp_w_up, mlp_w_down):
    batch, seq, d = x.shape
    depth = norm_mix.shape[0]
    x2 = x.reshape(batch * seq, d)
    for i in range(depth):
        j = i // 2
        if i % 2 == 0:
            qidx, widx, qlat, kidx, ckv = _proj(
                x2, norm_mix[i], attn_w_in[j], attn_q_norm[j], attn_kv_norm[j], attn_kidx_norm[j],
                attn_w_qidx[j], attn_w_uq[j], attn_w_uk[j])
            olat = _attention(qidx, widx, qlat, kidx, ckv, rel_bias, batch, seq)
            mixer_out = (olat, attn_w_uv[j], attn_w_o[j])
        else:
            mixer_out = None
            u = _glu(x2, norm_mix[i], conv_w_pw1[j], conv_b_pw1[j])
            x2 = _dwconv(x2, u, conv_w_dw[j], conv_b_dw[j], conv_ln_g[j], conv_ln_b[j],
                         conv_w_pw2[j], conv_b_pw2[j], batch, seq)
        last = i == depth - 1
        x2 = _mlp(x2, norm_mlp[i], mlp_w_up[i], mlp_w_down[i], norm_final, final_norm=last,
                  attn_out=mixer_out)
    if depth == 0:
        raise ValueError("depth must be positive")
    return x2.reshape(batch, seq, d)
```

```python
import functools
import math

import jax
import jax.numpy as jnp
from jax import lax
from jax.experimental import pallas as pl
from jax.experimental.pallas import tpu as pltpu

F32 = jnp.float32
BF16 = jnp.bfloat16
I32 = jnp.int32

D_MODEL = 1024
N_HEADS = 16
D_NOPE = 64
D_V = 64
D_CQ = 256
D_CKV = 128
IDX_HEADS = 8
IDX_DIM = 64
TOPK_MAX = 256
CONV_WIDTH = 31
NUM_BUCKETS = 32
MAX_EXACT = 16
MAX_DISTANCE = 128
EPS = 1e-6

LANES = 128
SUBLANES = 8
TQ = 128
SCORE_TILE = 512
ATT_TK = 512
BIAS_TK = 2 * TQ
PAD_KEYS = ATT_TK - TQ
NEG_MASK = -1e30
LOG2E = math.log2(math.e)
FIXED_BITS = 21
BITS_PER_TEST = 1
PREFIX_ROWS = 256
VMEM_LIMIT = 56 * 1024 * 1024


def _cparams(n_axes):
    return pltpu.CompilerParams(dimension_semantics=("arbitrary",) * n_axes,
                                vmem_limit_bytes=VMEM_LIMIT)


def _rms(x, g):
    return x * lax.rsqrt(jnp.mean(x * x, axis=-1, keepdims=True) + EPS) * g


def _proj_kernel(x_ref, g_ref, win_ref, qn_ref, kvn_ref, kin_ref, wqidx_ref, wuq_ref, wukt_ref,
                 qidx_ref, widx_ref, qlat_ref, kidx_ref, ckv_ref):
    h = _rms(x_ref[...], g_ref[...])
    proj = jnp.dot(h.astype(BF16), win_ref[...], preferred_element_type=F32)
    o1, o2, o3 = D_CQ, D_CQ + D_CKV, D_CQ + D_CKV + IDX_DIM
    cq = _rms(proj[:, :o1], qn_ref[...])
    ckv = _rms(proj[:, o1:o2], kvn_ref[...])
    kid = _rms(proj[:, o2:o3], kin_ref[...])
    widx_ref[...] = proj[:, o3:o3 + IDX_HEADS] * (IDX_HEADS ** -0.5)
    ckv_ref[...] = ckv.astype(BF16)
    kidx_ref[...] = kid.astype(BF16)
    cqb = cq.astype(BF16)
    qidx = jnp.dot(cqb, wqidx_ref[...], preferred_element_type=F32) * (IDX_DIM ** -0.5)
    qidx_ref[...] = qidx.astype(BF16)
    qh = jnp.dot(cqb, wuq_ref[...], preferred_element_type=F32).astype(BF16)
    for hp in range(N_HEADS // 2):
        ql = jnp.dot(qh[:, hp * 2 * D_NOPE:(hp + 1) * 2 * D_NOPE], wukt_ref[hp],
                     preferred_element_type=F32) * (D_NOPE ** -0.5 * LOG2E)
        qlat_ref[:, hp * 2 * D_CKV:(hp + 1) * 2 * D_CKV] = ql.astype(BF16)


def _proj(x2, g, w_in, qn, kvn, kin, w_qidx, w_uq, w_uk, tm=1024):
    n = x2.shape[0]
    ncol = w_in.shape[1]
    npad = -ncol % LANES
    win = jnp.pad(w_in, ((0, 0), (0, npad))).astype(BF16)
    wukt = jnp.transpose(w_uk, (1, 2, 0)).reshape(N_HEADS // 2, 2, D_NOPE, D_CKV)
    eye = jnp.eye(2, dtype=w_uk.dtype)
    wukt = (wukt[:, :, :, None, :] * eye[None, :, None, :, None]).reshape(
        N_HEADS // 2, 2 * D_NOPE, 2 * D_CKV).astype(BF16)
    full = lambda a: pl.BlockSpec(a.shape, lambda i: (0,) * a.ndim)
    row = lambda c: pl.BlockSpec((tm, c), lambda i: (i, 0))
    args = (x2, g.reshape(1, -1), win, qn.reshape(1, -1), kvn.reshape(1, -1), kin.reshape(1, -1),
            w_qidx.astype(BF16), w_uq.astype(BF16), wukt)
    return pl.pallas_call(
        _proj_kernel,
        grid=(n // tm,),
        in_specs=[row(D_MODEL)] + [full(a) for a in args[1:]],
        out_specs=[row(IDX_HEADS * IDX_DIM), row(IDX_HEADS), row(N_HEADS * D_CKV), row(IDX_DIM),
                   row(D_CKV)],
        out_shape=[jax.ShapeDtypeStruct((n, IDX_HEADS * IDX_DIM), BF16),
                   jax.ShapeDtypeStruct((n, IDX_HEADS), F32),
                   jax.ShapeDtypeStruct((n, N_HEADS * D_CKV), BF16),
                   jax.ShapeDtypeStruct((n, IDX_DIM), BF16),
                   jax.ShapeDtypeStruct((n, D_CKV), BF16)],
        compiler_params=_cparams(1),
        name="dsa_proj",
    )(*args)


def _attn_kernel(qidx_ref, widxt_ref, qlat_ref, kidx_ref, vaug_ref, bias_ref, tri_ref, olat_ref,
                 keys_ref, maskt_ref, a_ref, acc_ref, m_ref, s_ref, qk_ref, qs_ref, *, k_sel):
    n_rows = N_HEADS * TQ
    q0 = 2 * pl.program_id(1) * TQ

    n_tiles = q0 // SCORE_TILE + 1
    krow = lax.broadcasted_iota(I32, (SCORE_TILE, TQ), 0)
    qpos = [q0 + sub * TQ + lax.broadcasted_iota(I32, (1, TQ), 1) for sub in range(2)]
    trivial = [qp < k_sel for qp in qpos]

    def tile_rows(tt):
        return pl.ds(pl.multiple_of(tt * SCORE_TILE, SCORE_TILE), SCORE_TILE)

    def score_pass(sub):
        for h in range(IDX_HEADS):
            qs_ref[h * TQ:(h + 1) * TQ, :] = qidx_ref[sub * TQ:(sub + 1) * TQ,
                                                      h * IDX_DIM:(h + 1) * IDX_DIM]

        def score_dot(tt, slot):
            rows = tile_rows(jnp.minimum(tt, keys_ref.shape[1] // SCORE_TILE - 1))
            qk_ref[slot] = lax.dot_general(kidx_ref[rows, :], qs_ref[...], (((1,), (1,)), ((), ())),
                                           preferred_element_type=F32)

        def score_keys(tt, slot):
            sc = jnp.zeros((SCORE_TILE, TQ), F32)
            for h in range(IDX_HEADS):
                sc = sc + widxt_ref[sub, h:h + 1, :] * jnp.maximum(
                    qk_ref[slot, :, h * TQ:(h + 1) * TQ], 0.0)
            sc = jnp.where(tt * SCORE_TILE + krow <= qpos[sub], sc, -jnp.inf)
            bits = pltpu.bitcast(sc, I32)
            keys_ref[sub, tile_rows(tt), :] = bits ^ ((bits >> 31) & 0x7FFFFFFF)

        score_dot(0, 0)

        def score_pair(pp, carry):
            tt = 2 * pp
            score_dot(tt + 1, 1)
            score_keys(tt, 0)
            score_dot(tt + 2, 0)
            score_keys(tt + 1, 1)
            return carry

        lax.fori_loop(0, n_tiles // 2, score_pair, 0)

        @pl.when(n_tiles % 2 == 1)
        def _():
            score_keys(n_tiles - 1, 0)

    for sub in range(2):
        score_pass(sub)

    def tile_reduce(x, op):
        return op(x.reshape(SCORE_TILE // SUBLANES, SUBLANES, TQ), axis=0)

    def count_ge(thr):
        def body(tt, acc):
            return tuple(acc[sub] + tile_reduce(
                jnp.where(keys_ref[sub, tile_rows(tt), :] >= thr[sub], 1, 0), jnp.sum)
                for sub in range(2))
        zero = jnp.zeros((SUBLANES, TQ), I32)
        acc = lax.fori_loop(0, n_tiles, body, (zero, zero))
        return tuple(jnp.sum(a, axis=0, keepdims=True) for a in acc)

    def bit_step(step, st):
        trial = tuple(s[0] ^ (jnp.int32(1) << (31 - step)) for s in st)
        cnt = count_ge(trial)
        out = []
        for sub in range(2):
            cand, cntc, cnt_ub = st[sub]
            ok = cnt[sub] >= k_sel
            out.append((jnp.where(ok, trial[sub], cand), jnp.where(ok, cnt[sub], cntc),
                        jnp.where(ok, cnt_ub, cnt[sub])))
        return tuple(out)

    def resolve(step, st, tval, res):
        top = tuple(s[0] | ((jnp.int32(1) << (32 - step)) - 1) for s in st)

        def body(tt, c):
            out = []
            for sub in range(2):
                k = keys_ref[sub, tile_rows(tt), :]
                inb = jnp.logical_and(k >= st[sub][0], k <= top[sub])
                out.append((jnp.minimum(c[sub][0], tile_reduce(jnp.where(inb, k, 2 ** 31 - 1), jnp.min)),
                            jnp.maximum(c[sub][1], tile_reduce(jnp.where(inb, k, -2 ** 31), jnp.max))))
            return tuple(out)

        init = (jnp.full((SUBLANES, TQ), 2 ** 31 - 1, I32), jnp.full((SUBLANES, TQ), -2 ** 31, I32))
        mnmx = lax.fori_loop(0, n_tiles, body, (init, init))
        tvals, ress, n_open = [], [], 0
        for sub in range(2):
            _, cntc, cnt_ub = st[sub]
            mn = jnp.min(mnmx[sub][0], axis=0, keepdims=True)
            mx = jnp.max(mnmx[sub][1], axis=0, keepdims=True)
            known = jnp.logical_or(mn == mx, cnt_ub == k_sel - 1)
            tvals.append(jnp.where(known, mx, tval[sub]))
            ress.append(jnp.where(known, 1, res[sub]))
            settled = jnp.logical_or(jnp.logical_or(ress[sub] > 0, cntc == k_sel), trivial[sub])
            n_open = n_open + jnp.sum(jnp.where(settled, 0, 1))
        return tuple(tvals), tuple(ress), n_open

    st0 = (jnp.full((1, TQ), -2 ** 31, I32), jnp.full((1, TQ), 2 ** 30, I32), jnp.zeros((1, TQ), I32))
    st = lax.fori_loop(0, FIXED_BITS, bit_step, (st0, st0))
    zeros = jnp.zeros((1, TQ), I32)
    tval, res, n_open = resolve(FIXED_BITS, st, (st[0][0], st[1][0]), (zeros, zeros))

    def search_body(c):
        step, st, tval, res, _ = c
        for b in range(BITS_PER_TEST):
            st = bit_step(step + b, st)
        tval, res, n_open = resolve(step + BITS_PER_TEST, st, tval, res)
        return step + BITS_PER_TEST, st, tval, res, n_open

    _, st, tval, res, _ = lax.while_loop(
        lambda c: jnp.logical_and(c[0] < 32, c[4] > 0), search_body,
        (jnp.int32(FIXED_BITS), st, tval, res, n_open))

    thr, tie_take = [], []
    for sub in range(2):
        cand, cntc, cnt_ub = st[sub]
        exact = jnp.logical_or(cntc == k_sel, trivial[sub])
        t = jnp.where(exact, cand, jnp.where(res[sub] > 0, tval[sub], cand))
        thr.append(jnp.where(trivial[sub], -2 ** 31, t))
        tie_take.append(jnp.where(exact, 2.0 ** 30, (k_sel - cnt_ub).astype(F32)))

    def mask_tile(sub, tt, seen):
        k = keys_ref[sub, tile_rows(tt), :]
        idx = tt * SCORE_TILE + krow
        tie = k == thr[sub]
        tie_bf = jnp.where(tie, 1.0, 0.0).astype(BF16)
        wanted = []
        for c in range(SCORE_TILE // PREFIX_ROWS):
            rank = jnp.dot(tri_ref[...], tie_bf[c * PREFIX_ROWS:(c + 1) * PREFIX_ROWS],
                           preferred_element_type=F32)
            wanted.append(rank <= tie_take[sub] - seen)
            seen = seen + rank[PREFIX_ROWS - 1:PREFIX_ROWS, :]
        sel = jnp.logical_or(k > thr[sub], jnp.logical_and(tie, jnp.concatenate(wanted, axis=0)))
        sel = jnp.logical_and(sel, idx <= qpos[sub])
        rows = pl.ds(pl.multiple_of(PAD_KEYS + tt * SCORE_TILE, TQ), SCORE_TILE)
        maskt_ref[sub, rows, :] = jnp.where(sel, 0.0, NEG_MASK).astype(BF16)
        return seen

    def mask_trip(pp, seen):
        return tuple(mask_tile(sub, 2 * pp + 1, mask_tile(sub, 2 * pp, seen[sub])) for sub in range(2))

    zero_f = jnp.zeros((1, TQ), F32)
    seen = lax.fori_loop(0, n_tiles // 2, mask_trip, (zero_f, zero_f))

    @pl.when(n_tiles % 2 == 1)
    def _():
        for sub in range(2):
            mask_tile(sub, n_tiles - 1, seen[sub])

    for sub in range(2):
        maskt_ref[sub, :PAD_KEYS, :] = jnp.full((PAD_KEYS, TQ), NEG_MASK, BF16)

    eye = (lax.broadcasted_iota(I32, (TQ, TQ), 0) == lax.broadcasted_iota(I32, (TQ, TQ), 1))
    eye = jnp.where(eye, 1.0, 0.0).astype(BF16)
    for sub in range(2):
        for h in range(N_HEADS):
            a_ref[sub, h * TQ:(h + 1) * TQ, :D_CKV] = qlat_ref[sub * TQ:(sub + 1) * TQ,
                                                                h * D_CKV:(h + 1) * D_CKV]
            a_ref[sub, h * TQ:(h + 1) * TQ, D_CKV:] = eye
    n_chunks = ATT_TK // LANES
    n_steps = (q0 + TQ + ATT_TK - 1) // ATT_TK

    def key_rows(sub, j):
        r0 = jnp.maximum(q0 + (sub + 1) * TQ + PAD_KEYS - (j + 1) * ATT_TK, 0)
        return pl.ds(pl.multiple_of(r0, TQ), ATT_TK)

    def logits(sub, j):
        rows = key_rows(sub, j)
        rhs = jnp.concatenate([vaug_ref[rows, :D_CKV], maskt_ref[sub, rows, :]], axis=1)
        s_ref[sub] = lax.dot_general(a_ref[sub], rhs, (((1,), (1,)), ((), ())),
                                     preferred_element_type=F32)

    def consume(sub, j, newest=False):
        vk = vaug_ref[key_rows(sub, j), :]
        ps = []
        for h in range(N_HEADS):
            rows = slice(h * TQ, (h + 1) * TQ)
            sh = s_ref[sub, rows, :]
            if newest:
                sh = jnp.concatenate([sh[:, :ATT_TK - BIAS_TK],
                                      sh[:, ATT_TK - BIAS_TK:] + bias_ref[h]], axis=1)
                m_new = jnp.broadcast_to(jnp.max(sh, axis=1, keepdims=True), (TQ, LANES))
            else:
                m_old = m_ref[sub, rows, :]
                m_new = jnp.maximum(m_old, jnp.max(sh, axis=1, keepdims=True))
                alpha = jnp.exp2(m_old - m_new)
                acc_ref[sub, rows, :] = acc_ref[sub, rows, :] * jnp.concatenate([alpha, alpha], axis=1)
            m_ref[sub, rows, :] = m_new
            p = jnp.exp2(sh - jnp.concatenate([m_new] * n_chunks, axis=1))
            ps.append(p.astype(BF16))
        pv = jnp.dot(jnp.concatenate(ps, axis=0), vk, preferred_element_type=F32)
        if newest:
            acc_ref[sub] = pv
        else:
            acc_ref[sub] += pv

    logits(0, 0)

    logits(1, 0)
    consume(0, 0, newest=True)
    logits(0, 1)
    consume(1, 0, newest=True)

    def att_step(j, carry):
        logits(1, j)
        consume(0, j)
        logits(0, j + 1)
        consume(1, j)
        return carry

    lax.fori_loop(1, n_steps, att_step, 0)

    for sub in range(2):
        for h in range(N_HEADS):
            acc = acc_ref[sub, h * TQ:(h + 1) * TQ, :]
            olat_ref[sub * TQ:(sub + 1) * TQ, h * D_CKV:(h + 1) * D_CKV] = (
                acc[:, :D_CKV] / acc[:, D_CKV:]).astype(BF16)


def _t5_bucket(dist):
    n = jnp.maximum(dist, 0)
    nf = jnp.maximum(n, 1).astype(F32)
    large = MAX_EXACT + (jnp.log(nf / MAX_EXACT) / math.log(MAX_DISTANCE / MAX_EXACT)
                         * (NUM_BUCKETS - MAX_EXACT)).astype(I32)
    large = jnp.minimum(large, NUM_BUCKETS - 1)
    return jnp.where(n < MAX_EXACT, n, large)


def _near_bias(rel_bias):
    assert BIAS_TK >= TQ + MAX_DISTANCE - 1
    period = TQ + BIAS_TK
    e = jnp.arange(period)
    e = jnp.where(e < BIAS_TK, e, e - period)
    tab = rel_bias[_t5_bucket(BIAS_TK - TQ - e)] - rel_bias[NUM_BUCKETS - 1]
    flat = jnp.tile(tab, (TQ, 1))[:TQ * (period - 1)]
    near = flat.reshape(TQ, period - 1, N_HEADS)[:, :BIAS_TK]
    return (jnp.transpose(near, (2, 0, 1)) * LOG2E).astype(F32)


def _attention(qidx, widx, qlat, kidx, ckv, rel_bias, batch, seq):
    assert seq % SCORE_TILE == 0 and SCORE_TILE % TQ == 0 and ATT_TK % (2 * TQ) == 0
    k_sel = min(TOPK_MAX, seq // 4)
    nq = seq // TQ
    r3 = lambda a: a.reshape(batch, seq, a.shape[-1])
    widxt = jnp.transpose(widx.reshape(batch, nq, TQ, IDX_HEADS), (0, 1, 3, 2))
    ckv3 = jnp.pad(r3(ckv), ((0, 0), (PAD_KEYS, 0), (0, 0)))
    vaug = jnp.concatenate([ckv3, jnp.ones_like(ckv3)], axis=-1)
    bias = _near_bias(rel_bias)
    tri = jnp.tril(jnp.ones((PREFIX_ROWS, PREFIX_ROWS), BF16))

    qblk = lambda c: pl.BlockSpec((None, 2 * TQ, c), lambda b, i: (b, i, 0))
    qtile = lambda a: pl.BlockSpec((None, 2) + a.shape[2:], lambda b, i: (b, i, 0, 0))
    per_b = lambda a: pl.BlockSpec((None,) + a.shape[1:], lambda b, i: (b,) + (0,) * (a.ndim - 1),
                                   pipeline_mode=pl.Buffered(1))
    n_rows = N_HEADS * TQ
    kidx3 = r3(kidx)
    out = pl.pallas_call(
        functools.partial(_attn_kernel, k_sel=k_sel),
        grid=(batch, nq // 2),
        in_specs=[qblk(IDX_HEADS * IDX_DIM), qtile(widxt), qblk(N_HEADS * D_CKV), per_b(kidx3), per_b(vaug),
                  pl.BlockSpec(bias.shape, lambda b, i: (0, 0, 0),
                               pipeline_mode=pl.Buffered(1)),
                  pl.BlockSpec(tri.shape, lambda b, i: (0, 0))],
        out_specs=qblk(N_HEADS * D_CKV),
        out_shape=jax.ShapeDtypeStruct((batch, seq, N_HEADS * D_CKV), BF16),
        scratch_shapes=[
            pltpu.VMEM((2, seq, TQ), I32),
            pltpu.VMEM((2, seq + PAD_KEYS, TQ), BF16),
            pltpu.VMEM((2, n_rows, 2 * D_CKV), BF16),
            pltpu.VMEM((2, n_rows, 2 * D_CKV), F32),
            pltpu.VMEM((2, n_rows, LANES), F32),
            pltpu.VMEM((2, n_rows, ATT_TK), F32),
            pltpu.VMEM((2, SCORE_TILE, IDX_HEADS * TQ), F32),
            pltpu.VMEM((IDX_HEADS * TQ, IDX_DIM), BF16),
        ],
        compiler_params=_cparams(2),
        name="dsa_attn",
    )(r3(qidx), widxt, r3(qlat), kidx3, vaug, bias, tri)
    return out.reshape(batch * seq, N_HEADS * D_CKV)


def _mlp_kernel(*refs, f_chunk, final_norm, with_attn_out):
    if with_attn_out:
        x_ref, olat_ref, wuv_ref, wo_ref, g_ref, wup_ref, wdn_ref, gf_ref, y_ref, acc_ref = refs
        pair = 2 * D_CKV
        o = [jnp.dot(olat_ref[:, p * pair:(p + 1) * pair], wuv_ref[p], preferred_element_type=F32)
             for p in range(N_HEADS // 2)]
        o = jnp.concatenate(o, axis=1).astype(BF16)
        x = x_ref[...] + jnp.dot(o, wo_ref[...], preferred_element_type=F32)
    else:
        x_ref, g_ref, wup_ref, wdn_ref, gf_ref, y_ref, acc_ref = refs
        x = x_ref[...]
    h = _rms(x, g_ref[...]).astype(BF16)
    d_ff = wup_ref.shape[1]
    for c in range(d_ff // f_chunk):
        u = jnp.dot(h, wup_ref[:, c * f_chunk:(c + 1) * f_chunk], preferred_element_type=F32)
        a = jnp.square(jnp.maximum(u, 0.0)).astype(BF16)
        d = jnp.dot(a, wdn_ref[c * f_chunk:(c + 1) * f_chunk, :], preferred_element_type=F32)
        if c == 0:
            acc_ref[...] = d
        else:
            acc_ref[...] += d
    y = x + acc_ref[...]
    if final_norm:
        y = _rms(y, gf_ref[...])
    y_ref[...] = y


def _mlp(x2, g, w_up, w_down, g_final, final_norm, attn_out=None, tm=512, f_chunk=512):
    n = x2.shape[0]
    const = lambda a: pl.BlockSpec(a.shape, lambda i: (0,) * a.ndim, pipeline_mode=pl.Buffered(1))
    rows = lambda c: pl.BlockSpec((tm, c), lambda i: (i, 0))
    args, specs = [x2], [rows(D_MODEL)]
    if attn_out is not None:
        olat, w_uv, w_o = attn_out
        wuv = jnp.transpose(w_uv, (1, 0, 2)).reshape(N_HEADS // 2, 2, D_CKV, D_V)
        eye = jnp.eye(2, dtype=w_uv.dtype)
        wbd = (wuv[:, :, :, None, :] * eye[None, :, None, :, None]).reshape(
            N_HEADS // 2, 2 * D_CKV, 2 * D_V)
        extra = [olat, wbd.astype(BF16), w_o.astype(BF16)]
        args += extra
        specs += [rows(N_HEADS * D_CKV), const(extra[1]), const(extra[2])]
    weights = [g.reshape(1, -1), w_up.astype(BF16), w_down.astype(BF16), g_final.reshape(1, -1)]
    args += weights
    specs += [const(a) for a in weights]
    return pl.pallas_call(
        functools.partial(_mlp_kernel, f_chunk=f_chunk, final_norm=final_norm,
                          with_attn_out=attn_out is not None),
        grid=(n // tm,),
        in_specs=specs,
        out_specs=rows(D_MODEL),
        out_shape=jax.ShapeDtypeStruct((n, D_MODEL), F32),
        scratch_shapes=[pltpu.VMEM((tm, D_MODEL), F32)],
        compiler_params=_cparams(1),
        name="mlp_final" if final_norm else "mlp",
    )(*args)


def _glu_kernel(x_ref, g_ref, w_ref, b_ref, u_ref):
    h = _rms(x_ref[...], g_ref[...]).astype(BF16)
    u = jnp.dot(h, w_ref[...], preferred_element_type=F32) + b_ref[...]
    d = u.shape[1] // 2
    u_ref[...] = u[:, :d] * jax.nn.sigmoid(u[:, d:])


def _glu(x2, g, w_pw1, b_pw1, tm=1024):
    n = x2.shape[0]
    full = lambda a: pl.BlockSpec(a.shape, lambda i: (0,) * a.ndim)
    args = (x2, g.reshape(1, -1), w_pw1.astype(BF16), b_pw1.reshape(1, -1))
    return pl.pallas_call(
        _glu_kernel,
        grid=(n // tm,),
        in_specs=[pl.BlockSpec((tm, D_MODEL), lambda i: (i, 0))] + [full(a) for a in args[1:]],
        out_specs=pl.BlockSpec((tm, D_MODEL), lambda i: (i, 0)),
        out_shape=jax.ShapeDtypeStruct((n, D_MODEL), F32),
        compiler_params=_cparams(1),
        name="conv_glu",
    )(*args)


HALO = 32
CONV_ROWS = 128
SUBLANE_PAD = 8


def _dwconv_kernel(x_ref, u_ref, halo_ref, wdw_ref, bdw_ref, lng_ref, lnb_ref, w2_ref, b2_ref, y_ref,
                   ext_ref, cv_ref):
    i = pl.program_id(1)
    tm = u_ref.shape[0]
    ext_ref[:HALO, :] = jnp.where(i == 0, 0.0, halo_ref[...])
    ext_ref[HALO:HALO + tm, :] = u_ref[...]
    ext_ref[HALO + tm:, :] = jnp.zeros((SUBLANE_PAD, D_MODEL), F32)
    off = HALO - (CONV_WIDTH - 1)

    def rows_body(r, carry):
        r0 = pl.multiple_of(r * CONV_ROWS, CONV_ROWS)
        for lc in range(D_MODEL // LANES):
            lanes = slice(lc * LANES, (lc + 1) * LANES)
            acc = jnp.broadcast_to(bdw_ref[:, lanes], (CONV_ROWS, LANES))
            for s in range(SUBLANE_PAD):
                part = None
                for j in range(CONV_WIDTH):
                    if (off + j) % SUBLANE_PAD != s:
                        continue
                    rows = pl.ds(pl.multiple_of(r0 + (off + j - s), SUBLANE_PAD),
                                 CONV_ROWS + SUBLANE_PAD)
                    term = wdw_ref[j:j + 1, lanes] * ext_ref[rows, lanes]
                    part = term if part is None else part + term
                if part is not None:
                    acc = acc + part[s:s + CONV_ROWS]
            cv_ref[pl.ds(r0, CONV_ROWS), lanes] = acc
        return carry

    lax.fori_loop(0, tm // CONV_ROWS, rows_body, 0)
    v = cv_ref[...]
    mu = jnp.mean(v, axis=-1, keepdims=True)
    var = jnp.mean(jnp.square(v - mu), axis=-1, keepdims=True)
    v = (v - mu) * lax.rsqrt(var + EPS) * lng_ref[...] + lnb_ref[...]
    v = v * jax.nn.sigmoid(v)
    y_ref[...] = (x_ref[...] + jnp.dot(v.astype(BF16), w2_ref[...], preferred_element_type=F32)
                  + b2_ref[...])


def _dwconv(x2, u2, w_dw, b_dw, ln_g, ln_b, w_pw2, b_pw2, batch, seq, tm=1024):
    x3 = x2.reshape(batch, seq, D_MODEL)
    u3 = u2.reshape(batch, seq, D_MODEL)
    hb = tm // HALO
    blk = pl.BlockSpec((None, tm, D_MODEL), lambda b, i: (b, i, 0))
    halo = pl.BlockSpec((None, HALO, D_MODEL), lambda b, i: (b, jnp.maximum(i * hb - 1, 0), 0))
    full = lambda a: pl.BlockSpec(a.shape, lambda b, i: (0,) * a.ndim)
    args = (x3, u3, u3, w_dw, b_dw.reshape(1, -1), ln_g.reshape(1, -1), ln_b.reshape(1, -1),
            w_pw2.astype(BF16), b_pw2.reshape(1, -1))
    out = pl.pallas_call(
        _dwconv_kernel,
        grid=(batch, seq // tm),
        in_specs=[blk, blk, halo] + [full(a) for a in args[3:]],
        out_specs=blk,
        out_shape=jax.ShapeDtypeStruct((batch, seq, D_MODEL), F32),
        scratch_shapes=[pltpu.VMEM((HALO + tm + SUBLANE_PAD, D_MODEL), F32),
                        pltpu.VMEM((tm, D_MODEL), F32)],
        compiler_params=_cparams(2),
        name="conv_dw",
    )(*args)
    return out.reshape(batch * seq, D_MODEL)


def kernel(x, norm_mix, norm_mlp, norm_final, rel_bias, attn_w_in, attn_q_norm, attn_kv_norm, attn_kidx_norm, attn_w_qidx, attn_w_uq, attn_w_uk, attn_w_uv, attn_w_o, conv_w_pw1, conv_b_pw1, conv_w_dw, conv_b_dw, conv_ln_g, conv_ln_b, conv_w_pw2, conv_b_pw2, mlp_w_up, mlp_w_down):
    batch, seq, d = x.shape
    depth = norm_mix.shape[0]
    x2 = x.reshape(batch * seq, d)
    for i in range(depth):
        j = i // 2
        if i % 2 == 0:
            qidx, widx, qlat, kidx, ckv = _proj(
                x2, norm_mix[i], attn_w_in[j], attn_q_norm[j], attn_kv_norm[j], attn_kidx_norm[j],
                attn_w_qidx[j], attn_w_uq[j], attn_w_uk[j])
            olat = _attention(qidx, widx, qlat, kidx, ckv, rel_bias, batch, seq)
            mixer_out = (olat, attn_w_uv[j], attn_w_o[j])
        else:
            mixer_out = None
            u = _glu(x2, norm_mix[i], conv_w_pw1[j], conv_b_pw1[j])
            x2 = _dwconv(x2, u, conv_w_dw[j], conv_b_dw[j], conv_ln_g[j], conv_ln_b[j],
                         conv_w_pw2[j], conv_b_pw2[j], batch, seq)
        last = i == depth - 1
        x2 = _mlp(x2, norm_mlp[i], mlp_w_up[i], mlp_w_down[i], norm_final, final_norm=last,
                  attn_out=mixer_out)
    if depth == 0:
        raise ValueError("depth must be positive")
    return x2.reshape(batch, seq, d)
```

```python
import functools
import math

import jax
import jax.numpy as jnp
from jax import lax
from jax.experimental import pallas as pl
from jax.experimental.pallas import tpu as pltpu

F32 = jnp.float32
BF16 = jnp.bfloat16
I32 = jnp.int32

D_MODEL = 1024
N_HEADS = 16
D_NOPE = 64
D_V = 64
D_CQ = 256
D_CKV = 128
IDX_HEADS = 8
IDX_DIM = 64
TOPK_MAX = 256
CONV_WIDTH = 31
NUM_BUCKETS = 32
MAX_EXACT = 16
MAX_DISTANCE = 128
EPS = 1e-6

LANES = 128
SUBLANES = 8
TQ = 128
SCORE_TILE = 512
ATT_TK = 512
BIAS_TK = 2 * TQ
PAD_KEYS = ATT_TK - TQ
NEG_MASK = -1e30
LOG2E = math.log2(math.e)
FIXED_BITS = 22
BITS_PER_TEST = 1
PREFIX_ROWS = 128
VMEM_LIMIT = 56 * 1024 * 1024


def _cparams(n_axes):
    return pltpu.CompilerParams(dimension_semantics=("arbitrary",) * n_axes,
                                vmem_limit_bytes=VMEM_LIMIT)


def _rms(x, g):
    return x * lax.rsqrt(jnp.mean(x * x, axis=-1, keepdims=True) + EPS) * g


def _proj_kernel(x_ref, g_ref, win_ref, qn_ref, kvn_ref, kin_ref, wqidx_ref, wuq_ref, wukt_ref,
                 qidx_ref, widx_ref, qlat_ref, kidx_ref, ckv_ref):
    h = _rms(x_ref[...], g_ref[...])
    proj = jnp.dot(h.astype(BF16), win_ref[...], preferred_element_type=F32)
    o1, o2, o3 = D_CQ, D_CQ + D_CKV, D_CQ + D_CKV + IDX_DIM
    cq = _rms(proj[:, :o1], qn_ref[...])
    ckv = _rms(proj[:, o1:o2], kvn_ref[...])
    kid = _rms(proj[:, o2:o3], kin_ref[...])
    widx_ref[...] = proj[:, o3:o3 + IDX_HEADS] * (IDX_HEADS ** -0.5)
    ckv_ref[...] = ckv.astype(BF16)
    kidx_ref[...] = kid.astype(BF16)
    cqb = cq.astype(BF16)
    qidx = jnp.dot(cqb, wqidx_ref[...], preferred_element_type=F32) * (IDX_DIM ** -0.5)
    qidx_ref[...] = qidx.astype(BF16)
    qh = jnp.dot(cqb, wuq_ref[...], preferred_element_type=F32).astype(BF16)
    for hp in range(N_HEADS // 2):
        ql = jnp.dot(qh[:, hp * 2 * D_NOPE:(hp + 1) * 2 * D_NOPE], wukt_ref[hp],
                     preferred_element_type=F32) * (D_NOPE ** -0.5 * LOG2E)
        qlat_ref[:, hp * 2 * D_CKV:(hp + 1) * 2 * D_CKV] = ql.astype(BF16)


def _proj(x2, g, w_in, qn, kvn, kin, w_qidx, w_uq, w_uk, tm=1024):
    n = x2.shape[0]
    ncol = w_in.shape[1]
    npad = -ncol % LANES
    win = jnp.pad(w_in, ((0, 0), (0, npad))).astype(BF16)
    wukt = jnp.transpose(w_uk, (1, 2, 0)).reshape(N_HEADS // 2, 2, D_NOPE, D_CKV)
    eye = jnp.eye(2, dtype=w_uk.dtype)
    wukt = (wukt[:, :, :, None, :] * eye[None, :, None, :, None]).reshape(
        N_HEADS // 2, 2 * D_NOPE, 2 * D_CKV).astype(BF16)
    full = lambda a: pl.BlockSpec(a.shape, lambda i: (0,) * a.ndim)
    row = lambda c: pl.BlockSpec((tm, c), lambda i: (i, 0))
    args = (x2, g.reshape(1, -1), win, qn.reshape(1, -1), kvn.reshape(1, -1), kin.reshape(1, -1),
            w_qidx.astype(BF16), w_uq.astype(BF16), wukt)
    return pl.pallas_call(
        _proj_kernel,
        grid=(n // tm,),
        in_specs=[row(D_MODEL)] + [full(a) for a in args[1:]],
        out_specs=[row(IDX_HEADS * IDX_DIM), row(IDX_HEADS), row(N_HEADS * D_CKV), row(IDX_DIM),
                   row(D_CKV)],
        out_shape=[jax.ShapeDtypeStruct((n, IDX_HEADS * IDX_DIM), BF16),
                   jax.ShapeDtypeStruct((n, IDX_HEADS), F32),
                   jax.ShapeDtypeStruct((n, N_HEADS * D_CKV), BF16),
                   jax.ShapeDtypeStruct((n, IDX_DIM), BF16),
                   jax.ShapeDtypeStruct((n, D_CKV), BF16)],
        compiler_params=_cparams(1),
        name="dsa_proj",
    )(*args)


def _attn_kernel(qidx_ref, widxt_ref, qlat_ref, kidx_ref, vaug_ref, bias_ref, tri_ref, olat_ref,
                 keys_ref, maskt_ref, a_ref, acc_ref, m_ref, s_ref, qk_ref, qs_ref, *, k_sel):
    n_rows = N_HEADS * TQ
    q0 = 2 * pl.program_id(1) * TQ

    n_tiles = q0 // SCORE_TILE + 1
    krow = lax.broadcasted_iota(I32, (SCORE_TILE, TQ), 0)
    qpos = [q0 + sub * TQ + lax.broadcasted_iota(I32, (1, TQ), 1) for sub in range(2)]
    trivial = [qp < k_sel for qp in qpos]

    def tile_rows(tt):
        return pl.ds(pl.multiple_of(tt * SCORE_TILE, SCORE_TILE), SCORE_TILE)

    def score_pass(sub):
        for h in range(IDX_HEADS):
            qs_ref[h * TQ:(h + 1) * TQ, :] = qidx_ref[sub * TQ:(sub + 1) * TQ,
                                                      h * IDX_DIM:(h + 1) * IDX_DIM]

        def score_dot(tt, slot):
            rows = tile_rows(jnp.minimum(tt, keys_ref.shape[1] // SCORE_TILE - 1))
            qk_ref[slot] = lax.dot_general(kidx_ref[rows, :], qs_ref[...], (((1,), (1,)), ((), ())),
                                           preferred_element_type=F32)

        def score_keys(tt, slot):
            sc = jnp.zeros((SCORE_TILE, TQ), F32)
            for h in range(IDX_HEADS):
                sc = sc + widxt_ref[sub, h:h + 1, :] * jnp.maximum(
                    qk_ref[slot, :, h * TQ:(h + 1) * TQ], 0.0)
            sc = jnp.where(tt * SCORE_TILE + krow <= qpos[sub], sc, -jnp.inf)
            bits = pltpu.bitcast(sc, I32)
            keys_ref[sub, tile_rows(tt), :] = bits ^ ((bits >> 31) & 0x7FFFFFFF)

        score_dot(0, 0)

        def score_pair(pp, carry):
            tt = 2 * pp
            score_dot(tt + 1, 1)
            score_keys(tt, 0)
            score_dot(tt + 2, 0)
            score_keys(tt + 1, 1)
            return carry

        lax.fori_loop(0, n_tiles // 2, score_pair, 0)

        @pl.when(n_tiles % 2 == 1)
        def _():
            score_keys(n_tiles - 1, 0)

    for sub in range(2):
        score_pass(sub)

    def tile_reduce(x, op):
        return op(x.reshape(SCORE_TILE // SUBLANES, SUBLANES, TQ), axis=0)

    def count_ge(thr):
        def body(tt, acc):
            return tuple(acc[sub] + tile_reduce(
                jnp.where(keys_ref[sub, tile_rows(tt), :] >= thr[sub], 1, 0), jnp.sum)
                for sub in range(2))
        zero = jnp.zeros((SUBLANES, TQ), I32)
        acc = lax.fori_loop(0, n_tiles, body, (zero, zero))
        return tuple(jnp.sum(a, axis=0, keepdims=True) for a in acc)

    def bit_step(step, st):
        trial = tuple(s[0] ^ (jnp.int32(1) << (31 - step)) for s in st)
        cnt = count_ge(trial)
        out = []
        for sub in range(2):
            cand, cntc, cnt_ub = st[sub]
            ok = cnt[sub] >= k_sel
            out.append((jnp.where(ok, trial[sub], cand), jnp.where(ok, cnt[sub], cntc),
                        jnp.where(ok, cnt_ub, cnt[sub])))
        return tuple(out)

    def resolve(step, st, tval, res):
        top = tuple(s[0] | ((jnp.int32(1) << (32 - step)) - 1) for s in st)

        def body(tt, c):
            out = []
            for sub in range(2):
                k = keys_ref[sub, tile_rows(tt), :]
                inb = jnp.logical_and(k >= st[sub][0], k <= top[sub])
                out.append((jnp.minimum(c[sub][0], tile_reduce(jnp.where(inb, k, 2 ** 31 - 1), jnp.min)),
                            jnp.maximum(c[sub][1], tile_reduce(jnp.where(inb, k, -2 ** 31), jnp.max))))
            return tuple(out)

        init = (jnp.full((SUBLANES, TQ), 2 ** 31 - 1, I32), jnp.full((SUBLANES, TQ), -2 ** 31, I32))
        mnmx = lax.fori_loop(0, n_tiles, body, (init, init))
        tvals, ress, n_open = [], [], 0
        for sub in range(2):
            _, cntc, cnt_ub = st[sub]
            mn = jnp.min(mnmx[sub][0], axis=0, keepdims=True)
            mx = jnp.max(mnmx[sub][1], axis=0, keepdims=True)
            known = jnp.logical_or(mn == mx, cnt_ub == k_sel - 1)
            tvals.append(jnp.where(known, mx, tval[sub]))
            ress.append(jnp.where(known, 1, res[sub]))
            settled = jnp.logical_or(jnp.logical_or(ress[sub] > 0, cntc == k_sel), trivial[sub])
            n_open = n_open + jnp.sum(jnp.where(settled, 0, 1))
        return tuple(tvals), tuple(ress), n_open

    st0 = (jnp.full((1, TQ), -2 ** 31, I32), jnp.full((1, TQ), 2 ** 30, I32), jnp.zeros((1, TQ), I32))
    st = lax.fori_loop(0, FIXED_BITS, bit_step, (st0, st0))
    zeros = jnp.zeros((1, TQ), I32)
    tval, res, n_open = resolve(FIXED_BITS, st, (st[0][0], st[1][0]), (zeros, zeros))

    def search_body(c):
        step, st, tval, res, _ = c
        for b in range(BITS_PER_TEST):
            st = bit_step(step + b, st)
        tval, res, n_open = resolve(step + BITS_PER_TEST, st, tval, res)
        return step + BITS_PER_TEST, st, tval, res, n_open

    _, st, tval, res, _ = lax.while_loop(
        lambda c: jnp.logical_and(c[0] < 32, c[4] > 0), search_body,
        (jnp.int32(FIXED_BITS), st, tval, res, n_open))

    thr, tie_take = [], []
    for sub in range(2):
        cand, cntc, cnt_ub = st[sub]
        exact = jnp.logical_or(cntc == k_sel, trivial[sub])
        t = jnp.where(exact, cand, jnp.where(res[sub] > 0, tval[sub], cand))
        thr.append(jnp.where(trivial[sub], -2 ** 31, t))
        tie_take.append(jnp.where(exact, 2.0 ** 30, (k_sel - cnt_ub).astype(F32)))

    def mask_tile(sub, tt, seen):
        k = keys_ref[sub, tile_rows(tt), :]
        idx = tt * SCORE_TILE + krow
        tie = k == thr[sub]
        tie_bf = jnp.where(tie, 1.0, 0.0).astype(BF16)
        wanted = []
        for c in range(SCORE_TILE // PREFIX_ROWS):
            rank = jnp.dot(tri_ref[...], tie_bf[c * PREFIX_ROWS:(c + 1) * PREFIX_ROWS],
                           preferred_element_type=F32)
            wanted.append(rank <= tie_take[sub] - seen)
            seen = seen + rank[PREFIX_ROWS - 1:PREFIX_ROWS, :]
        sel = jnp.logical_or(k > thr[sub], jnp.logical_and(tie, jnp.concatenate(wanted, axis=0)))
        sel = jnp.logical_and(sel, idx <= qpos[sub])
        rows = pl.ds(pl.multiple_of(PAD_KEYS + tt * SCORE_TILE, TQ), SCORE_TILE)
        maskt_ref[sub, rows, :] = jnp.where(sel, 0.0, NEG_MASK).astype(BF16)
        return seen

    def mask_trip(pp, seen):
        return tuple(mask_tile(sub, 2 * pp + 1, mask_tile(sub, 2 * pp, seen[sub])) for sub in range(2))

    zero_f = jnp.zeros((1, TQ), F32)
    seen = lax.fori_loop(0, n_tiles // 2, mask_trip, (zero_f, zero_f))

    @pl.when(n_tiles % 2 == 1)
    def _():
        for sub in range(2):
            mask_tile(sub, n_tiles - 1, seen[sub])

    for sub in range(2):
        maskt_ref[sub, :PAD_KEYS, :] = jnp.full((PAD_KEYS, TQ), NEG_MASK, BF16)

    eye = (lax.broadcasted_iota(I32, (TQ, TQ), 0) == lax.broadcasted_iota(I32, (TQ, TQ), 1))
    eye = jnp.where(eye, 1.0, 0.0).astype(BF16)
    for sub in range(2):
        for h in range(N_HEADS):
            a_ref[sub, h * TQ:(h + 1) * TQ, :D_CKV] = qlat_ref[sub * TQ:(sub + 1) * TQ,
                                                                h * D_CKV:(h + 1) * D_CKV]
            a_ref[sub, h * TQ:(h + 1) * TQ, D_CKV:] = eye
    n_chunks = ATT_TK // LANES
    n_steps = (q0 + TQ + ATT_TK - 1) // ATT_TK

    def key_rows(sub, j):
        r0 = jnp.maximum(q0 + (sub + 1) * TQ + PAD_KEYS - (j + 1) * ATT_TK, 0)
        return pl.ds(pl.multiple_of(r0, TQ), ATT_TK)

    def logits(sub, j):
        rows = key_rows(sub, j)
        rhs = jnp.concatenate([vaug_ref[rows, :D_CKV], maskt_ref[sub, rows, :]], axis=1)
        s_ref[sub] = lax.dot_general(a_ref[sub], rhs, (((1,), (1,)), ((), ())),
                                     preferred_element_type=F32)

    def consume(sub, j, newest=False):
        vk = vaug_ref[key_rows(sub, j), :]
        ps = []
        for h in range(N_HEADS):
            rows = slice(h * TQ, (h + 1) * TQ)
            sh = s_ref[sub, rows, :]
            if newest:
                sh = jnp.concatenate([sh[:, :ATT_TK - BIAS_TK],
                                      sh[:, ATT_TK - BIAS_TK:] + bias_ref[h]], axis=1)
                m_new = jnp.broadcast_to(jnp.max(sh, axis=1, keepdims=True), (TQ, LANES))
            else:
                m_old = m_ref[sub, rows, :]
                m_new = jnp.maximum(m_old, jnp.max(sh, axis=1, keepdims=True))
                alpha = jnp.exp2(m_old - m_new)
                acc_ref[sub, rows, :] = acc_ref[sub, rows, :] * jnp.concatenate([alpha, alpha], axis=1)
            m_ref[sub, rows, :] = m_new
            p = jnp.exp2(sh - jnp.concatenate([m_new] * n_chunks, axis=1))
            ps.append(p.astype(BF16))
        pv = jnp.dot(jnp.concatenate(ps, axis=0), vk, preferred_element_type=F32)
        if newest:
            acc_ref[sub] = pv
        else:
            acc_ref[sub] += pv

    logits(0, 0)

    logits(1, 0)
    consume(0, 0, newest=True)
    logits(0, 1)
    consume(1, 0, newest=True)

    def att_step(j, carry):
        logits(1, j)
        consume(0, j)
        logits(0, j + 1)
        consume(1, j)
        return carry

    lax.fori_loop(1, n_steps, att_step, 0)

    for sub in range(2):
        for h in range(N_HEADS):
            acc = acc_ref[sub, h * TQ:(h + 1) * TQ, :]
            olat_ref[sub * TQ:(sub + 1) * TQ, h * D_CKV:(h + 1) * D_CKV] = (
                acc[:, :D_CKV] / acc[:, D_CKV:]).astype(BF16)


def _t5_bucket(dist):
    n = jnp.maximum(dist, 0)
    nf = jnp.maximum(n, 1).astype(F32)
    large = MAX_EXACT + (jnp.log(nf / MAX_EXACT) / math.log(MAX_DISTANCE / MAX_EXACT)
                         * (NUM_BUCKETS - MAX_EXACT)).astype(I32)
    large = jnp.minimum(large, NUM_BUCKETS - 1)
    return jnp.where(n < MAX_EXACT, n, large)


def _near_bias(rel_bias):
    assert BIAS_TK >= TQ + MAX_DISTANCE - 1
    period = TQ + BIAS_TK
    e = jnp.arange(period)
    e = jnp.where(e < BIAS_TK, e, e - period)
    tab = rel_bias[_t5_bucket(BIAS_TK - TQ - e)] - rel_bias[NUM_BUCKETS - 1]
    flat = jnp.tile(tab, (TQ, 1))[:TQ * (period - 1)]
    near = flat.reshape(TQ, period - 1, N_HEADS)[:, :BIAS_TK]
    return (jnp.transpose(near, (2, 0, 1)) * LOG2E).astype(F32)


def _attention(qidx, widx, qlat, kidx, ckv, rel_bias, batch, seq):
    assert seq % SCORE_TILE == 0 and SCORE_TILE % TQ == 0 and ATT_TK % (2 * TQ) == 0
    k_sel = min(TOPK_MAX, seq // 4)
    nq = seq // TQ
    r3 = lambda a: a.reshape(batch, seq, a.shape[-1])
    widxt = jnp.transpose(widx.reshape(batch, nq, TQ, IDX_HEADS), (0, 1, 3, 2))
    ckv3 = jnp.pad(r3(ckv), ((0, 0), (PAD_KEYS, 0), (0, 0)))
    vaug = jnp.concatenate([ckv3, jnp.ones_like(ckv3)], axis=-1)
    bias = _near_bias(rel_bias)
    tri = jnp.tril(jnp.ones((PREFIX_ROWS, PREFIX_ROWS), BF16))

    qblk = lambda c: pl.BlockSpec((None, 2 * TQ, c), lambda b, i: (b, i, 0))
    qtile = lambda a: pl.BlockSpec((None, 2) + a.shape[2:], lambda b, i: (b, i, 0, 0))
    per_b = lambda a: pl.BlockSpec((None,) + a.shape[1:], lambda b, i: (b,) + (0,) * (a.ndim - 1),
                                   pipeline_mode=pl.Buffered(1))
    n_rows = N_HEADS * TQ
    kidx3 = r3(kidx)
    out = pl.pallas_call(
        functools.partial(_attn_kernel, k_sel=k_sel),
        grid=(batch, nq // 2),
        in_specs=[qblk(IDX_HEADS * IDX_DIM), qtile(widxt), qblk(N_HEADS * D_CKV), per_b(kidx3), per_b(vaug),
                  pl.BlockSpec(bias.shape, lambda b, i: (0, 0, 0),
                               pipeline_mode=pl.Buffered(1)),
                  pl.BlockSpec(tri.shape, lambda b, i: (0, 0))],
        out_specs=qblk(N_HEADS * D_CKV),
        out_shape=jax.ShapeDtypeStruct((batch, seq, N_HEADS * D_CKV), BF16),
        scratch_shapes=[
            pltpu.VMEM((2, seq, TQ), I32),
            pltpu.VMEM((2, seq + PAD_KEYS, TQ), BF16),
            pltpu.VMEM((2, n_rows, 2 * D_CKV), BF16),
            pltpu.VMEM((2, n_rows, 2 * D_CKV), F32),
            pltpu.VMEM((2, n_rows, LANES), F32),
            pltpu.VMEM((2, n_rows, ATT_TK), F32),
            pltpu.VMEM((2, SCORE_TILE, IDX_HEADS * TQ), F32),
            pltpu.VMEM((IDX_HEADS * TQ, IDX_DIM), BF16),
        ],
        compiler_params=_cparams(2),
        name="dsa_attn",
    )(r3(qidx), widxt, r3(qlat), kidx3, vaug, bias, tri)
    return out.reshape(batch * seq, N_HEADS * D_CKV)


def _mlp_kernel(*refs, f_chunk, final_norm, with_attn_out):
    if with_attn_out:
        x_ref, olat_ref, wuv_ref, wo_ref, g_ref, wup_ref, wdn_ref, gf_ref, y_ref, acc_ref = refs
        pair = 2 * D_CKV
        o = [jnp.dot(olat_ref[:, p * pair:(p + 1) * pair], wuv_ref[p], preferred_element_type=F32)
             for p in range(N_HEADS // 2)]
        o = jnp.concatenate(o, axis=1).astype(BF16)
        x = x_ref[...] + jnp.dot(o, wo_ref[...], preferred_element_type=F32)
    else:
        x_ref, g_ref, wup_ref, wdn_ref, gf_ref, y_ref, acc_ref = refs
        x = x_ref[...]
    h = _rms(x, g_ref[...]).astype(BF16)
    d_ff = wup_ref.shape[1]
    for c in range(d_ff // f_chunk):
        u = jnp.dot(h, wup_ref[:, c * f_chunk:(c + 1) * f_chunk], preferred_element_type=F32)
        a = jnp.square(jnp.maximum(u, 0.0)).astype(BF16)
        d = jnp.dot(a, wdn_ref[c * f_chunk:(c + 1) * f_chunk, :], preferred_element_type=F32)
        if c == 0:
            acc_ref[...] = d
        else:
            acc_ref[...] += d
    y = x + acc_ref[...]
    if final_norm:
        y = _rms(y, gf_ref[...])
    y_ref[...] = y


def _mlp(x2, g, w_up, w_down, g_final, final_norm, attn_out=None, tm=512, f_chunk=512):
    n = x2.shape[0]
    const = lambda a: pl.BlockSpec(a.shape, lambda i: (0,) * a.ndim, pipeline_mode=pl.Buffered(1))
    rows = lambda c: pl.BlockSpec((tm, c), lambda i: (i, 0))
    args, specs = [x2], [rows(D_MODEL)]
    if attn_out is not None:
        olat, w_uv, w_o = attn_out
        wuv = jnp.transpose(w_uv, (1, 0, 2)).reshape(N_HEADS // 2, 2, D_CKV, D_V)
        eye = jnp.eye(2, dtype=w_uv.dtype)
        wbd = (wuv[:, :, :, None, :] * eye[None, :, None, :, None]).reshape(
            N_HEADS // 2, 2 * D_CKV, 2 * D_V)
        extra = [olat, wbd.astype(BF16), w_o.astype(BF16)]
        args += extra
        specs += [rows(N_HEADS * D_CKV), const(extra[1]), const(extra[2])]
    weights = [g.reshape(1, -1), w_up.astype(BF16), w_down.astype(BF16), g_final.reshape(1, -1)]
    args += weights
    specs += [const(a) for a in weights]
    return pl.pallas_call(
        functools.partial(_mlp_kernel, f_chunk=f_chunk, final_norm=final_norm,
                          with_attn_out=attn_out is not None),
        grid=(n // tm,),
        in_specs=specs,
        out_specs=rows(D_MODEL),
        out_shape=jax.ShapeDtypeStruct((n, D_MODEL), F32),
        scratch_shapes=[pltpu.VMEM((tm, D_MODEL), F32)],
        compiler_params=_cparams(1),
        name="mlp_final" if final_norm else "mlp",
    )(*args)


def _glu_kernel(x_ref, g_ref, w_ref, b_ref, u_ref):
    h = _rms(x_ref[...], g_ref[...]).astype(BF16)
    u = jnp.dot(h, w_ref[...], preferred_element_type=F32) + b_ref[...]
    d = u.shape[1] // 2
    u_ref[...] = u[:, :d] * jax.nn.sigmoid(u[:, d:])


def _glu(x2, g, w_pw1, b_pw1, tm=1024):
    n = x2.shape[0]
    full = lambda a: pl.BlockSpec(a.shape, lambda i: (0,) * a.ndim)
    args = (x2, g.reshape(1, -1), w_pw1.astype(BF16), b_pw1.reshape(1, -1))
    return pl.pallas_call(
        _glu_kernel,
        grid=(n // tm,),
        in_specs=[pl.BlockSpec((tm, D_MODEL), lambda i: (i, 0))] + [full(a) for a in args[1:]],
        out_specs=pl.BlockSpec((tm, D_MODEL), lambda i: (i, 0)),
        out_shape=jax.ShapeDtypeStruct((n, D_MODEL), F32),
        compiler_params=_cparams(1),
        name="conv_glu",
    )(*args)


HALO = 32
CONV_ROWS = 128
SUBLANE_PAD = 8


def _dwconv_kernel(x_ref, u_ref, halo_ref, wdw_ref, bdw_ref, lng_ref, lnb_ref, w2_ref, b2_ref, y_ref,
                   ext_ref, cv_ref):
    i = pl.program_id(1)
    tm = u_ref.shape[0]
    ext_ref[:HALO, :] = jnp.where(i == 0, 0.0, halo_ref[...])
    ext_ref[HALO:HALO + tm, :] = u_ref[...]
    ext_ref[HALO + tm:, :] = jnp.zeros((SUBLANE_PAD, D_MODEL), F32)
    off = HALO - (CONV_WIDTH - 1)

    def rows_body(r, carry):
        r0 = pl.multiple_of(r * CONV_ROWS, CONV_ROWS)
        for lc in range(D_MODEL // LANES):
            lanes = slice(lc * LANES, (lc + 1) * LANES)
            acc = jnp.broadcast_to(bdw_ref[:, lanes], (CONV_ROWS, LANES))
            for s in range(SUBLANE_PAD):
                part = None
                for j in range(CONV_WIDTH):
                    if (off + j) % SUBLANE_PAD != s:
                        continue
                    rows = pl.ds(pl.multiple_of(r0 + (off + j - s), SUBLANE_PAD),
                                 CONV_ROWS + SUBLANE_PAD)
                    term = wdw_ref[j:j + 1, lanes] * ext_ref[rows, lanes]
                    part = term if part is None else part + term
                if part is not None:
                    acc = acc + part[s:s + CONV_ROWS]
            cv_ref[pl.ds(r0, CONV_ROWS), lanes] = acc
        return carry

    lax.fori_loop(0, tm // CONV_ROWS, rows_body, 0)
    v = cv_ref[...]
    mu = jnp.mean(v, axis=-1, keepdims=True)
    var = jnp.mean(jnp.square(v - mu), axis=-1, keepdims=True)
    v = (v - mu) * lax.rsqrt(var + EPS) * lng_ref[...] + lnb_ref[...]
    v = v * jax.nn.sigmoid(v)
    y_ref[...] = (x_ref[...] + jnp.dot(v.astype(BF16), w2_ref[...], preferred_element_type=F32)
                  + b2_ref[...])


def _dwconv(x2, u2, w_dw, b_dw, ln_g, ln_b, w_pw2, b_pw2, batch, seq, tm=1024):
    x3 = x2.reshape(batch, seq, D_MODEL)
    u3 = u2.reshape(batch, seq, D_MODEL)
    hb = tm // HALO
    blk = pl.BlockSpec((None, tm, D_MODEL), lambda b, i: (b, i, 0))
    halo = pl.BlockSpec((None, HALO, D_MODEL), lambda b, i: (b, jnp.maximum(i * hb - 1, 0), 0))
    full = lambda a: pl.BlockSpec(a.shape, lambda b, i: (0,) * a.ndim)
    args = (x3, u3, u3, w_dw, b_dw.reshape(1, -1), ln_g.reshape(1, -1), ln_b.reshape(1, -1),
            w_pw2.astype(BF16), b_pw2.reshape(1, -1))
    out = pl.pallas_call(
        _dwconv_kernel,
        grid=(batch, seq // tm),
        in_specs=[blk, blk, halo] + [full(a) for a in args[3:]],
        out_specs=blk,
        out_shape=jax.ShapeDtypeStruct((batch, seq, D_MODEL), F32),
        scratch_shapes=[pltpu.VMEM((HALO + tm + SUBLANE_PAD, D_MODEL), F32),
                        pltpu.VMEM((tm, D_MODEL), F32)],
        compiler_params=_cparams(2),
        name="conv_dw",
    )(*args)
    return out.reshape(batch * seq, D_MODEL)


def kernel(x, norm_mix, norm_mlp, norm_final, rel_bias, attn_w_in, attn_q_norm, attn_kv_norm, attn_kidx_norm, attn_w_qidx, attn_w_uq, attn_w_uk, attn_w_uv, attn_w_o, conv_w_pw1, conv_b_pw1, conv_w_dw, conv_b_dw, conv_ln_g, conv_ln_b, conv_w_pw2, conv_b_pw2, mlp_w_up, mlp_w_down):
    batch, seq, d = x.shape
    depth = norm_mix.shape[0]
    x2 = x.reshape(batch * seq, d)
    for i in range(depth):
        j = i // 2
        if i % 2 == 0:
            qidx, widx, qlat, kidx, ckv = _proj(
                x2, norm_mix[i], attn_w_in[j], attn_q_norm[j], attn_kv_norm[j], attn_kidx_norm[j],
                attn_w_qidx[j], attn_w_uq[j], attn_w_uk[j])
            olat = _attention(qidx, widx, qlat, kidx, ckv, rel_bias, batch, seq)
            mixer_out = (olat, attn_w_uv[j], attn_w_o[j])
        else:
            mixer_out = None
            u = _glu(x2, norm_mix[i], conv_w_pw1[j], conv_b_pw1[j])
            x2 = _dwconv(x2, u, conv_w_dw[j], conv_b_dw[j], conv_ln_g[j], conv_ln_b[j],
                         conv_w_pw2[j], conv_b_pw2[j], batch, seq)
        last = i == depth - 1
        x2 = _mlp(x2, norm_mlp[i], mlp_w_up[i], mlp_w_down[i], norm_final, final_norm=last,
                  attn_out=mixer_out)
    if depth == 0:
        raise ValueError("depth must be positive")
    return x2.reshape(batch, seq, d)
```
